```python
import math
import functools
import jax
import jax.numpy as jnp
from jax import lax
import numpy as np

D_MODEL = 2048
BATCH = 16
SEQ = 2048
DEPTH = 1
DEC_BATCH = 32
DEC_SEQ = 1
PAST_LEN = 16384
PAGE_SIZE = 128

HEAD_DIM = 128
N_HEADS_A = 8
N_KV_A = 4
G_A = N_HEADS_A // N_KV_A
N_HEADS_B = 8
N_KV_B = 4
G_B = N_HEADS_B // N_KV_B
N_IDX_HEADS = 16
IDX_DIM = 64
IDX_WEIGHT_SCALE = (N_IDX_HEADS * IDX_DIM) ** -0.5
DSA_TOPK = 256
MOBA_BLOCK = 256
MOBA_TOPK = 3
N_BUCKETS = 32
MAX_DISTANCE = 128
D_FF = -(-8 * D_MODEL // (3 * 256)) * 256
Q_BLOCK = 128
EPS = 1e-6
NEG = -1e30
F32 = jnp.float32
IN_SPLITS = (N_HEADS_A * HEAD_DIM, N_KV_A * HEAD_DIM, N_KV_A * HEAD_DIM,
             N_IDX_HEADS * IDX_DIM, IDX_DIM, N_IDX_HEADS,
             N_HEADS_B * HEAD_DIM, N_KV_B * HEAD_DIM, N_KV_B * HEAD_DIM,
             D_MODEL, D_MODEL)

kernel_name = 'hybrid_dsa_moba_decoder_step'


def rms_norm(x, g):
    x32 = x.astype(F32)
    y = x32 * lax.rsqrt(jnp.mean(x32 * x32, axis=-1, keepdims=True) + EPS)
    return (y * g.astype(F32)).astype(x.dtype)


def adaln(c, w, b):
    m = (jax.nn.silu(c) @ w + b)[:, None, :]
    return jnp.split(m, 6, axis=-1)


def modulate(h, shift, scale):
    return h * (1 + scale) + shift


def rel_bucket(dist):
    n = jnp.maximum(dist, 0)
    max_exact = N_BUCKETS // 2
    nf = jnp.maximum(n, 1).astype(F32)
    large = max_exact + (jnp.log(nf / max_exact) / math.log(MAX_DISTANCE / max_exact)
                         * (N_BUCKETS - max_exact)).astype(jnp.int32)
    large = jnp.minimum(large, N_BUCKETS - 1)
    return jnp.where(n < max_exact, n, large)


def split_points():
    pts, acc = [], 0
    for w in IN_SPLITS[:-1]:
        acc += w
        pts.append(acc)
    return pts


def mixer_inputs(h, w_in, g_qa, g_ka, g_kidx, g_qb, g_kb):
    lead = h.shape[:-1]
    z = h @ w_in
    qa, ka, va, qi, ki, wi, qb, kb, vb, ga, gb = jnp.split(z, split_points(), axis=-1)
    heads = lambda t, n, d: t.reshape(lead + (n, d))
    qa = rms_norm(heads(qa, N_HEADS_A, HEAD_DIM), g_qa)
    ka = rms_norm(heads(ka, N_KV_A, HEAD_DIM), g_ka)
    va = heads(va, N_KV_A, HEAD_DIM)
    qi = heads(qi, N_IDX_HEADS, IDX_DIM)
    ki = rms_norm(ki, g_kidx)
    wi = wi * IDX_WEIGHT_SCALE
    qb = rms_norm(heads(qb, N_HEADS_B, HEAD_DIM), g_qb)
    kb = rms_norm(heads(kb, N_KV_B, HEAD_DIM), g_kb)
    vb = heads(vb, N_KV_B, HEAD_DIM)
    return qa, ka, va, qi, ki, wi, qb, kb, vb, ga, gb


def indexer_select(qi, wi, kidx, qpos, kpos, k):
    s = jax.nn.relu(jnp.einsum('...qhd,...sd->...qhs', qi, kidx).astype(F32))
    score = jnp.einsum('...qh,...qhs->...qs', wi.astype(F32), s)
    admissible = kpos[None, :] <= qpos[:, None]
    score = jnp.where(admissible, score, NEG)
    _, idx = lax.top_k(score, k)
    valid = idx <= qpos[:, None]
    return idx, valid


def dsa_attend(qa, kg, vg, dist, valid, table):
    lead = qa.shape[:-2]
    qg = qa.reshape(lead + (N_KV_A, G_A, HEAD_DIM))
    logits = jnp.einsum('...kgd,...skd->...kgs', qg, kg).astype(F32) * HEAD_DIM ** -0.5
    bias = table[rel_bucket(dist)].astype(F32)
    bias = jnp.moveaxis(bias.reshape(dist.shape + (N_KV_A, G_A)), -3, -1)
    logits = jnp.where(valid[..., None, None, :], logits + bias, NEG)
    p = jax.nn.softmax(logits, axis=-1).astype(vg.dtype)
    o = jnp.einsum('...kgs,...skd->...kgd', p, vg)
    return o.reshape(lead + (N_HEADS_A * HEAD_DIM,))


def moba_attend(q, qpos, k_seq, v_seq, table):
    n_q = q.shape[0]
    length = k_seq.shape[0]
    n_full = length // MOBA_BLOCK
    scale = HEAD_DIM ** -0.5
    qg = q.reshape(n_q, N_KV_B, G_B, HEAD_DIM)
    cur = qpos // MOBA_BLOCK
    offs = jnp.arange(MOBA_BLOCK)
    own_pos = cur[:, None] * MOBA_BLOCK + offs
    own_valid = own_pos <= qpos[:, None]
    own_rows = jnp.minimum(own_pos, length - 1)
    k_own, v_own = k_seq[own_rows], v_seq[own_rows]
    own_logits = jnp.einsum('qkgd,qskd->qkgs', qg, k_own).astype(F32) * scale
    own_bias = table[rel_bucket(qpos[:, None] - own_pos)].astype(F32)
    own_bias = own_bias.reshape(n_q, MOBA_BLOCK, N_KV_B, G_B).transpose(0, 2, 3, 1)
    own_logits = jnp.where(own_valid[:, None, None, :], own_logits + own_bias, NEG)
    n_sel = min(MOBA_TOPK, n_full)
    if n_sel == 0:
        p_own = jax.nn.softmax(own_logits, axis=-1).astype(v_seq.dtype)
        o = jnp.einsum('qkgs,qskd->qkgd', p_own, v_own)
        return o.reshape(n_q, N_HEADS_B * HEAD_DIM)
    kb = k_seq[: n_full * MOBA_BLOCK].reshape(n_full, MOBA_BLOCK, N_KV_B, HEAD_DIM)
    vb = v_seq[: n_full * MOBA_BLOCK].reshape(n_full, MOBA_BLOCK, N_KV_B, HEAD_DIM)
    kmean = jnp.mean(kb.astype(F32), axis=1)
    gate = jnp.einsum('qkgd,nkd->qkgn', qg.astype(F32), kmean)
    gate = jnp.where(jnp.arange(n_full) < cur[:, None, None, None], gate, NEG)
    _, sel = lax.top_k(gate, n_sel)
    sel_valid = sel < cur[:, None, None, None]
    kvh = jnp.arange(N_KV_B)[None, :, None, None]
    k_sel = kb.transpose(2, 0, 1, 3)[kvh, sel]
    v_sel = vb.transpose(2, 0, 1, 3)[kvh, sel]
    sel_logits = jnp.einsum('qkgd,qkgjsd->qkgjs', qg, k_sel).astype(F32) * scale
    sel_pos = sel[..., None] * MOBA_BLOCK + offs
    head = (jnp.arange(N_KV_B)[:, None] * G_B + jnp.arange(G_B)[None, :])[None, :, :, None, None]
    sel_bias = table[rel_bucket(qpos[:, None, None, None, None] - sel_pos), head].astype(F32)
    sel_logits = jnp.where(sel_valid[..., None], sel_logits + sel_bias, NEG)
    sel_logits = sel_logits.reshape(n_q, N_KV_B, G_B, n_sel * MOBA_BLOCK)
    p = jax.nn.softmax(jnp.concatenate([sel_logits, own_logits], axis=-1), axis=-1).astype(v_seq.dtype)
    p_sel = p[..., : n_sel * MOBA_BLOCK].reshape(n_q, N_KV_B, G_B, n_sel, MOBA_BLOCK)
    p_own = p[..., n_sel * MOBA_BLOCK:]
    o = (jnp.einsum('qkgjs,qkgjsd->qkgd', p_sel, v_sel)
         + jnp.einsum('qkgs,qskd->qkgd', p_own, v_own))
    return o.reshape(n_q, N_HEADS_B * HEAD_DIM)


def prompt_mixers(qa, ka, va, qi, ki, wi, qb, kb, vb, table_a, table_b):
    n_b, n_s = qa.shape[:2]
    nqb = n_s // Q_BLOCK
    k_a = min(DSA_TOPK, n_s // 4)
    pos = jnp.arange(n_s)
    items = lambda t: t.reshape((n_b * nqb, Q_BLOCK) + t.shape[2:])
    b_idx = jnp.repeat(jnp.arange(n_b, dtype=jnp.int32), nqb)
    qpos = jnp.tile(pos.reshape(nqb, Q_BLOCK), (n_b, 1))

    def block(args):
        b, qp, qa_i, qi_i, wi_i, qb_i = args
        idx, valid = indexer_select(qi_i, wi_i, ki[b], qp, pos, k_a)
        ka_b, va_b = ka[b], va[b]
        oa = dsa_attend(qa_i, ka_b[idx], va_b[idx], qp[:, None] - idx, valid, table_a)
        ob = moba_attend(qb_i, qp, kb[b], vb[b], table_b)
        return oa, ob

    oa, ob = lax.map(block, (b_idx, qpos, items(qa), items(qi), items(wi), items(qb)))
    return oa.reshape(n_b, n_s, -1), ob.reshape(n_b, n_s, -1)


def gather_pages(pool, page_table):
    rows = pool[page_table]
    return rows.reshape((page_table.shape[0], -1) + pool.shape[2:])


def gather_rows(pool, new, page_table, pos):
    page = pool.shape[1]
    past_len = page_table.shape[1] * page
    b = jnp.arange(pos.shape[0]).reshape((-1,) + (1,) * (pos.ndim - 1))
    pc = jnp.clip(pos, 0, past_len - 1)
    past_rows = pool[page_table[b, pc // page], pc % page]
    new_rows = new[b, jnp.clip(pos - past_len, 0, new.shape[1] - 1)]
    is_past = (pos < past_len).reshape(pos.shape + (1,) * (pool.ndim - 2))
    return jnp.where(is_past, past_rows, new_rows)


def sample_mixers(qa, ka, va, qi, ki, wi, qb, kb, vb, c_ka, c_va, c_kidx, c_kb, c_vb,
                  page_table, table_a, table_b):
    n_ds = qa.shape[1]
    past_len = page_table.shape[1] * c_ka.shape[1]
    length = past_len + n_ds
    qpos = past_len + jnp.arange(n_ds)
    kpos = jnp.arange(length)
    ki_full = jnp.concatenate([gather_pages(c_kidx, page_table), ki], axis=1)
    idx, valid = indexer_select(qi, wi, ki_full, qpos, kpos, min(DSA_TOPK, length // 4))
    kg = gather_rows(c_ka, ka, page_table, idx)
    vg = gather_rows(c_va, va, page_table, idx)
    oa = dsa_attend(qa, kg, vg, qpos[:, None] - idx, valid, table_a)
    kb_full = jnp.concatenate([gather_pages(c_kb, page_table), kb], axis=1)
    vb_full = jnp.concatenate([gather_pages(c_vb, page_table), vb], axis=1)
    ob = jax.vmap(moba_attend, in_axes=(0, None, 0, 0, None))(qb, qpos, kb_full, vb_full, table_b)
    return oa, ob


def merge_branches(oa, ob, ga, gb, w_pa, w_pb, w_out):
    m = jax.nn.sigmoid(ga) * (oa @ w_pa) + jax.nn.sigmoid(gb) * (ob @ w_pb)
    return m @ w_out


def swiglu(h, w_in, w_out):
    a, u = jnp.split(h @ w_in, 2, axis=-1)
    return (jax.nn.silu(a) * u) @ w_out


def decoder_layer(x, c, w_ada, b_ada, g_norm1, proj, merge, ffn, mixer_fn):
    sh1, sc1, gt1, sh2, sc2, gt2 = adaln(c, w_ada, b_ada)
    qa, ka, va, qi, ki, wi, qb, kb, vb, ga, gb = mixer_inputs(
        modulate(rms_norm(x, g_norm1), sh1, sc1), *proj)
    oa, ob = mixer_fn(qa, ka, va, qi, ki, wi, qb, kb, vb)
    x = x + gt1 * merge_branches(oa, ob, ga, gb, *merge)
    g_norm2, w_f_in, w_f_out = ffn
    x = x + gt2 * swiglu(modulate(rms_norm(x, g_norm2), sh2, sc2), w_f_in, w_f_out)
    return x, (ka, va, ki, kb, vb)


def setup_inputs(seed: int = 0) -> dict:
    key = jax.random.key(seed)
    ks = iter(jax.random.split(key, 32))
    n_pages = PAST_LEN // PAGE_SIZE
    n_pool = (DEC_BATCH * n_pages * 5 + 3) // 4
    d_in = sum(IN_SPLITS)
    nrm = lambda shape, s: jax.random.normal(next(ks), shape, F32) * s
    return {
        'x_prompt': nrm((BATCH, SEQ, D_MODEL), 1.0),
        'x_sample': nrm((DEC_BATCH, DEC_SEQ, D_MODEL), 1.0),
        'cache_ka': nrm((DEPTH, n_pool, PAGE_SIZE, N_KV_A, HEAD_DIM), 1.0),
        'cache_va': nrm((DEPTH, n_pool, PAGE_SIZE, N_KV_A, HEAD_DIM), 1.0),
        'cache_kidx': nrm((DEPTH, n_pool, PAGE_SIZE, IDX_DIM), 1.0),
        'cache_kb': nrm((DEPTH, n_pool, PAGE_SIZE, N_KV_B, HEAD_DIM), 1.0),
        'cache_vb': nrm((DEPTH, n_pool, PAGE_SIZE, N_KV_B, HEAD_DIM), 1.0),
        'page_table': jax.random.permutation(next(ks), n_pool)[: DEC_BATCH * n_pages]
                      .reshape(DEC_BATCH, n_pages).astype(jnp.int32),
        'c_prompt': nrm((BATCH, D_MODEL), 1.0),
        'c_sample': nrm((DEC_BATCH, D_MODEL), 1.0),
        'rel_bias': nrm((N_BUCKETS, N_HEADS_A + N_HEADS_B), 0.5),
        'w_ada': nrm((DEPTH, D_MODEL, 6 * D_MODEL), 0.5 * D_MODEL ** -0.5),
        'b_ada': nrm((DEPTH, 6 * D_MODEL), 0.02),
        'g_norm1': 1.0 + nrm((DEPTH, D_MODEL), 0.02),
        'w_in': nrm((DEPTH, D_MODEL, d_in), D_MODEL ** -0.5),
        'g_qa': 1.0 + nrm((DEPTH, HEAD_DIM), 0.02),
        'g_ka': 1.0 + nrm((DEPTH, HEAD_DIM), 0.02),
        'g_kidx': 1.0 + nrm((DEPTH, IDX_DIM), 0.02),
        'g_qb': 1.0 + nrm((DEPTH, HEAD_DIM), 0.02),
        'g_kb': 1.0 + nrm((DEPTH, HEAD_DIM), 0.02),
        'w_pa': nrm((DEPTH, N_HEADS_A * HEAD_DIM, D_MODEL), (N_HEADS_A * HEAD_DIM) ** -0.5),
        'w_pb': nrm((DEPTH, N_HEADS_B * HEAD_DIM, D_MODEL), (N_HEADS_B * HEAD_DIM) ** -0.5),
        'w_out': nrm((DEPTH, D_MODEL, D_MODEL), D_MODEL ** -0.5),
        'g_norm2': 1.0 + nrm((DEPTH, D_MODEL), 0.02),
        'w_ffn_in': nrm((DEPTH, D_MODEL, 2 * D_FF), D_MODEL ** -0.5),
        'w_ffn_out': nrm((DEPTH, D_FF, D_MODEL), D_FF ** -0.5),
    }


def reference(x_prompt, x_sample, cache_ka, cache_va, cache_kidx, cache_kb, cache_vb, page_table,
              c_prompt, c_sample, rel_bias, w_ada, b_ada, g_norm1, w_in, g_qa, g_ka, g_kidx,
              g_qb, g_kb, w_pa, w_pb, w_out, g_norm2, w_ffn_in, w_ffn_out):
    table_a = rel_bias[:, :N_HEADS_A]
    table_b = rel_bias[:, N_HEADS_A:]
    xp, xs = x_prompt, x_sample
    rows_p, rows_s = [], []
    for l in range(DEPTH):
        proj = (w_in[l], g_qa[l], g_ka[l], g_kidx[l], g_qb[l], g_kb[l])
        merge = (w_pa[l], w_pb[l], w_out[l])
        ffn = (g_norm2[l], w_ffn_in[l], w_ffn_out[l])
        prompt_fn = functools.partial(prompt_mixers, table_a=table_a, table_b=table_b)
        sample_fn = functools.partial(
            sample_mixers, c_ka=cache_ka[l], c_va=cache_va[l], c_kidx=cache_kidx[l],
            c_kb=cache_kb[l], c_vb=cache_vb[l], page_table=page_table,
            table_a=table_a, table_b=table_b)
        xp, kv_p = decoder_layer(xp, c_prompt, w_ada[l], b_ada[l], g_norm1[l], proj, merge, ffn, prompt_fn)
        xs, kv_s = decoder_layer(xs, c_sample, w_ada[l], b_ada[l], g_norm1[l], proj, merge, ffn, sample_fn)
        rows_p.append(kv_p)
        rows_s.append(kv_s)
    new_ka_p, new_va_p, new_kidx_p, new_kb_p, new_vb_p = [jnp.stack(t) for t in zip(*rows_p)]
    new_ka_s, new_va_s, new_kidx_s, new_kb_s, new_vb_s = [jnp.stack(t) for t in zip(*rows_s)]
    return (xp, xs, new_ka_p, new_va_p, new_kidx_p, new_kb_p, new_vb_p,
            new_ka_s, new_va_s, new_kidx_s, new_kb_s, new_vb_s)
```

```python
import functools
import math

import jax
import jax.numpy as jnp
from jax import lax
from jax.experimental import pallas as pl
from jax.experimental.pallas import tpu as pltpu

F32 = jnp.float32
BF16 = jnp.bfloat16
I32 = jnp.int32

HEAD_DIM = 128
N_HEADS = 8
N_KV = 4
GQA = N_HEADS // N_KV
N_IDX_HEADS = 16
IDX_DIM = 64
IDX_WEIGHT_SCALE = (N_IDX_HEADS * IDX_DIM) ** -0.5
DSA_TOPK = 256
MOBA_BLOCK = 256
MOBA_TOPK = 3
N_BUCKETS = 32
MAX_DISTANCE = 128
EPS = 1e-6
NEG = -1e30
ATTN_SCALE = HEAD_DIM ** -0.5

LANE = 128
INT_MIN = -(2 ** 31)
VMEM_LIMIT_V7X = 56 * 1024 * 1024

TQ = 128
TK = 128
KV_W = N_KV * HEAD_DIM
Q_W = N_HEADS * HEAD_DIM


def _cparams(sem):
    return pltpu.CompilerParams(dimension_semantics=sem, vmem_limit_bytes=VMEM_LIMIT_V7X)


def _nt(a, b):
    return lax.dot_general(a, b, (((1,), (1,)), ((), ())), preferred_element_type=F32)


def _dot(a, b):
    return jnp.dot(a, b, preferred_element_type=F32)


def _bucket_lower_bounds():
    max_exact = N_BUCKETS // 2

    def bucket(n):
        if n < max_exact:
            return n
        large = max_exact + int(math.log(n / max_exact) / math.log(MAX_DISTANCE / max_exact)
                                * (N_BUCKETS - max_exact))
        return min(large, N_BUCKETS - 1)

    lows = [None] * N_BUCKETS
    for n in range(0, 4 * MAX_DISTANCE):
        b = bucket(n)
        if lows[b] is None:
            lows[b] = n
    return lows


_BUCKET_LOW = _bucket_lower_bounds()


def _bias_of_distance(dist, table_ref, h):
    out = jnp.full(dist.shape, table_ref[0, h], F32)
    for b in range(1, N_BUCKETS):
        if _BUCKET_LOW[b] is None:
            continue
        out = jnp.where(dist >= _BUCKET_LOW[b], table_ref[b, h], out)
    return out


def _bias_kernel(table_ref, mat_ref, tail_ref, *, tail_w):
    h = pl.program_id(0)
    kl = lax.broadcasted_iota(I32, (TK, TQ), 0)
    ql = lax.broadcasted_iota(I32, (TK, TQ), 1)
    for dd in range(3):
        mat_ref[dd, 0] = _bias_of_distance(ql - kl + dd * TQ, table_ref, h)
    lane = lax.broadcasted_iota(I32, (1, tail_w), 1)
    tail_ref[0] = _bias_of_distance(tail_w - lane, table_ref, h)


def _bias_tables(rel_bias, tail_w):
    n_h = rel_bias.shape[1]
    return pl.pallas_call(
        functools.partial(_bias_kernel, tail_w=tail_w),
        grid=(n_h,),
        in_specs=[pl.BlockSpec(memory_space=pltpu.SMEM)],
        out_specs=[pl.BlockSpec((3, 1, TK, TQ), lambda h: (0, h, 0, 0)),
                   pl.BlockSpec((1, 1, tail_w), lambda h: (h, 0, 0))],
        out_shape=[jax.ShapeDtypeStruct((3, n_h, TK, TQ), F32),
                   jax.ShapeDtypeStruct((n_h, 1, tail_w), F32)],
        compiler_params=_cparams(("arbitrary",)),
        name="bias_tables",
    )(rel_bias)


def _ada_kernel(c_ref, w_ref, b_ref, o_ref):
    c = c_ref[...]
    s = (c / (1.0 + jnp.exp(-c))).astype(BF16)
    o_ref[...] = _dot(s, w_ref[...].astype(BF16)) + b_ref[...]


def _adaln(c, w, b):
    n, d = c.shape
    tn = min(1024, d)
    return pl.pallas_call(
        _ada_kernel,
        grid=(w.shape[1] // tn,),
        in_specs=[pl.BlockSpec((n, d), lambda j: (0, 0)),
                  pl.BlockSpec((d, tn), lambda j: (0, j)),
                  pl.BlockSpec((1, tn), lambda j: (0, j))],
        out_specs=pl.BlockSpec((n, tn), lambda j: (0, j)),
        out_shape=jax.ShapeDtypeStruct((n, w.shape[1]), F32),
        compiler_params=_cparams(("arbitrary",)),
        name="adaln",
    )(c, w, b.reshape(1, -1))


def _row_tile(r):
    return min(r, 512)


def _tok_spec(tr, w):
    return pl.BlockSpec((1, tr, w), lambda g, r: (g, r, 0))


def _mod_spec(mod, tr):
    if mod.shape[1] == 1:
        return pl.BlockSpec((1, 1, mod.shape[2]), lambda g, r: (g, 0, 0))
    return pl.BlockSpec((1, tr, mod.shape[2]), lambda g, r: (g, r, 0))


def _const_spec(shape):
    nd = len(shape)
    return pl.BlockSpec(shape, lambda g, r: (0,) * nd)


def _rms(x, g):
    return x * lax.rsqrt(jnp.mean(x * x, axis=-1, keepdims=True) + EPS) * g


def _normmod_kernel(x_ref, g_ref, sh_ref, sc_ref, o_ref):
    y = _rms(x_ref[0], g_ref[...])
    o_ref[0] = (y * (1.0 + sc_ref[0]) + sh_ref[0]).astype(BF16)


def _normmod(x, g, sh, sc):
    n_g, n_r, d = x.shape
    tr = _row_tile(n_r)
    return pl.pallas_call(
        _normmod_kernel,
        grid=(n_g, n_r // tr),
        in_specs=[_tok_spec(tr, d), _const_spec((1, d)), _mod_spec(sh, tr), _mod_spec(sc, tr)],
        out_specs=_tok_spec(tr, d),
        out_shape=jax.ShapeDtypeStruct((n_g, n_r, d), BF16),
        compiler_params=_cparams(("arbitrary", "arbitrary")),
        name="norm_modulate",
    )(x, g.reshape(1, d), sh, sc)


def _head_norm_store(z, g, o_ref, n, dtype):
    for hh in range(n):
        cols = slice(hh * HEAD_DIM, (hh + 1) * HEAD_DIM)
        o_ref[0, :, cols] = _rms(z[:, cols], g).astype(dtype)


_A_QA = (0, Q_W)
_A_KA = (_A_QA[1], _A_QA[1] + KV_W)
_A_VA = (_A_KA[1], _A_KA[1] + KV_W)
_A_QI = (_A_VA[1], _A_VA[1] + N_IDX_HEADS * IDX_DIM)
_A_MISC = (_A_QI[1], _A_QI[1] + LANE)
_WI_LANE = IDX_DIM


def _proj_a_kernel(h_ref, w_ref, gq_ref, gk_ref, gki_ref,
                   qa_o, ka_o, va_o, qi_o, ki_o, misc_o, kilo_o, kihi_o):
    h = h_ref[0]
    _head_norm_store(_dot(h, w_ref[:, _A_QA[0]:_A_QA[1]]), gq_ref[...], qa_o, N_HEADS, BF16)
    _head_norm_store(_dot(h, w_ref[:, _A_KA[0]:_A_KA[1]]), gk_ref[...], ka_o, N_KV, F32)
    va_o[0] = _dot(h, w_ref[:, _A_VA[0]:_A_VA[1]])
    qi_o[0] = _dot(h, w_ref[:, _A_QI[0]:_A_QI[1]]).astype(BF16)
    z = _dot(h, w_ref[:, _A_MISC[0]:_A_MISC[1]])
    is_ki = lax.broadcasted_iota(I32, z.shape, 1) < IDX_DIM
    ssq = jnp.sum(jnp.where(is_ki, z * z, 0.0), axis=-1, keepdims=True) * (1.0 / IDX_DIM)
    kin = jnp.where(is_ki, z * lax.rsqrt(ssq + EPS) * gki_ref[...], 0.0)
    ki_o[0] = kin[:, :IDX_DIM]
    misc_o[0] = jnp.where(is_ki, kin, z * IDX_WEIGHT_SCALE)
    kilo_o[0] = kin.astype(BF16)
    kihi_o[0] = pltpu.roll(kin, IDX_DIM, axis=1).astype(BF16)


def _proj_a(h, w_a, g_qa, g_ka, g_kidx_pad):
    n_g, n_r, d = h.shape
    tr = _row_tile(n_r)
    widths = [(Q_W, BF16), (KV_W, F32), (KV_W, F32), (N_IDX_HEADS * IDX_DIM, BF16),
              (IDX_DIM, F32), (LANE, F32), (LANE, BF16), (LANE, BF16)]
    return pl.pallas_call(
        _proj_a_kernel,
        grid=(n_g, n_r // tr),
        in_specs=[_tok_spec(tr, d), _const_spec(w_a.shape), _const_spec((1, HEAD_DIM)),
                  _const_spec((1, HEAD_DIM)), _const_spec((1, LANE))],
        out_specs=[_tok_spec(tr, w) for w, _ in widths],
        out_shape=[jax.ShapeDtypeStruct((n_g, n_r, w), dt) for w, dt in widths],
        compiler_params=_cparams(("arbitrary", "arbitrary")),
        name="proj_dsa",
    )(h, w_a, g_qa.reshape(1, -1), g_ka.reshape(1, -1), g_kidx_pad)


def _proj_b_kernel(h_ref, w_ref, gq_ref, gk_ref, qb_o, kb_o, vb_o):
    h = h_ref[0]
    _head_norm_store(_dot(h, w_ref[:, 0:Q_W]), gq_ref[...], qb_o, N_HEADS, BF16)
    _head_norm_store(_dot(h, w_ref[:, Q_W:Q_W + KV_W]), gk_ref[...], kb_o, N_KV, F32)
    vb_o[0] = _dot(h, w_ref[:, Q_W + KV_W:Q_W + 2 * KV_W])


def _proj_b(h, w_b, g_qb, g_kb):
    n_g, n_r, d = h.shape
    tr = _row_tile(n_r)
    widths = [(Q_W, BF16), (KV_W, F32), (KV_W, F32)]
    return pl.pallas_call(
        _proj_b_kernel,
        grid=(n_g, n_r // tr),
        in_specs=[_tok_spec(tr, d), _const_spec(w_b.shape), _const_spec((1, HEAD_DIM)),
                  _const_spec((1, HEAD_DIM))],
        out_specs=[_tok_spec(tr, w) for w, _ in widths],
        out_shape=[jax.ShapeDtypeStruct((n_g, n_r, w), dt) for w, dt in widths],
        compiler_params=_cparams(("arbitrary", "arbitrary")),
        name="proj_moba",
    )(h, w_b, g_qb.reshape(1, -1), g_kb.reshape(1, -1))


def _sigmoid(x):
    return 1.0 / (1.0 + jnp.exp(-x))


def _gate_merge_kernel(h_ref, oa_ref, ob_ref, wga_ref, wgb_ref, wpa_ref, wpb_ref, m_o):
    h = h_ref[0]
    m = (_sigmoid(_dot(h, wga_ref[...])) * _dot(oa_ref[0], wpa_ref[...])
         + _sigmoid(_dot(h, wgb_ref[...])) * _dot(ob_ref[0], wpb_ref[...]))
    m_o[0] = m.astype(BF16)


def _gate_merge(h, oa, ob, w_ga, w_gb, w_pa, w_pb):
    n_g, n_r, d = h.shape
    tr = _row_tile(n_r)
    tn = d // 2
    tok = lambda w: pl.BlockSpec((1, tr, w), lambda c, g, r: (g, r, 0))
    col = lambda k: pl.BlockSpec((k, tn), lambda c, g, r: (0, c))
    return pl.pallas_call(
        _gate_merge_kernel,
        grid=(d // tn, n_g, n_r // tr),
        in_specs=[tok(d), tok(Q_W), tok(Q_W), col(d), col(d), col(Q_W), col(Q_W)],
        out_specs=pl.BlockSpec((1, tr, tn), lambda c, g, r: (g, r, c)),
        out_shape=jax.ShapeDtypeStruct((n_g, n_r, d), BF16),
        compiler_params=_cparams(("arbitrary", "arbitrary", "arbitrary")),
        name="gate_merge",
    )(h, oa, ob, w_ga, w_gb, w_pa, w_pb)


def _out_proj_kernel(m_ref, x_ref, w_ref, gt_ref, g2_ref, sh_ref, sc_ref, x2_o, h2_o):
    x2 = x_ref[0] + gt_ref[0] * _dot(m_ref[0], w_ref[...])
    x2_o[0] = x2
    h2_o[0] = (_rms(x2, g2_ref[...]) * (1.0 + sc_ref[0]) + sh_ref[0]).astype(BF16)


def _out_proj(m, x, w_out, gt, g2, sh, sc):
    n_g, n_r, d = x.shape
    tr = _row_tile(n_r)
    return pl.pallas_call(
        _out_proj_kernel,
        grid=(n_g, n_r // tr),
        in_specs=[_tok_spec(tr, d), _tok_spec(tr, d), _const_spec(w_out.shape), _mod_spec(gt, tr),
                  _const_spec((1, d)), _mod_spec(sh, tr), _mod_spec(sc, tr)],
        out_specs=[_tok_spec(tr, d), _tok_spec(tr, d)],
        out_shape=[jax.ShapeDtypeStruct((n_g, n_r, d), F32),
                   jax.ShapeDtypeStruct((n_g, n_r, d), BF16)],
        compiler_params=_cparams(("arbitrary", "arbitrary")),
        name="out_proj",
    )(m, x, w_out, gt, g2.reshape(1, d), sh, sc)


def _ffn_kernel(h_ref, x_ref, wa_ref, wu_ref, wo_ref, gt_ref, y_o, acc_s):
    j = pl.program_id(2)

    @pl.when(j == 0)
    def _():
        acc_s[...] = jnp.zeros_like(acc_s)

    h = h_ref[0]
    a = _dot(h, wa_ref[...])
    u = _dot(h, wu_ref[...])
    act = (a * _sigmoid(a) * u).astype(BF16)
    acc_s[...] += _dot(act, wo_ref[...])

    @pl.when(j == pl.num_programs(2) - 1)
    def _():
        y_o[0] = x_ref[0] + gt_ref[0] * acc_s[...]


def _ffn(h2, x2, w_in, w_out, gt):
    n_g, n_r, d = x2.shape
    d_ff = w_out.shape[0]
    tr = _row_tile(n_r)
    tf = 512 if d_ff % 512 == 0 else 256
    n_f = d_ff // tf
    tok = lambda w: pl.BlockSpec((1, tr, w), lambda g, r, j: (g, r, 0))
    if gt.shape[1] == 1:
        gt_spec = pl.BlockSpec((1, 1, d), lambda g, r, j: (g, 0, 0))
    else:
        gt_spec = pl.BlockSpec((1, tr, d), lambda g, r, j: (g, r, 0))
    return pl.pallas_call(
        _ffn_kernel,
        grid=(n_g, n_r // tr, n_f),
        in_specs=[tok(d), tok(d),
                  pl.BlockSpec((d, tf), lambda g, r, j: (0, j)),
                  pl.BlockSpec((d, tf), lambda g, r, j: (0, j + n_f)),
                  pl.BlockSpec((tf, d), lambda g, r, j: (j, 0)),
                  gt_spec],
        out_specs=tok(d),
        out_shape=jax.ShapeDtypeStruct((n_g, n_r, d), F32),
        scratch_shapes=[pltpu.VMEM((tr, d), F32)],
        compiler_params=_cparams(("arbitrary", "arbitrary", "arbitrary")),
        name="swiglu_ffn",
    )(h2, x2, w_in, w_in, w_out, gt)


def _sortable(x):
    bits = pltpu.bitcast(x, I32)
    return bits ^ (lax.shift_right_arithmetic(bits, 31) & 0x7FFFFFFF)


def _stage_kv(k_ref, v_ref, k_s, vt_s, n_chunks):
    def body(c, carry):
        rows = pl.ds(pl.multiple_of(c * TK, TK), TK)
        k_s[rows, :] = k_ref[0, rows, :].astype(BF16)
        v = v_ref[0, rows, :]
        for kh in range(N_KV):
            cols = slice(kh * HEAD_DIM, (kh + 1) * HEAD_DIM)
            vt_s[c, cols, :] = v[:, cols].T.astype(BF16)
        return carry
    lax.fori_loop(0, n_chunks, body, 0)


def _softmax_init(m_s, l_s, acc_s):
    m_s[...] = jnp.full(m_s.shape, NEG, F32)
    l_s[...] = jnp.zeros(l_s.shape, F32)
    acc_s[...] = jnp.zeros(acc_s.shape, F32)


def _attend_chunk(c, dd, mask_of_head, q_ref, k_s, vt_s, bias_ref, m_s, l_s, acc_s):
    rows = pl.ds(pl.multiple_of(c * TK, TK), TK)
    for kh in range(N_KV):
        cols = slice(kh * HEAD_DIM, (kh + 1) * HEAD_DIM)
        kc = k_s[rows, cols]
        vt = vt_s[c, cols, :]
        for g in range(GQA):
            h = kh * GQA + g
            mk = mask_of_head(h)
            qh = q_ref[0, :, h * HEAD_DIM:(h + 1) * HEAD_DIM]
            lg = _nt(kc, qh) * ATTN_SCALE + bias_ref[dd, h]
            lg = jnp.where(mk, lg, NEG)
            m_old = m_s[h:h + 1, :]
            m_new = jnp.maximum(m_old, jnp.max(lg, axis=0, keepdims=True))
            alpha = jnp.exp(m_old - m_new)
            p = jnp.where(mk, jnp.exp(lg - m_new), 0.0)
            l_s[h:h + 1, :] = alpha * l_s[h:h + 1, :] + jnp.sum(p, axis=0, keepdims=True)
            acc_s[h] = alpha * acc_s[h] + _dot(vt, p.astype(BF16))
            m_s[h:h + 1, :] = m_new


def _attend_finish(o_ref, l_s, acc_s):
    for h in range(N_HEADS):
        o = acc_s[h] * (1.0 / l_s[h:h + 1, :])
        o_ref[0, :, h * HEAD_DIM:(h + 1) * HEAD_DIM] = o.T.astype(BF16)


def _dsa_kernel(qi_ref, misc_ref, kilo_ref, kihi_ref, qa_ref, ka_ref, va_ref, bias_ref, o_ref,
                k_s, vt_s, sk_s, w_s, t_s, m_s, l_s, acc_s, *, n_chunks, top_k, pos_bits):
    i = pl.program_id(1)
    kl = lax.broadcasted_iota(I32, (TK, TQ), 0)
    ql = lax.broadcasted_iota(I32, (TK, TQ), 1)
    qpos = i * TQ + lax.broadcasted_iota(I32, (1, TQ), 1)

    @pl.when(i == 0)
    def _():
        _stage_kv(ka_ref, va_ref, k_s, vt_s, n_chunks)

    w_s[...] = misc_ref[0].T

    def score_body(c, carry):
        rows = pl.ds(pl.multiple_of(c * TK, TK), TK)
        k_lo = kilo_ref[0, rows, :]
        k_hi = kihi_ref[0, rows, :]
        acc = jnp.zeros((TK, TQ), F32)
        for p in range(N_IDX_HEADS // 2):
            qp = qi_ref[0, :, p * LANE:(p + 1) * LANE]
            r0 = _WI_LANE + 2 * p
            acc = acc + w_s[r0:r0 + 1, :] * jnp.maximum(_nt(k_lo, qp), 0.0)
            acc = acc + w_s[r0 + 1:r0 + 2, :] * jnp.maximum(_nt(k_hi, qp), 0.0)
        admissible = (c * TK + kl) <= (i * TQ + ql)
        sk_s[c] = jnp.where(admissible, _sortable(acc), INT_MIN)
        return carry
    lax.fori_loop(0, i + 1, score_body, 0)

    def count(pred):
        def body(c, acc):
            return acc + jnp.where(pred(c, sk_s[c]), 1.0, 0.0)
        acc = lax.fori_loop(0, i + 1, body, jnp.zeros((TK, TQ), F32))
        return jnp.sum(acc, axis=0, keepdims=True)

    kf = float(top_k)
    t_s[...] = jnp.full((1, TQ), INT_MIN + 1, I32)

    @pl.when((i + 1) * TQ > top_k)
    def _():
        zero = jnp.zeros((1, TQ), I32)
        t0 = jnp.where(count(lambda c, s: s >= zero) >= kf, zero, INT_MIN)

        def bit_body(s, t):
            cand = t + lax.shift_left(jnp.int32(1), 30 - s)
            return jnp.where(count(lambda c, sk: sk >= cand) >= kf, cand, t)
        t = lax.fori_loop(0, 31, bit_body, t0)
        few = qpos + 1 <= top_k
        t = jnp.where(few, INT_MIN + 1, t)
        t_s[...] = t

        need = kf - count(lambda c, sk: sk > t)
        n_eq = count(lambda c, sk: sk == t)
        tied = jnp.max(jnp.where(jnp.logical_and(n_eq > need, jnp.logical_not(few)), 1.0, 0.0))

        @pl.when(tied > 0.0)
        def _():
            def pos_body(s, lo):
                cand = lo + lax.shift_left(jnp.int32(1), pos_bits - 1 - s)
                below = count(lambda c, sk: jnp.logical_and(sk == t, c * TK + kl < cand))
                return jnp.where(below < need, cand, lo)
            last = lax.fori_loop(0, pos_bits, pos_body, jnp.zeros((1, TQ), I32))

            def drop_body(c, carry):
                sk = sk_s[c]
                drop = jnp.logical_and(sk == t, c * TK + kl > last)
                sk_s[c] = jnp.where(drop, INT_MIN, sk)
                return carry
            lax.fori_loop(0, i + 1, drop_body, 0)

    _softmax_init(m_s, l_s, acc_s)

    def attend_body(c, carry):
        mk = sk_s[c] >= t_s[...]
        _attend_chunk(c, jnp.minimum(i - c, 2), lambda h: mk, qa_ref, k_s, vt_s, bias_ref,
                      m_s, l_s, acc_s)
        return carry
    lax.fori_loop(0, i + 1, attend_body, 0)
    _attend_finish(o_ref, l_s, acc_s)


def _attn_scratch(n_chunks):
    return [pltpu.VMEM((n_chunks * TK, KV_W), BF16),
            pltpu.VMEM((n_chunks, KV_W, TK), BF16)]


def _softmax_scratch():
    return [pltpu.VMEM((N_HEADS, TQ), F32), pltpu.VMEM((N_HEADS, TQ), F32),
            pltpu.VMEM((N_HEADS, HEAD_DIM, TQ), F32)]


def _dsa_prompt(qi, misc, kilo, kihi, qa, ka, va, bias):
    n_b, n_s, _ = qa.shape
    n_chunks = n_s // TK
    top_k = min(DSA_TOPK, n_s // 4)
    qblk = lambda w: pl.BlockSpec((1, TQ, w), lambda b, i: (b, i, 0))
    seq = lambda w: pl.BlockSpec((1, n_s, w), lambda b, i: (b, 0, 0))
    return pl.pallas_call(
        functools.partial(_dsa_kernel, n_chunks=n_chunks, top_k=top_k,
                          pos_bits=max(1, (n_s - 1).bit_length())),
        grid=(n_b, n_s // TQ),
        in_specs=[qblk(N_IDX_HEADS * IDX_DIM), qblk(LANE), seq(LANE), seq(LANE), qblk(Q_W),
                  seq(KV_W), seq(KV_W),
                  pl.BlockSpec((3, N_HEADS, TK, TQ), lambda b, i: (0, 0, 0, 0))],
        out_specs=qblk(Q_W),
        out_shape=jax.ShapeDtypeStruct((n_b, n_s, Q_W), BF16),
        scratch_shapes=_attn_scratch(n_chunks) + [
            pltpu.VMEM((n_chunks, TK, TQ), I32),
            pltpu.VMEM((LANE, TQ), F32),
            pltpu.VMEM((1, TQ), I32),
        ] + _softmax_scratch(),
        compiler_params=_cparams(("arbitrary", "arbitrary")),
        name="dsa_prompt",
    )(qi, misc, kilo, kihi, qa, ka, va, bias)


def _split_hi_lo(x):
    hi = x.astype(BF16)
    return hi, (x - hi.astype(F32)).astype(BF16)


def _moba_kernel(qb_ref, kb_ref, vb_ref, bias_ref, o_ref,
                 k_s, vt_s, kmean_s, sel_s, m_s, l_s, acc_s, *, n_chunks, n_blocks, n_sel):
    i = pl.program_id(1)
    cpb = MOBA_BLOCK // TK
    cur = (i * TQ) // MOBA_BLOCK
    kl = lax.broadcasted_iota(I32, (TK, TQ), 0)
    ql = lax.broadcasted_iota(I32, (TK, TQ), 1)

    @pl.when(i == 0)
    def _():
        _stage_kv(kb_ref, vb_ref, k_s, vt_s, n_chunks)
        for n in range(n_blocks):
            blk = kb_ref[0, n * MOBA_BLOCK:(n + 1) * MOBA_BLOCK, :]
            kmean_s[n:n + 1, :] = jnp.sum(blk, axis=0, keepdims=True) * (1.0 / MOBA_BLOCK)

    blk_id = lax.broadcasted_iota(I32, (n_blocks, TQ), 0)
    for kh in range(N_KV):
        cols = slice(kh * HEAD_DIM, (kh + 1) * HEAD_DIM)
        km_hi, km_lo = _split_hi_lo(kmean_s[:, cols])
        for g in range(GQA):
            h = kh * GQA + g
            qh = qb_ref[0, :, h * HEAD_DIM:(h + 1) * HEAD_DIM]
            gate = _nt(km_hi, qh) + _nt(km_lo, qh)
            gate = jnp.where(blk_id < cur, gate, NEG)
            for n in range(n_blocks):
                gn = gate[n:n + 1, :]
                beats = jnp.logical_or(gate > gn, jnp.logical_and(gate == gn, blk_id < n))
                rank = jnp.sum(jnp.where(beats, 1.0, 0.0), axis=0, keepdims=True)
                keep = jnp.where(rank < n_sel, 1.0, 0.0)
                sel_s[h, n:n + 1, :] = jnp.where(n < cur, keep, 0.0)

    _softmax_init(m_s, l_s, acc_s)

    def attend_body(c, carry):
        blk = c // cpb
        causal = (c * TK + kl) <= (i * TQ + ql)

        def mask_of_head(h):
            row = jnp.where(blk == cur, 1.0, sel_s[h, pl.ds(blk, 1), :])
            return jnp.where(causal, jnp.broadcast_to(row, (TK, TQ)), 0.0) > 0.0
        _attend_chunk(c, jnp.minimum(i - c, 2), mask_of_head, qb_ref, k_s, vt_s, bias_ref,
                      m_s, l_s, acc_s)
        return carry
    lax.fori_loop(0, i + 1, attend_body, 0)
    _attend_finish(o_ref, l_s, acc_s)


def _moba_prompt(qb, kb, vb, bias):
    n_b, n_s, _ = qb.shape
    n_chunks = n_s // TK
    n_blocks = n_s // MOBA_BLOCK
    qblk = lambda w: pl.BlockSpec((1, TQ, w), lambda b, i: (b, i, 0))
    seq = lambda w: pl.BlockSpec((1, n_s, w), lambda b, i: (b, 0, 0))
    return pl.pallas_call(
        functools.partial(_moba_kernel, n_chunks=n_chunks, n_blocks=n_blocks,
                          n_sel=min(MOBA_TOPK, n_blocks)),
        grid=(n_b, n_s // TQ),
        in_specs=[qblk(Q_W), seq(KV_W), seq(KV_W),
                  pl.BlockSpec((3, N_HEADS, TK, TQ), lambda b, i: (0, 0, 0, 0))],
        out_specs=qblk(Q_W),
        out_shape=jax.ShapeDtypeStruct((n_b, n_s, Q_W), BF16),
        scratch_shapes=_attn_scratch(n_chunks) + [
            pltpu.VMEM((n_blocks, KV_W), F32),
            pltpu.VMEM((N_HEADS, n_blocks, TQ), F32),
        ] + _softmax_scratch(),
        compiler_params=_cparams(("arbitrary", "arbitrary")),
        name="moba_prompt",
    )(qb, kb, vb, bias)


SPAN = 2048


def _page_copies(pt_ref, b, first_page, n_pages, pool_ref, buf, sem, page):
    return [pltpu.make_async_copy(pool_ref.at[pt_ref[b, first_page + p]],
                                  buf.at[pl.ds(p * page, page)], sem)
            for p in range(n_pages)]


def _sidx_kernel(pt_ref, qi_ref, wi_ref, kin_ref, pool_ref, mask_o, new_o,
                 xbuf, sem, s_s, *, n_spans, page, top_k):
    b = pl.program_id(0)
    n_b = pl.num_programs(0)
    pps = SPAN // page
    slot = lax.rem(b, 2)

    def copies(bb, sl):
        out = []
        for j in range(n_spans):
            out += _page_copies(pt_ref, bb, j * pps, pps, pool_ref, xbuf.at[sl, j], sem.at[sl], page)
        return out

    @pl.when(b == 0)
    def _():
        for cp in copies(b, slot):
            cp.start()

    @pl.when(b + 1 < n_b)
    def _():
        for cp in copies(b + 1, 1 - slot):
            cp.start()

    for cp in copies(b, slot):
        cp.wait()

    q = qi_ref[0]
    w = wi_ref[0]
    for j in range(n_spans):
        x = xbuf[slot, j].astype(BF16)
        s = jnp.sum(w * jnp.maximum(_nt(q, x), 0.0), axis=0, keepdims=True)
        s_s[j:j + 1, :] = _sortable(s + 0.0)
    kin = kin_ref[0].astype(BF16).astype(F32)
    s_new = jnp.sum(q.astype(F32) * kin, axis=1, keepdims=True)
    s_new = jnp.sum(w * jnp.maximum(s_new, 0.0), axis=0, keepdims=True)
    key_new = _sortable(s_new + 0.0)
    sk = s_s[...]

    def count(pred_past, pred_new):
        c = jnp.sum(jnp.where(pred_past, 1.0, 0.0), axis=1, keepdims=True)
        return jnp.sum(c, axis=0, keepdims=True) + jnp.where(pred_new, 1.0, 0.0)

    kf = float(top_k)
    zero = jnp.zeros((1, 1), I32)
    t0 = jnp.where(count(sk >= zero, key_new >= zero) >= kf, zero, INT_MIN)

    def bit_body(s, t):
        cand = t + lax.shift_left(jnp.int32(1), 30 - s)
        return jnp.where(count(sk >= cand, key_new >= cand) >= kf, cand, t)
    t = lax.fori_loop(0, 31, bit_body, t0)

    need = kf - count(sk > t, key_new > t)
    pos = (lax.broadcasted_iota(I32, sk.shape, 0) * SPAN + lax.broadcasted_iota(I32, sk.shape, 1))
    eq = sk == t
    pos_bits = max(1, (n_spans * SPAN - 1).bit_length())

    def pos_body(s, lo):
        cand = lo + lax.shift_left(jnp.int32(1), pos_bits - 1 - s)
        below = count(jnp.logical_and(eq, pos < cand), False)
        return jnp.where(below < need, cand, lo)
    last = lax.fori_loop(0, pos_bits, pos_body, jnp.zeros((1, 1), I32))
    n_eq_past = count(eq, False)
    sel = jnp.logical_or(sk > t, jnp.logical_and(eq, pos <= last))
    mask_o[0] = jnp.where(sel, 1.0, 0.0)
    new_sel = jnp.logical_or(key_new > t, jnp.logical_and(key_new == t, n_eq_past < need))
    new_o[0] = jnp.broadcast_to(jnp.where(new_sel, 1.0, 0.0), (1, LANE))


def _sample_index(page_table, qi, wi, ki_new, pool):
    n_b, n_pages = page_table.shape
    page = pool.shape[1]
    n_spans = n_pages * page // SPAN
    top_k = min(DSA_TOPK, (n_pages * page + 1) // 4)
    grid_spec = pltpu.PrefetchScalarGridSpec(
        num_scalar_prefetch=1,
        grid=(n_b,),
        in_specs=[pl.BlockSpec((1, N_IDX_HEADS, IDX_DIM), lambda b, pt: (b, 0, 0)),
                  pl.BlockSpec((1, N_IDX_HEADS, 1), lambda b, pt: (b, 0, 0)),
                  pl.BlockSpec((1, 1, IDX_DIM), lambda b, pt: (b, 0, 0)),
                  pl.BlockSpec(memory_space=pl.ANY)],
        out_specs=[pl.BlockSpec((1, n_spans, SPAN), lambda b, pt: (b, 0, 0)),
                   pl.BlockSpec((1, 1, LANE), lambda b, pt: (b, 0, 0))],
        scratch_shapes=[pltpu.VMEM((2, n_spans, SPAN, IDX_DIM), F32),
                        pltpu.SemaphoreType.DMA((2,)),
                        pltpu.VMEM((n_spans, SPAN), I32)],
    )
    return pl.pallas_call(
        functools.partial(_sidx_kernel, n_spans=n_spans, page=page, top_k=top_k),
        grid_spec=grid_spec,
        out_shape=[jax.ShapeDtypeStruct((n_b, n_spans, SPAN), F32),
                   jax.ShapeDtypeStruct((n_b, 1, LANE), F32)],
        compiler_params=_cparams(("arbitrary",)),
        name="sample_indexer",
    )(page_table, qi, wi, ki_new, pool)


def _rows_per_kv(x):
    return jnp.concatenate(
        [jnp.broadcast_to(x[:, kh * HEAD_DIM:(kh + 1) * HEAD_DIM], (GQA, HEAD_DIM))
         for kh in range(N_KV)], axis=0)


def _pick_kv_rows(parts):
    row_kv = lax.broadcasted_iota(I32, parts[0].shape, 0) // GQA
    out = parts[0]
    for kh in range(1, N_KV):
        out = jnp.where(row_kv == kh, parts[kh], out)
    return out


def _sdsa_kernel(pt_ref, mask_ref, new_ref, q_ref, kn_ref, vn_ref, tail_ref, far_ref, b0_ref,
                 kpool_ref, vpool_ref, o_ref, kbuf, vbuf, sem, m_s, l_s, acc_s, *, n_spans, page):
    b = pl.program_id(0)
    j = pl.program_id(1)
    n_b = pl.num_programs(0)
    pps = SPAN // page
    step = b * n_spans + j
    slot = lax.rem(step, 2)

    def copies(bb, jj, sl):
        return (_page_copies(pt_ref, bb, jj * pps, pps, kpool_ref, kbuf.at[sl], sem.at[0, sl], page)
                + _page_copies(pt_ref, bb, jj * pps, pps, vpool_ref, vbuf.at[sl], sem.at[1, sl], page))

    @pl.when(step == 0)
    def _():
        for cp in copies(b, j, slot):
            cp.start()

    @pl.when(step + 1 < n_b * n_spans)
    def _():
        nxt = step + 1
        for cp in copies(nxt // n_spans, lax.rem(nxt, n_spans), 1 - slot):
            cp.start()

    q = q_ref[0]

    @pl.when(j == 0)
    def _():
        qf = q.astype(F32)
        kn = _rows_per_kv(kn_ref[0].astype(BF16).astype(F32))
        lg = jnp.sum(qf * kn, axis=1, keepdims=True) * ATTN_SCALE + b0_ref[...]
        keep = new_ref[0][:, 0:1] > 0.0
        m_s[...] = jnp.where(keep, lg, NEG)
        l_s[...] = jnp.where(keep, jnp.ones_like(lg), 0.0)
        vn = _rows_per_kv(vn_ref[0].astype(BF16).astype(F32))
        acc_s[...] = jnp.where(keep, vn, 0.0)

    for cp in copies(b, j, slot):
        cp.wait()

    kc = kbuf[slot].astype(BF16)
    vc = vbuf[slot].astype(BF16)
    lg = _pick_kv_rows([_nt(q, kc[:, kh * HEAD_DIM:(kh + 1) * HEAD_DIM]) for kh in range(N_KV)])
    bias = jnp.where(j == n_spans - 1, tail_ref[...], far_ref[...])
    mk = mask_ref[0, pl.ds(j, 1), :] > 0.0
    lg = jnp.where(mk, lg * ATTN_SCALE + bias, NEG)
    m_old = m_s[...]
    m_new = jnp.maximum(m_old, jnp.max(lg, axis=1, keepdims=True))
    alpha = jnp.exp(m_old - m_new)
    p = jnp.where(mk, jnp.exp(lg - m_new), 0.0)
    l_s[...] = alpha * l_s[...] + jnp.sum(p, axis=1, keepdims=True)
    pb = p.astype(BF16)
    pv = _pick_kv_rows([_dot(pb, vc[:, kh * HEAD_DIM:(kh + 1) * HEAD_DIM]) for kh in range(N_KV)])
    acc_s[...] = alpha * acc_s[...] + pv
    m_s[...] = m_new

    @pl.when(j == n_spans - 1)
    def _():
        o_ref[0] = (acc_s[...] * (1.0 / l_s[...])).astype(BF16)


def _sample_dsa(page_table, mask, new_sel, qa, ka_new, va_new, tail, far, bias0, k_pool, v_pool):
    n_b, n_pages = page_table.shape
    page = k_pool.shape[1]
    n_spans = n_pages * page // SPAN
    per_b = lambda shape: pl.BlockSpec((1,) + shape, lambda b, j, pt: (b, 0, 0))
    const = lambda shape: pl.BlockSpec(shape, lambda b, j, pt: (0, 0))
    grid_spec = pltpu.PrefetchScalarGridSpec(
        num_scalar_prefetch=1,
        grid=(n_b, n_spans),
        in_specs=[per_b((n_spans, SPAN)), per_b((1, LANE)), per_b((N_HEADS, HEAD_DIM)),
                  per_b((1, KV_W)), per_b((1, KV_W)),
                  const((N_HEADS, SPAN)), const((N_HEADS, 1)), const((N_HEADS, 1)),
                  pl.BlockSpec(memory_space=pl.ANY), pl.BlockSpec(memory_space=pl.ANY)],
        out_specs=per_b((N_HEADS, HEAD_DIM)),
        scratch_shapes=[pltpu.VMEM((2, SPAN, KV_W), F32), pltpu.VMEM((2, SPAN, KV_W), F32),
                        pltpu.SemaphoreType.DMA((2, 2)),
                        pltpu.VMEM((N_HEADS, 1), F32), pltpu.VMEM((N_HEADS, 1), F32),
                        pltpu.VMEM((N_HEADS, HEAD_DIM), F32)],
    )
    return pl.pallas_call(
        functools.partial(_sdsa_kernel, n_spans=n_spans, page=page),
        grid_spec=grid_spec,
        out_shape=jax.ShapeDtypeStruct((n_b, N_HEADS, HEAD_DIM), BF16),
        compiler_params=_cparams(("arbitrary", "arbitrary")),
        name="sample_dsa",
    )(page_table, mask, new_sel, qa, ka_new, va_new, tail, far, bias0, k_pool, v_pool)


def _smoba_gate_kernel(pt_ref, q_ref, kpool_ref, sel_o, kbuf, sem, kmean_s, *, n_spans, page, n_sel):
    b = pl.program_id(0)
    j = pl.program_id(1)
    n_b = pl.num_programs(0)
    pps = SPAN // page
    bps = SPAN // MOBA_BLOCK
    n_blocks = n_spans * bps
    step = b * n_spans + j
    slot = lax.rem(step, 2)

    def copies(bb, jj, sl):
        return _page_copies(pt_ref, bb, jj * pps, pps, kpool_ref, kbuf.at[sl], sem.at[sl], page)

    @pl.when(step == 0)
    def _():
        for cp in copies(b, j, slot):
            cp.start()

    @pl.when(step + 1 < n_b * n_spans)
    def _():
        nxt = step + 1
        for cp in copies(nxt // n_spans, lax.rem(nxt, n_spans), 1 - slot):
            cp.start()

    for cp in copies(b, j, slot):
        cp.wait()

    for n in range(bps):
        blk = kbuf[slot, n * MOBA_BLOCK:(n + 1) * MOBA_BLOCK, :]
        kmean_s[pl.ds(j * bps + n, 1), :] = jnp.sum(blk, axis=0, keepdims=True) * (1.0 / MOBA_BLOCK)

    @pl.when(j == n_spans - 1)
    def _():
        q = q_ref[0]
        parts = []
        for kh in range(N_KV):
            km_hi, km_lo = _split_hi_lo(kmean_s[:, kh * HEAD_DIM:(kh + 1) * HEAD_DIM])
            parts.append(_nt(q, km_hi) + _nt(q, km_lo))
        gate = _pick_kv_rows(parts)
        lane = lax.broadcasted_iota(I32, gate.shape, 1).astype(F32)
        out_lane = lax.broadcasted_iota(I32, (N_HEADS, LANE), 1)
        picked = jnp.zeros((N_HEADS, LANE), F32)
        for r in range(n_sel):
            best = jnp.max(gate, axis=1, keepdims=True)
            idx = jnp.min(jnp.where(gate == best, lane, float(n_blocks)), axis=1, keepdims=True)
            picked = jnp.where(out_lane == r, idx, picked)
            gate = jnp.where(lane == idx, -jnp.inf, gate)
        sel_o[0] = picked.astype(I32)


def _sample_moba_gate(page_table, qb, k_pool):
    n_b, n_pages = page_table.shape
    page = k_pool.shape[1]
    n_spans = n_pages * page // SPAN
    n_blocks = (n_pages * page + 1) // MOBA_BLOCK
    grid_spec = pltpu.PrefetchScalarGridSpec(
        num_scalar_prefetch=1,
        grid=(n_b, n_spans),
        in_specs=[pl.BlockSpec((1, N_HEADS, HEAD_DIM), lambda b, j, pt: (b, 0, 0)),
                  pl.BlockSpec(memory_space=pl.ANY)],
        out_specs=pl.BlockSpec((1, N_HEADS, LANE), lambda b, j, pt: (b, 0, 0)),
        scratch_shapes=[pltpu.VMEM((2, SPAN, KV_W), F32), pltpu.SemaphoreType.DMA((2,)),
                        pltpu.VMEM((n_blocks, KV_W), F32)],
    )
    return pl.pallas_call(
        functools.partial(_smoba_gate_kernel, n_spans=n_spans, page=page,
                          n_sel=min(MOBA_TOPK, n_blocks)),
        grid_spec=grid_spec,
        out_shape=jax.ShapeDtypeStruct((n_b, N_HEADS, LANE), I32),
        compiler_params=_cparams(("arbitrary", "arbitrary")),
        name="sample_moba_gate",
    )(page_table, qb, k_pool)


def _smoba_attn_kernel(pt_ref, sel_ref, selv_ref, q_ref, kn_ref, vn_ref, tail_ref, far_ref, b0_ref,
                       kpool_ref, vpool_ref, o_ref, kbuf, vbuf, sem, *, n_sel, page, n_blocks):
    b = pl.program_id(0)
    n_b = pl.num_programs(0)
    ppb = MOBA_BLOCK // page
    slot = lax.rem(b, 2)
    n_keys = n_sel * MOBA_BLOCK

    def copies(bb, sl):
        out = []
        for h in range(N_HEADS):
            cols = pl.ds((h // GQA) * HEAD_DIM, HEAD_DIM)
            for r in range(n_sel):
                blk = sel_ref[(bb * N_HEADS + h) * n_sel + r]
                for p in range(ppb):
                    pg = pt_ref[bb, blk * ppb + p]
                    rows = pl.ds((r * ppb + p) * page, page)
                    out.append(pltpu.make_async_copy(kpool_ref.at[pg, :, cols],
                                                     kbuf.at[sl, h, rows], sem.at[0, sl]))
                    out.append(pltpu.make_async_copy(vpool_ref.at[pg, :, cols],
                                                     vbuf.at[sl, h, rows], sem.at[1, sl]))
        return out

    @pl.when(b == 0)
    def _():
        for cp in copies(b, slot):
            cp.start()

    @pl.when(b + 1 < n_b)
    def _():
        for cp in copies(b + 1, 1 - slot):
            cp.start()

    for cp in copies(b, slot):
        cp.wait()

    q = q_ref[0]
    row = lax.broadcasted_iota(I32, (N_HEADS, n_keys), 0)
    lg = jnp.zeros((N_HEADS, n_keys), F32)
    for h in range(N_HEADS):
        lg = jnp.where(row == h, _nt(q, kbuf[slot, h].astype(BF16)), lg)
    selv = selv_ref[0]
    tail = tail_ref[...]
    far = far_ref[...]
    bias = jnp.concatenate(
        [jnp.where(selv[:, r:r + 1] == n_blocks - 1, tail, far) for r in range(n_sel)], axis=1)
    lg = lg * ATTN_SCALE + bias
    qf = q.astype(F32)
    kn = _rows_per_kv(kn_ref[0].astype(BF16).astype(F32))
    lg_new = jnp.sum(qf * kn, axis=1, keepdims=True) * ATTN_SCALE + b0_ref[...]
    m = jnp.maximum(jnp.max(lg, axis=1, keepdims=True), lg_new)
    p = jnp.exp(lg - m)
    p_new = jnp.exp(lg_new - m)
    denom = jnp.sum(p, axis=1, keepdims=True) + p_new
    pb = p.astype(BF16)
    acc = p_new.astype(BF16).astype(F32) * _rows_per_kv(vn_ref[0].astype(BF16).astype(F32))
    row_o = lax.broadcasted_iota(I32, (N_HEADS, HEAD_DIM), 0)
    for h in range(N_HEADS):
        acc = acc + jnp.where(row_o == h, _dot(pb, vbuf[slot, h].astype(BF16)), 0.0)
    o_ref[0] = (acc * (1.0 / denom)).astype(BF16)


def _sample_moba_attn(page_table, sel, qb, kb_new, vb_new, tail, far, bias0, k_pool, v_pool):
    n_b, n_pages = page_table.shape
    page = k_pool.shape[1]
    n_blocks = (n_pages * page + 1) // MOBA_BLOCK
    n_sel = min(MOBA_TOPK, n_blocks)
    sel_flat = sel[:, :, :n_sel].reshape(-1)
    per_b = lambda shape: pl.BlockSpec((1,) + shape, lambda b, pt, sf: (b, 0, 0))
    const = lambda shape: pl.BlockSpec(shape, lambda b, pt, sf: (0, 0))
    grid_spec = pltpu.PrefetchScalarGridSpec(
        num_scalar_prefetch=2,
        grid=(n_b,),
        in_specs=[per_b((N_HEADS, LANE)), per_b((N_HEADS, HEAD_DIM)), per_b((1, KV_W)),
                  per_b((1, KV_W)), const((N_HEADS, MOBA_BLOCK)), const((N_HEADS, 1)),
                  const((N_HEADS, 1)),
                  pl.BlockSpec(memory_space=pl.ANY), pl.BlockSpec(memory_space=pl.ANY)],
        out_specs=per_b((N_HEADS, HEAD_DIM)),
        scratch_shapes=[pltpu.VMEM((2, N_HEADS, n_sel * MOBA_BLOCK, HEAD_DIM), F32),
                        pltpu.VMEM((2, N_HEADS, n_sel * MOBA_BLOCK, HEAD_DIM), F32),
                        pltpu.SemaphoreType.DMA((2, 2))],
    )
    return pl.pallas_call(
        functools.partial(_smoba_attn_kernel, n_sel=n_sel, page=page, n_blocks=n_blocks),
        grid_spec=grid_spec,
        out_shape=jax.ShapeDtypeStruct((n_b, N_HEADS, HEAD_DIM), BF16),
        compiler_params=_cparams(("arbitrary",)),
        name="sample_moba_attn",
    )(page_table, sel_flat, sel, qb, kb_new, vb_new, tail, far, bias0, k_pool, v_pool)


def _split_w_in(w_in):
    widths = (Q_W, KV_W, KV_W, N_IDX_HEADS * IDX_DIM, IDX_DIM, N_IDX_HEADS, Q_W, KV_W, KV_W,
              w_in.shape[0], w_in.shape[0])
    pts = [0]
    for w in widths:
        pts.append(pts[-1] + w)
    qa, ka, va, qi, ki, wi, qb, kb, vb, ga, gb = [w_in[:, pts[n]:pts[n + 1]] for n in range(11)]
    pad = jnp.zeros((w_in.shape[0], LANE - IDX_DIM - N_IDX_HEADS), w_in.dtype)
    w_a = jnp.concatenate([qa, ka, va, qi, ki, wi, pad], axis=1).astype(BF16)
    w_b = jnp.concatenate([qb, kb, vb], axis=1).astype(BF16)
    return w_a, w_b, ga.astype(BF16), gb.astype(BF16)


def kernel(x_prompt, x_sample, cache_ka, cache_va, cache_kidx, cache_kb, cache_vb, page_table,
           c_prompt, c_sample, rel_bias, w_ada, b_ada, g_norm1, w_in, g_qa, g_ka, g_kidx, g_qb,
           g_kb, w_pa, w_pb, w_out, g_norm2, w_ffn_in, w_ffn_out):
    depth = w_in.shape[0]
    n_b, n_s, d = x_prompt.shape
    n_db, n_ds, _ = x_sample.shape
    page = cache_ka.shape[2]
    past_len = page_table.shape[1] * page
    assert n_ds == 1, "the decode kernels handle one new token per sequence"
    assert n_s % MOBA_BLOCK == 0 and past_len % SPAN == 0 and SPAN % page == 0
    assert MOBA_BLOCK % page == 0 and past_len >= 4 * DSA_TOPK

    bias_mat, bias_tail = _bias_tables(rel_bias, SPAN)
    mat_a, mat_b = bias_mat[:, :N_HEADS], bias_mat[:, N_HEADS:]
    tail = bias_tail[:, 0, :]
    tail_a, tail_b = tail[:N_HEADS], tail[N_HEADS:, SPAN - MOBA_BLOCK:]
    far = rel_bias[N_BUCKETS - 1].reshape(-1, 1)
    bias0 = rel_bias[0].reshape(-1, 1)

    xp = x_prompt
    xs = x_sample.reshape(1, n_db, d)
    c_all = jnp.concatenate([c_prompt, c_sample], axis=0)
    rows_p, rows_s = [], []
    for l in range(depth):
        mod = _adaln(c_all, w_ada[l], b_ada[l])
        mod_p = [m.reshape(n_b, 1, d) for m in jnp.split(mod[:n_b], 6, axis=-1)]
        mod_s = [m.reshape(1, n_db, d) for m in jnp.split(mod[n_b:], 6, axis=-1)]
        w_a, w_b, w_ga, w_gb = _split_w_in(w_in[l])
        w_pa_l, w_pb_l, w_out_l = w_pa[l].astype(BF16), w_pb[l].astype(BF16), w_out[l].astype(BF16)
        w_fi, w_fo = w_ffn_in[l].astype(BF16), w_ffn_out[l].astype(BF16)
        g_kidx_pad = jnp.pad(g_kidx[l], (0, LANE - IDX_DIM)).reshape(1, LANE)

        def project(x, mods):
            h = _normmod(x, g_norm1[l], mods[0], mods[1])
            return h, _proj_a(h, w_a, g_qa[l], g_ka[l], g_kidx_pad), _proj_b(h, w_b, g_qb[l], g_kb[l])

        def finish(x, h, oa, ob, mods):
            m = _gate_merge(h, oa, ob, w_ga, w_gb, w_pa_l, w_pb_l)
            x2, h2 = _out_proj(m, x, w_out_l, mods[2], g_norm2[l], mods[3], mods[4])
            return _ffn(h2, x2, w_fi, w_fo, mods[5])

        h, (qa, ka, va, qi, ki, misc, kilo, kihi), (qb, kb, vb) = project(xp, mod_p)
        oa = _dsa_prompt(qi, misc, kilo, kihi, qa, ka, va, mat_a)
        ob = _moba_prompt(qb, kb, vb, mat_b)
        xp = finish(xp, h, oa, ob, mod_p)
        rows_p.append((ka.reshape(n_b, n_s, N_KV, HEAD_DIM), va.reshape(n_b, n_s, N_KV, HEAD_DIM),
                       ki, kb.reshape(n_b, n_s, N_KV, HEAD_DIM), vb.reshape(n_b, n_s, N_KV, HEAD_DIM)))

        h, (qa, ka, va, qi, ki, misc, kilo, kihi), (qb, kb, vb) = project(xs, mod_s)
        tok = lambda t: t.reshape(n_db, 1, t.shape[-1])
        heads = lambda t: t.reshape(n_db, N_HEADS, HEAD_DIM)
        pool = lambda c: c[l].reshape(c.shape[1], page, -1)
        wi = misc[0, :, _WI_LANE:_WI_LANE + N_IDX_HEADS].reshape(n_db, N_IDX_HEADS, 1)
        mask, new_sel = _sample_index(page_table, qi.reshape(n_db, N_IDX_HEADS, IDX_DIM), wi,
                                      tok(ki[0]), cache_kidx[l])
        oa = _sample_dsa(page_table, mask, new_sel, heads(qa[0]), tok(ka[0]), tok(va[0]),
                         tail_a, far[:N_HEADS], bias0[:N_HEADS], pool(cache_ka), pool(cache_va))
        sel = _sample_moba_gate(page_table, heads(qb[0]), pool(cache_kb))
        ob = _sample_moba_attn(page_table, sel, heads(qb[0]), tok(kb[0]), tok(vb[0]),
                               tail_b, far[N_HEADS:], bias0[N_HEADS:], pool(cache_kb), pool(cache_vb))
        xs = finish(xs, h, oa.reshape(1, n_db, Q_W), ob.reshape(1, n_db, Q_W), mod_s)
        kv4 = lambda t: t.reshape(n_db, 1, N_KV, HEAD_DIM)
        rows_s.append((kv4(ka[0]), kv4(va[0]), ki.reshape(n_db, 1, IDX_DIM), kv4(kb[0]), kv4(vb[0])))

    outs_p = [jnp.stack(t) for t in zip(*rows_p)]
    outs_s = [jnp.stack(t) for t in zip(*rows_s)]
    return (xp, xs.reshape(n_db, n_ds, d), *outs_p, *outs_s)
```

```python
import functools
import math

import jax
import jax.numpy as jnp
from jax import lax
from jax.experimental import pallas as pl
from jax.experimental.pallas import tpu as pltpu

F32 = jnp.float32
BF16 = jnp.bfloat16
I32 = jnp.int32

HEAD_DIM = 128
N_HEADS = 8
N_KV = 4
GQA = N_HEADS // N_KV
N_IDX_HEADS = 16
IDX_DIM = 64
IDX_WEIGHT_SCALE = (N_IDX_HEADS * IDX_DIM) ** -0.5
DSA_TOPK = 256
MOBA_BLOCK = 256
MOBA_TOPK = 3
N_BUCKETS = 32
MAX_DISTANCE = 128
EPS = 1e-6
NEG = -1e30
ATTN_SCALE = HEAD_DIM ** -0.5

SUBLANE = 8
LANE = 128
INT_MIN = -(2 ** 31)
VMEM_LIMIT_V7X = 56 * 1024 * 1024

TQ = 128
TK = 128
TK2 = MOBA_BLOCK
CPP = TK2 // TK
KV_W = N_KV * HEAD_DIM
Q_W = N_HEADS * HEAD_DIM


def _cparams(sem):
    return pltpu.CompilerParams(dimension_semantics=sem, vmem_limit_bytes=VMEM_LIMIT_V7X)


def _nt(a, b):
    return lax.dot_general(a, b, (((1,), (1,)), ((), ())), preferred_element_type=F32)


def _dot(a, b):
    return jnp.dot(a, b, preferred_element_type=F32)


def _bucket_lower_bounds():
    max_exact = N_BUCKETS // 2

    def bucket(n):
        if n < max_exact:
            return n
        large = max_exact + int(math.log(n / max_exact) / math.log(MAX_DISTANCE / max_exact)
                                * (N_BUCKETS - max_exact))
        return min(large, N_BUCKETS - 1)

    lows = [None] * N_BUCKETS
    for n in range(0, 4 * MAX_DISTANCE):
        b = bucket(n)
        if lows[b] is None:
            lows[b] = n
    return lows


_BUCKET_LOW = _bucket_lower_bounds()


def _bias_of_distance(dist, table_ref, h):
    out = jnp.full(dist.shape, table_ref[0, h], F32)
    for b in range(1, N_BUCKETS):
        if _BUCKET_LOW[b] is None:
            continue
        out = jnp.where(dist >= _BUCKET_LOW[b], table_ref[b, h], out)
    return out


def _bias_kernel(table_ref, mat_ref, tail_ref, *, tail_w):
    h = pl.program_id(0)
    kl = lax.broadcasted_iota(I32, (TK, TQ), 0)
    ql = lax.broadcasted_iota(I32, (TK, TQ), 1)
    for dd in range(3):
        mat_ref[dd, 0] = _bias_of_distance(ql - kl + dd * TQ, table_ref, h)
    lane = lax.broadcasted_iota(I32, (1, tail_w), 1)
    tail_ref[0] = _bias_of_distance(tail_w - lane, table_ref, h)


def _bias_tables(rel_bias, tail_w):
    n_h = rel_bias.shape[1]
    return pl.pallas_call(
        functools.partial(_bias_kernel, tail_w=tail_w),
        grid=(n_h,),
        in_specs=[pl.BlockSpec(memory_space=pltpu.SMEM)],
        out_specs=[pl.BlockSpec((3, 1, TK, TQ), lambda h: (0, h, 0, 0)),
                   pl.BlockSpec((1, 1, tail_w), lambda h: (h, 0, 0))],
        out_shape=[jax.ShapeDtypeStruct((3, n_h, TK, TQ), F32),
                   jax.ShapeDtypeStruct((n_h, 1, tail_w), F32)],
        compiler_params=_cparams(("arbitrary",)),
        name="bias_tables",
    )(rel_bias)


def _ada_kernel(c_ref, w_ref, b_ref, o_ref):
    c = c_ref[...]
    s = (c / (1.0 + jnp.exp(-c))).astype(BF16)
    o_ref[...] = _dot(s, w_ref[...].astype(BF16)) + b_ref[...]


def _adaln(c, w, b):
    n, d = c.shape
    tn = min(1024, d)
    return pl.pallas_call(
        _ada_kernel,
        grid=(w.shape[1] // tn,),
        in_specs=[pl.BlockSpec((n, d), lambda j: (0, 0)),
                  pl.BlockSpec((d, tn), lambda j: (0, j)),
                  pl.BlockSpec((1, tn), lambda j: (0, j))],
        out_specs=pl.BlockSpec((n, tn), lambda j: (0, j)),
        out_shape=jax.ShapeDtypeStruct((n, w.shape[1]), F32),
        compiler_params=_cparams(("arbitrary",)),
        name="adaln",
    )(c, w, b.reshape(1, -1))


def _row_tile(r):
    return min(r, 512)


def _tok_spec(tr, w):
    return pl.BlockSpec((1, tr, w), lambda g, r: (g, r, 0))


def _mod_spec(mod, tr):
    if mod.shape[1] == 1:
        return pl.BlockSpec((1, 1, mod.shape[2]), lambda g, r: (g, 0, 0))
    return pl.BlockSpec((1, tr, mod.shape[2]), lambda g, r: (g, r, 0))


def _const_spec(shape):
    nd = len(shape)
    return pl.BlockSpec(shape, lambda g, r: (0,) * nd)


def _rms(x, g):
    return x * lax.rsqrt(jnp.mean(x * x, axis=-1, keepdims=True) + EPS) * g


def _normmod_kernel(x_ref, g_ref, sh_ref, sc_ref, o_ref):
    y = _rms(x_ref[0], g_ref[...])
    o_ref[0] = (y * (1.0 + sc_ref[0]) + sh_ref[0]).astype(BF16)


def _normmod(x, g, sh, sc):
    n_g, n_r, d = x.shape
    tr = _row_tile(n_r)
    return pl.pallas_call(
        _normmod_kernel,
        grid=(n_g, n_r // tr),
        in_specs=[_tok_spec(tr, d), _const_spec((1, d)), _mod_spec(sh, tr), _mod_spec(sc, tr)],
        out_specs=_tok_spec(tr, d),
        out_shape=jax.ShapeDtypeStruct((n_g, n_r, d), BF16),
        compiler_params=_cparams(("arbitrary", "arbitrary")),
        name="norm_modulate",
    )(x, g.reshape(1, d), sh, sc)


def _head_norm_store(z, g, o_ref, n, dtype):
    for hh in range(n):
        cols = slice(hh * HEAD_DIM, (hh + 1) * HEAD_DIM)
        o_ref[0, :, cols] = _rms(z[:, cols], g).astype(dtype)


_A_QA = (0, Q_W)
_A_KA = (_A_QA[1], _A_QA[1] + KV_W)
_A_VA = (_A_KA[1], _A_KA[1] + KV_W)
_A_QI = (_A_VA[1], _A_VA[1] + N_IDX_HEADS * IDX_DIM)
_A_MISC = (_A_QI[1], _A_QI[1] + LANE)
_WI_LANE = IDX_DIM


def _proj_a_kernel(h_ref, w_ref, gq_ref, gk_ref, gki_ref,
                   qa_o, ka_o, va_o, qi_o, ki_o, misc_o, kilo_o, kihi_o):
    h = h_ref[0]
    _head_norm_store(_dot(h, w_ref[:, _A_QA[0]:_A_QA[1]]), gq_ref[...], qa_o, N_HEADS, BF16)
    _head_norm_store(_dot(h, w_ref[:, _A_KA[0]:_A_KA[1]]), gk_ref[...], ka_o, N_KV, F32)
    va_o[0] = _dot(h, w_ref[:, _A_VA[0]:_A_VA[1]])
    qi_o[0] = _dot(h, w_ref[:, _A_QI[0]:_A_QI[1]]).astype(BF16)
    z = _dot(h, w_ref[:, _A_MISC[0]:_A_MISC[1]])
    is_ki = lax.broadcasted_iota(I32, z.shape, 1) < IDX_DIM
    ssq = jnp.sum(jnp.where(is_ki, z * z, 0.0), axis=-1, keepdims=True) * (1.0 / IDX_DIM)
    kin = jnp.where(is_ki, z * lax.rsqrt(ssq + EPS) * gki_ref[...], 0.0)
    ki_o[0] = kin[:, :IDX_DIM]
    misc_o[0] = jnp.where(is_ki, kin, z * IDX_WEIGHT_SCALE)
    kilo_o[0] = kin.astype(BF16)
    kihi_o[0] = pltpu.roll(kin, IDX_DIM, axis=1).astype(BF16)


def _proj_a(h, w_a, g_qa, g_ka, g_kidx_pad):
    n_g, n_r, d = h.shape
    tr = _row_tile(n_r)
    widths = [(Q_W, BF16), (KV_W, F32), (KV_W, F32), (N_IDX_HEADS * IDX_DIM, BF16),
              (IDX_DIM, F32), (LANE, F32), (LANE, BF16), (LANE, BF16)]
    return pl.pallas_call(
        _proj_a_kernel,
        grid=(n_g, n_r // tr),
        in_specs=[_tok_spec(tr, d), _const_spec(w_a.shape), _const_spec((1, HEAD_DIM)),
                  _const_spec((1, HEAD_DIM)), _const_spec((1, LANE))],
        out_specs=[_tok_spec(tr, w) for w, _ in widths],
        out_shape=[jax.ShapeDtypeStruct((n_g, n_r, w), dt) for w, dt in widths],
        compiler_params=_cparams(("arbitrary", "arbitrary")),
        name="proj_dsa",
    )(h, w_a, g_qa.reshape(1, -1), g_ka.reshape(1, -1), g_kidx_pad)


def _proj_b_kernel(h_ref, w_ref, gq_ref, gk_ref, qb_o, kb_o, vb_o):
    h = h_ref[0]
    _head_norm_store(_dot(h, w_ref[:, 0:Q_W]), gq_ref[...], qb_o, N_HEADS, BF16)
    _head_norm_store(_dot(h, w_ref[:, Q_W:Q_W + KV_W]), gk_ref[...], kb_o, N_KV, F32)
    vb_o[0] = _dot(h, w_ref[:, Q_W + KV_W:Q_W + 2 * KV_W])


def _proj_b(h, w_b, g_qb, g_kb):
    n_g, n_r, d = h.shape
    tr = _row_tile(n_r)
    widths = [(Q_W, BF16), (KV_W, F32), (KV_W, F32)]
    return pl.pallas_call(
        _proj_b_kernel,
        grid=(n_g, n_r // tr),
        in_specs=[_tok_spec(tr, d), _const_spec(w_b.shape), _const_spec((1, HEAD_DIM)),
                  _const_spec((1, HEAD_DIM))],
        out_specs=[_tok_spec(tr, w) for w, _ in widths],
        out_shape=[jax.ShapeDtypeStruct((n_g, n_r, w), dt) for w, dt in widths],
        compiler_params=_cparams(("arbitrary", "arbitrary")),
        name="proj_moba",
    )(h, w_b, g_qb.reshape(1, -1), g_kb.reshape(1, -1))


def _sigmoid(x):
    return 1.0 / (1.0 + jnp.exp(-x))


def _gate_merge_kernel(h_ref, oa_ref, ob_ref, wga_ref, wgb_ref, wpa_ref, wpb_ref, m_o):
    h = h_ref[0]
    m = (_sigmoid(_dot(h, wga_ref[...])) * _dot(oa_ref[0], wpa_ref[...])
         + _sigmoid(_dot(h, wgb_ref[...])) * _dot(ob_ref[0], wpb_ref[...]))
    m_o[0] = m.astype(BF16)


def _gate_merge(h, oa, ob, w_ga, w_gb, w_pa, w_pb):
    n_g, n_r, d = h.shape
    tr = _row_tile(n_r)
    tn = d // 2
    tok = lambda w: pl.BlockSpec((1, tr, w), lambda c, g, r: (g, r, 0))
    col = lambda k: pl.BlockSpec((k, tn), lambda c, g, r: (0, c))
    return pl.pallas_call(
        _gate_merge_kernel,
        grid=(d // tn, n_g, n_r // tr),
        in_specs=[tok(d), tok(Q_W), tok(Q_W), col(d), col(d), col(Q_W), col(Q_W)],
        out_specs=pl.BlockSpec((1, tr, tn), lambda c, g, r: (g, r, c)),
        out_shape=jax.ShapeDtypeStruct((n_g, n_r, d), BF16),
        compiler_params=_cparams(("arbitrary", "arbitrary", "arbitrary")),
        name="gate_merge",
    )(h, oa, ob, w_ga, w_gb, w_pa, w_pb)


def _out_proj_kernel(m_ref, x_ref, w_ref, gt_ref, g2_ref, sh_ref, sc_ref, x2_o, h2_o):
    x2 = x_ref[0] + gt_ref[0] * _dot(m_ref[0], w_ref[...])
    x2_o[0] = x2
    h2_o[0] = (_rms(x2, g2_ref[...]) * (1.0 + sc_ref[0]) + sh_ref[0]).astype(BF16)


def _out_proj(m, x, w_out, gt, g2, sh, sc):
    n_g, n_r, d = x.shape
    tr = _row_tile(n_r)
    return pl.pallas_call(
        _out_proj_kernel,
        grid=(n_g, n_r // tr),
        in_specs=[_tok_spec(tr, d), _tok_spec(tr, d), _const_spec(w_out.shape), _mod_spec(gt, tr),
                  _const_spec((1, d)), _mod_spec(sh, tr), _mod_spec(sc, tr)],
        out_specs=[_tok_spec(tr, d), _tok_spec(tr, d)],
        out_shape=[jax.ShapeDtypeStruct((n_g, n_r, d), F32),
                   jax.ShapeDtypeStruct((n_g, n_r, d), BF16)],
        compiler_params=_cparams(("arbitrary", "arbitrary")),
        name="out_proj",
    )(m, x, w_out, gt, g2.reshape(1, d), sh, sc)


def _ffn_kernel(h_ref, x_ref, wa_ref, wu_ref, wo_ref, gt_ref, y_o, acc_s):
    j = pl.program_id(2)

    @pl.when(j == 0)
    def _():
        acc_s[...] = jnp.zeros_like(acc_s)

    h = h_ref[0]
    a = _dot(h, wa_ref[...])
    u = _dot(h, wu_ref[...])
    act = (a * _sigmoid(a) * u).astype(BF16)
    acc_s[...] += _dot(act, wo_ref[...])

    @pl.when(j == pl.num_programs(2) - 1)
    def _():
        y_o[0] = x_ref[0] + gt_ref[0] * acc_s[...]


def _ffn(h2, x2, w_in, w_out, gt):
    n_g, n_r, d = x2.shape
    d_ff = w_out.shape[0]
    tr = _row_tile(n_r)
    tf = 512 if d_ff % 512 == 0 else 256
    n_f = d_ff // tf
    tok = lambda w: pl.BlockSpec((1, tr, w), lambda g, r, j: (g, r, 0))
    if gt.shape[1] == 1:
        gt_spec = pl.BlockSpec((1, 1, d), lambda g, r, j: (g, 0, 0))
    else:
        gt_spec = pl.BlockSpec((1, tr, d), lambda g, r, j: (g, r, 0))
    return pl.pallas_call(
        _ffn_kernel,
        grid=(n_g, n_r // tr, n_f),
        in_specs=[tok(d), tok(d),
                  pl.BlockSpec((d, tf), lambda g, r, j: (0, j)),
                  pl.BlockSpec((d, tf), lambda g, r, j: (0, j + n_f)),
                  pl.BlockSpec((tf, d), lambda g, r, j: (j, 0)),
                  gt_spec],
        out_specs=tok(d),
        out_shape=jax.ShapeDtypeStruct((n_g, n_r, d), F32),
        scratch_shapes=[pltpu.VMEM((tr, d), F32)],
        compiler_params=_cparams(("arbitrary", "arbitrary", "arbitrary")),
        name="swiglu_ffn",
    )(h2, x2, w_in, w_in, w_out, gt)


def _sortable(x):
    bits = pltpu.bitcast(x, I32)
    return bits ^ (lax.shift_right_arithmetic(bits, 31) & 0x7FFFFFFF)


def _stage_kv(k_ref, v_ref, k_s, vt_s, n_blocks):
    def body(c, carry):
        rows = pl.ds(pl.multiple_of(c * TK2, TK2), TK2)
        k_s[rows, :] = k_ref[0, rows, :].astype(BF16)
        v = v_ref[0, rows, :]
        for kh in range(N_KV):
            cols = slice(kh * HEAD_DIM, (kh + 1) * HEAD_DIM)
            vt_s[c, cols, :] = v[:, cols].T.astype(BF16)
        return carry
    lax.fori_loop(0, n_blocks, body, 0)


def _fold8(x, op):
    r, w = x.shape
    return op(x.reshape(r // SUBLANE, SUBLANE, w), axis=0)


def _attend(i, n_pc, mask_of, q_ref, k_s, vt_s, bias_ref, lg_s, acc_s, o_ref):
    def logits_body(c2, mx):
        rows = pl.ds(pl.multiple_of(c2 * TK2, TK2), TK2)
        out = []
        for kh in range(N_KV):
            q2 = jnp.concatenate(
                [q_ref[0, :, (kh * GQA + g) * HEAD_DIM:(kh * GQA + g + 1) * HEAD_DIM]
                 for g in range(GQA)], axis=0)
            lg = _nt(k_s[rows, kh * HEAD_DIM:(kh + 1) * HEAD_DIM], q2) * ATTN_SCALE
            best = mx[kh]
            for a in range(CPP):
                dd = jnp.clip(i - (c2 * CPP + a), 0, 2)
                folded = []
                for g in range(GQA):
                    h = kh * GQA + g
                    t = lg[a * TK:(a + 1) * TK, g * TQ:(g + 1) * TQ] + bias_ref[dd, h]
                    t = jnp.where(mask_of(c2, a, h), t, NEG)
                    lg_s[kh, c2, a * TK:(a + 1) * TK, g * TQ:(g + 1) * TQ] = t
                    folded.append(_fold8(t, jnp.max))
                best = jnp.maximum(best, jnp.concatenate(folded, axis=1))
            out.append(best)
        return tuple(out)

    mx0 = tuple(jnp.full((SUBLANE, GQA * TQ), NEG, F32) for _ in range(N_KV))
    mx = lax.fori_loop(0, n_pc, logits_body, mx0)
    m = [jnp.max(x, axis=0, keepdims=True) for x in mx]

    acc_s[...] = jnp.zeros(acc_s.shape, F32)

    def pv_body(c2, ls):
        out = []
        for kh in range(N_KV):
            p = jnp.exp(lg_s[kh, c2] - m[kh])
            out.append(ls[kh] + _fold8(p, jnp.sum))
            acc_s[kh] += _dot(vt_s[c2, kh * HEAD_DIM:(kh + 1) * HEAD_DIM, :], p.astype(BF16))
        return tuple(out)

    ls0 = tuple(jnp.zeros((SUBLANE, GQA * TQ), F32) for _ in range(N_KV))
    ls = lax.fori_loop(0, n_pc, pv_body, ls0)
    for kh in range(N_KV):
        o2 = acc_s[kh] * (1.0 / jnp.sum(ls[kh], axis=0, keepdims=True))
        for g in range(GQA):
            h = kh * GQA + g
            o_ref[0, :, h * HEAD_DIM:(h + 1) * HEAD_DIM] = o2[:, g * TQ:(g + 1) * TQ].T.astype(BF16)


def _dsa_kernel(qi_ref, misc_ref, kilo_ref, kihi_ref, qa_ref, ka_ref, va_ref, bias_ref, o_ref,
                k_s, vt_s, sk_s, w_s, lg_s, acc_s, *, n_blocks, top_k, pos_bits):
    i = pl.program_id(1)
    n_pc = i // CPP + 1
    kl = lax.broadcasted_iota(I32, (TK, TQ), 0)
    ql = lax.broadcasted_iota(I32, (TK, TQ), 1)
    qpos = i * TQ + lax.broadcasted_iota(I32, (1, TQ), 1)

    @pl.when(i == 0)
    def _():
        _stage_kv(ka_ref, va_ref, k_s, vt_s, n_blocks)

    w_s[...] = misc_ref[0].T

    def score_body(c2, carry):
        rows = pl.ds(pl.multiple_of(c2 * TK2, TK2), TK2)
        k2 = jnp.concatenate([kilo_ref[0, rows, :], kihi_ref[0, rows, :]], axis=0)
        acc = [jnp.zeros((TK, TQ), F32) for _ in range(CPP)]
        for p2 in range(N_IDX_HEADS // 4):
            qp = jnp.concatenate([qi_ref[0, :, (2 * p2 + pair) * LANE:(2 * p2 + pair + 1) * LANE]
                                  for pair in range(2)], axis=0)
            s = jnp.maximum(_nt(k2, qp), 0.0)
            for half in range(2):
                for pair in range(2):
                    r = _WI_LANE + 4 * p2 + 2 * pair + half
                    w = w_s[r:r + 1, :]
                    for a in range(CPP):
                        r0 = half * TK2 + a * TK
                        acc[a] = acc[a] + w * s[r0:r0 + TK, pair * TQ:(pair + 1) * TQ]
        for a in range(CPP):
            c = c2 * CPP + a
            admissible = (c * TK + kl) <= (i * TQ + ql)
            sk_s[c] = jnp.where(admissible, _sortable(acc[a]), INT_MIN)
        return carry
    lax.fori_loop(0, n_pc, score_body, 0)

    def count(pred):
        def body(c, acc):
            return acc + jnp.where(pred(c, sk_s[c]), 1.0, 0.0)
        acc = lax.fori_loop(0, i + 1, body, jnp.zeros((TK, TQ), F32))
        return jnp.sum(acc, axis=0, keepdims=True)

    kf = float(top_k)
    few = qpos + 1 <= top_k

    def threshold():
        zero = jnp.zeros((1, TQ), I32)
        t0 = jnp.where(count(lambda c, s: s >= zero) >= kf, zero, INT_MIN)

        def bit_body(s, t):
            cand = t + lax.shift_left(jnp.int32(1), 30 - s)
            return jnp.where(count(lambda c, sk: sk >= cand) >= kf, cand, t)
        return lax.fori_loop(0, 31, bit_body, t0)

    t = lax.cond((i + 1) * TQ > top_k, threshold, lambda: jnp.full((1, TQ), INT_MIN + 1, I32))
    t = jnp.where(few, INT_MIN + 1, t)

    @pl.when((i + 1) * TQ > top_k)
    def _():
        need = kf - count(lambda c, sk: sk > t)
        n_eq = count(lambda c, sk: sk == t)
        tied = jnp.max(jnp.where(jnp.logical_and(n_eq > need, jnp.logical_not(few)), 1.0, 0.0))

        @pl.when(tied > 0.0)
        def _():
            def pos_body(s, lo):
                cand = lo + lax.shift_left(jnp.int32(1), pos_bits - 1 - s)
                below = count(lambda c, sk: jnp.logical_and(sk == t, c * TK + kl < cand))
                return jnp.where(below < need, cand, lo)
            last = lax.fori_loop(0, pos_bits, pos_body, jnp.zeros((1, TQ), I32))

            def drop_body(c, carry):
                sk = sk_s[c]
                drop = jnp.logical_and(sk == t, c * TK + kl > last)
                sk_s[c] = jnp.where(drop, INT_MIN, sk)
                return carry
            lax.fori_loop(0, i + 1, drop_body, 0)

    _attend(i, n_pc, lambda c2, a, h: sk_s[c2 * CPP + a] >= t, qa_ref, k_s, vt_s, bias_ref,
            lg_s, acc_s, o_ref)


def _attn_scratch(n_blocks):
    return [pltpu.VMEM((n_blocks * TK2, KV_W), BF16),
            pltpu.VMEM((n_blocks, KV_W, TK2), BF16)]


def _softmax_scratch(n_blocks):
    return [pltpu.VMEM((N_KV, n_blocks, TK2, GQA * TQ), F32),
            pltpu.VMEM((N_KV, HEAD_DIM, GQA * TQ), F32)]


def _dsa_prompt(qi, misc, kilo, kihi, qa, ka, va, bias):
    n_b, n_s, _ = qa.shape
    n_blocks = n_s // TK2
    top_k = min(DSA_TOPK, n_s // 4)
    qblk = lambda w: pl.BlockSpec((1, TQ, w), lambda b, i: (b, i, 0))
    seq = lambda w: pl.BlockSpec((1, n_s, w), lambda b, i: (b, 0, 0))
    return pl.pallas_call(
        functools.partial(_dsa_kernel, n_blocks=n_blocks, top_k=top_k,
                          pos_bits=max(1, (n_s - 1).bit_length())),
        grid=(n_b, n_s // TQ),
        in_specs=[qblk(N_IDX_HEADS * IDX_DIM), qblk(LANE), seq(LANE), seq(LANE), qblk(Q_W),
                  seq(KV_W), seq(KV_W),
                  pl.BlockSpec((3, N_HEADS, TK, TQ), lambda b, i: (0, 0, 0, 0))],
        out_specs=qblk(Q_W),
        out_shape=jax.ShapeDtypeStruct((n_b, n_s, Q_W), BF16),
        scratch_shapes=_attn_scratch(n_blocks) + [
            pltpu.VMEM((n_s // TK, TK, TQ), I32),
            pltpu.VMEM((LANE, TQ), F32),
        ] + _softmax_scratch(n_blocks),
        compiler_params=_cparams(("arbitrary", "arbitrary")),
        name="dsa_prompt",
    )(qi, misc, kilo, kihi, qa, ka, va, bias)


def _split_hi_lo(x):
    hi = x.astype(BF16)
    return hi, (x - hi.astype(F32)).astype(BF16)


def _moba_kernel(qb_ref, kb_ref, vb_ref, bias_ref, o_ref,
                 k_s, vt_s, kmean_s, sel_s, lg_s, acc_s, *, n_blocks, n_sel):
    i = pl.program_id(1)
    cur = (i * TQ) // MOBA_BLOCK
    kl = lax.broadcasted_iota(I32, (TK, TQ), 0)
    ql = lax.broadcasted_iota(I32, (TK, TQ), 1)

    @pl.when(i == 0)
    def _():
        _stage_kv(kb_ref, vb_ref, k_s, vt_s, n_blocks)
        for n in range(n_blocks):
            blk = kb_ref[0, n * MOBA_BLOCK:(n + 1) * MOBA_BLOCK, :]
            kmean_s[n:n + 1, :] = jnp.sum(blk, axis=0, keepdims=True) * (1.0 / MOBA_BLOCK)

    blk_id = lax.broadcasted_iota(I32, (n_blocks, TQ), 0)
    for kh in range(N_KV):
        cols = slice(kh * HEAD_DIM, (kh + 1) * HEAD_DIM)
        km_hi, km_lo = _split_hi_lo(kmean_s[:, cols])
        for g in range(GQA):
            h = kh * GQA + g
            qh = qb_ref[0, :, h * HEAD_DIM:(h + 1) * HEAD_DIM]
            gate = _nt(km_hi, qh) + _nt(km_lo, qh)
            gate = jnp.where(blk_id < cur, gate, NEG)
            for n in range(n_blocks):
                gn = gate[n:n + 1, :]
                beats = jnp.logical_or(gate > gn, jnp.logical_and(gate == gn, blk_id < n))
                rank = jnp.sum(jnp.where(beats, 1.0, 0.0), axis=0, keepdims=True)
                keep = jnp.where(rank < n_sel, 1.0, 0.0)
                sel_s[h, n:n + 1, :] = jnp.where(n < cur, keep, 0.0)

    def mask_of(c2, a, h):
        causal = (c2 * TK2 + a * TK + kl) <= (i * TQ + ql)
        row = jnp.where(c2 == cur, 1.0, sel_s[h, pl.ds(c2, 1), :])
        return jnp.where(causal, jnp.broadcast_to(row, (TK, TQ)), 0.0) > 0.0

    _attend(i, cur + 1, mask_of, qb_ref, k_s, vt_s, bias_ref, lg_s, acc_s, o_ref)


def _moba_prompt(qb, kb, vb, bias):
    n_b, n_s, _ = qb.shape
    n_blocks = n_s // MOBA_BLOCK
    qblk = lambda w: pl.BlockSpec((1, TQ, w), lambda b, i: (b, i, 0))
    seq = lambda w: pl.BlockSpec((1, n_s, w), lambda b, i: (b, 0, 0))
    return pl.pallas_call(
        functools.partial(_moba_kernel, n_blocks=n_blocks, n_sel=min(MOBA_TOPK, n_blocks)),
        grid=(n_b, n_s // TQ),
        in_specs=[qblk(Q_W), seq(KV_W), seq(KV_W),
                  pl.BlockSpec((3, N_HEADS, TK, TQ), lambda b, i: (0, 0, 0, 0))],
        out_specs=qblk(Q_W),
        out_shape=jax.ShapeDtypeStruct((n_b, n_s, Q_W), BF16),
        scratch_shapes=_attn_scratch(n_blocks) + [
            pltpu.VMEM((n_blocks, KV_W), F32),
            pltpu.VMEM((N_HEADS, n_blocks, TQ), F32),
        ] + _softmax_scratch(n_blocks),
        compiler_params=_cparams(("arbitrary", "arbitrary")),
        name="moba_prompt",
    )(qb, kb, vb, bias)


SPAN = 2048
SPAN_ROWS = SPAN * N_KV


def _page_copies(pt_ref, b, first_page, n_pages, pool_ref, buf, sem, rows):
    return [pltpu.make_async_copy(pool_ref.at[pt_ref[b, first_page + p]],
                                  buf.at[pl.ds(p * rows, rows)], sem)
            for p in range(n_pages)]


def _own_kv_lane(shape):
    row = lax.broadcasted_iota(I32, shape, 0)
    lane = lax.broadcasted_iota(I32, shape, 1)
    return (lane & (N_KV - 1)) == row // GQA


def _sidx_kernel(pt_ref, qi_ref, wi_ref, kin_ref, pool_ref, mask_o, new_o,
                 xbuf, sem, s_s, *, n_spans, n_pages, page, top_k):
    b = pl.program_id(0)
    n_b = pl.num_programs(0)
    slot = lax.rem(b, 2)

    def copies(bb, sl):
        return [pltpu.make_async_copy(pool_ref.at[pt_ref[bb, p]],
                                      xbuf.at[sl, :, pl.ds(p * page, page)], sem.at[sl])
                for p in range(n_pages)]

    @pl.when(b == 0)
    def _():
        for cp in copies(b, slot):
            cp.start()

    @pl.when(b + 1 < n_b)
    def _():
        for cp in copies(b + 1, 1 - slot):
            cp.start()

    for cp in copies(b, slot):
        cp.wait()

    q = qi_ref[0]
    w = wi_ref[0]
    for j in range(n_spans):
        x = xbuf[slot, :, j * SPAN:(j + 1) * SPAN].astype(BF16)
        s = jnp.sum(w * jnp.maximum(_dot(q, x), 0.0), axis=0, keepdims=True)
        s_s[j:j + 1, :] = _sortable(s + 0.0)
    kin = kin_ref[0].astype(BF16).astype(F32)
    s_new = jnp.sum(q.astype(F32) * kin, axis=1, keepdims=True)
    s_new = jnp.sum(w * jnp.maximum(s_new, 0.0), axis=0, keepdims=True)
    key_new = _sortable(s_new + 0.0)
    sk = s_s[...]

    def count(pred_past, pred_new):
        c = jnp.sum(jnp.where(pred_past, 1.0, 0.0), axis=1, keepdims=True)
        return jnp.sum(c, axis=0, keepdims=True) + jnp.where(pred_new, 1.0, 0.0)

    kf = float(top_k)
    zero = jnp.zeros((1, 1), I32)
    t0 = jnp.where(count(sk >= zero, key_new >= zero) >= kf, zero, INT_MIN)

    def bit_body(s, t):
        cand = t + lax.shift_left(jnp.int32(1), 30 - s)
        return jnp.where(count(sk >= cand, key_new >= cand) >= kf, cand, t)
    t = lax.fori_loop(0, 31, bit_body, t0)

    need = kf - count(sk > t, key_new > t)
    pos = (lax.broadcasted_iota(I32, sk.shape, 0) * SPAN + lax.broadcasted_iota(I32, sk.shape, 1))
    eq = sk == t
    pos_bits = max(1, (n_spans * SPAN - 1).bit_length())

    def pos_body(s, lo):
        cand = lo + lax.shift_left(jnp.int32(1), pos_bits - 1 - s)
        below = count(jnp.logical_and(eq, pos < cand), False)
        return jnp.where(below < need, cand, lo)
    last = lax.fori_loop(0, pos_bits, pos_body, jnp.zeros((1, 1), I32))
    n_eq_past = count(eq, False)
    sel = jnp.logical_or(sk > t, jnp.logical_and(eq, pos <= last))
    mask_o[0] = jnp.where(sel, 1.0, 0.0)
    new_sel = jnp.logical_or(key_new > t, jnp.logical_and(key_new == t, n_eq_past < need))
    new_o[0] = jnp.broadcast_to(jnp.where(new_sel, 1.0, 0.0), (1, LANE))


def _sample_index(page_table, qi, wi, ki_new, pool_t):
    n_b, n_pages = page_table.shape
    page = pool_t.shape[2]
    n_spans = n_pages * page // SPAN
    top_k = min(DSA_TOPK, (n_pages * page + 1) // 4)
    grid_spec = pltpu.PrefetchScalarGridSpec(
        num_scalar_prefetch=1,
        grid=(n_b,),
        in_specs=[pl.BlockSpec((1, N_IDX_HEADS, IDX_DIM), lambda b, pt: (b, 0, 0)),
                  pl.BlockSpec((1, N_IDX_HEADS, 1), lambda b, pt: (b, 0, 0)),
                  pl.BlockSpec((1, 1, IDX_DIM), lambda b, pt: (b, 0, 0)),
                  pl.BlockSpec(memory_space=pl.ANY)],
        out_specs=[pl.BlockSpec((1, n_spans, SPAN), lambda b, pt: (b, 0, 0)),
                   pl.BlockSpec((1, 1, LANE), lambda b, pt: (b, 0, 0))],
        scratch_shapes=[pltpu.VMEM((2, IDX_DIM, n_pages * page), F32),
                        pltpu.SemaphoreType.DMA((2,)),
                        pltpu.VMEM((n_spans, SPAN), I32)],
    )
    return pl.pallas_call(
        functools.partial(_sidx_kernel, n_spans=n_spans, n_pages=n_pages, page=page, top_k=top_k),
        grid_spec=grid_spec,
        out_shape=[jax.ShapeDtypeStruct((n_b, n_spans, SPAN), F32),
                   jax.ShapeDtypeStruct((n_b, 1, LANE), F32)],
        compiler_params=_cparams(("arbitrary",)),
        name="sample_indexer",
    )(page_table, qi, wi, ki_new, pool_t)


def _rows_per_kv(x):
    return jnp.concatenate(
        [jnp.broadcast_to(x[:, kh * HEAD_DIM:(kh + 1) * HEAD_DIM], (GQA, HEAD_DIM))
         for kh in range(N_KV)], axis=0)


def _sdsa_kernel(pt_ref, mask_ref, new_ref, q_ref, kn_ref, vn_ref, tail_ref, far_ref, b0_ref,
                 kpool_ref, vpool_ref, o_ref, kbuf, vbuf, sem, m_s, l_s, acc_s, *, n_spans, page):
    b = pl.program_id(0)
    j = pl.program_id(1)
    n_b = pl.num_programs(0)
    pps = SPAN // page
    step = b * n_spans + j
    slot = lax.rem(step, 2)

    def copies(bb, jj, sl):
        rows = page * N_KV
        return (_page_copies(pt_ref, bb, jj * pps, pps, kpool_ref, kbuf.at[sl], sem.at[0, sl], rows)
                + _page_copies(pt_ref, bb, jj * pps, pps, vpool_ref, vbuf.at[sl], sem.at[1, sl], rows))

    @pl.when(step == 0)
    def _():
        for cp in copies(b, j, slot):
            cp.start()

    @pl.when(step + 1 < n_b * n_spans)
    def _():
        nxt = step + 1
        for cp in copies(nxt // n_spans, lax.rem(nxt, n_spans), 1 - slot):
            cp.start()

    q = q_ref[0]

    @pl.when(j == 0)
    def _():
        qf = q.astype(F32)
        kn = _rows_per_kv(kn_ref[0].astype(BF16).astype(F32))
        lg = jnp.sum(qf * kn, axis=1, keepdims=True) * ATTN_SCALE + b0_ref[...]
        keep = new_ref[0][:, 0:1] > 0.0
        m_s[...] = jnp.where(keep, lg, NEG)
        l_s[...] = jnp.where(keep, jnp.ones_like(lg), 0.0)
        vn = _rows_per_kv(vn_ref[0].astype(BF16).astype(F32))
        acc_s[...] = jnp.where(keep, vn, 0.0)

    for cp in copies(b, j, slot):
        cp.wait()

    kc = kbuf[slot].astype(BF16)
    vc = vbuf[slot].astype(BF16)
    bias = jnp.where(j == n_spans - 1, tail_ref[...], far_ref[...])
    mk = jnp.logical_and(mask_ref[0, pl.ds(j, 1), :] > 0.0, _own_kv_lane((N_HEADS, SPAN_ROWS)))
    lg = jnp.where(mk, _nt(q, kc) * ATTN_SCALE + bias, NEG)
    m_old = m_s[...]
    m_new = jnp.maximum(m_old, jnp.max(lg, axis=1, keepdims=True))
    alpha = jnp.exp(m_old - m_new)
    p = jnp.where(mk, jnp.exp(lg - m_new), 0.0)
    l_s[...] = alpha * l_s[...] + jnp.sum(p, axis=1, keepdims=True)
    acc_s[...] = alpha * acc_s[...] + _dot(p.astype(BF16), vc)
    m_s[...] = m_new

    @pl.when(j == n_spans - 1)
    def _():
        o_ref[0] = (acc_s[...] * (1.0 / l_s[...])).astype(BF16)


def _sample_dsa(page_table, mask4, new_sel, qa, ka_new, va_new, tail4, far, bias0, k_pool, v_pool):
    n_b, n_pages = page_table.shape
    page = k_pool.shape[1] // N_KV
    n_spans = n_pages * page // SPAN
    per_b = lambda shape: pl.BlockSpec((1,) + shape, lambda b, j, pt: (b, 0, 0))
    const = lambda shape: pl.BlockSpec(shape, lambda b, j, pt: (0, 0))
    grid_spec = pltpu.PrefetchScalarGridSpec(
        num_scalar_prefetch=1,
        grid=(n_b, n_spans),
        in_specs=[per_b((n_spans, SPAN_ROWS)), per_b((1, LANE)), per_b((N_HEADS, HEAD_DIM)),
                  per_b((1, KV_W)), per_b((1, KV_W)),
                  const((N_HEADS, SPAN_ROWS)), const((N_HEADS, 1)), const((N_HEADS, 1)),
                  pl.BlockSpec(memory_space=pl.ANY), pl.BlockSpec(memory_space=pl.ANY)],
        out_specs=per_b((N_HEADS, HEAD_DIM)),
        scratch_shapes=[pltpu.VMEM((2, SPAN_ROWS, HEAD_DIM), F32),
                        pltpu.VMEM((2, SPAN_ROWS, HEAD_DIM), F32),
                        pltpu.SemaphoreType.DMA((2, 2)),
                        pltpu.VMEM((N_HEADS, 1), F32), pltpu.VMEM((N_HEADS, 1), F32),
                        pltpu.VMEM((N_HEADS, HEAD_DIM), F32)],
    )
    return pl.pallas_call(
        functools.partial(_sdsa_kernel, n_spans=n_spans, page=page),
        grid_spec=grid_spec,
        out_shape=jax.ShapeDtypeStruct((n_b, N_HEADS, HEAD_DIM), BF16),
        compiler_params=_cparams(("arbitrary", "arbitrary")),
        name="sample_dsa",
    )(page_table, mask4, new_sel, qa, ka_new, va_new, tail4, far, bias0, k_pool, v_pool)


def _smoba_gate_kernel(pt_ref, q_ref, kpool_ref, sel_o, kbuf, sem, kmean_s,
                       *, n_spans, page, n_sel, n_blocks):
    b = pl.program_id(0)
    j = pl.program_id(1)
    n_b = pl.num_programs(0)
    pps = SPAN // page
    bps = SPAN // MOBA_BLOCK
    blk_rows = MOBA_BLOCK * N_KV
    step = b * n_spans + j
    slot = lax.rem(step, 2)

    def copies(bb, jj, sl):
        return _page_copies(pt_ref, bb, jj * pps, pps, kpool_ref, kbuf.at[sl], sem.at[sl],
                            page * N_KV)

    @pl.when(step == 0)
    def _():
        for cp in copies(b, j, slot):
            cp.start()

    @pl.when(step + 1 < n_b * n_spans)
    def _():
        nxt = step + 1
        for cp in copies(nxt // n_spans, lax.rem(nxt, n_spans), 1 - slot):
            cp.start()

    for cp in copies(b, j, slot):
        cp.wait()

    means = []
    for n in range(bps):
        s8 = _fold8(kbuf[slot, n * blk_rows:(n + 1) * blk_rows, :], jnp.sum)
        means.append((s8[:N_KV] + s8[N_KV:]) * (1.0 / MOBA_BLOCK))
    kmean_s[pl.ds(pl.multiple_of(j * bps * N_KV, bps * N_KV), bps * N_KV), :] = (
        jnp.concatenate(means, axis=0))

    @pl.when(j == n_spans - 1)
    def _():
        km_hi, km_lo = _split_hi_lo(kmean_s[...])
        q = q_ref[0]
        gate = _nt(q, km_hi) + _nt(q, km_lo)
        gate = jnp.where(_own_kv_lane(gate.shape), gate, -jnp.inf)
        lane = lax.broadcasted_iota(I32, gate.shape, 1).astype(F32)
        out_lane = lax.broadcasted_iota(I32, (N_HEADS, LANE), 1)
        picked = jnp.zeros((N_HEADS, LANE), F32)
        for r in range(n_sel):
            best = jnp.max(gate, axis=1, keepdims=True)
            idx = jnp.min(jnp.where(gate == best, lane, float(n_blocks * N_KV)), axis=1, keepdims=True)
            picked = jnp.where(out_lane == r, idx, picked)
            gate = jnp.where(lane == idx, -jnp.inf, gate)
        sel_o[0] = picked.astype(I32) // N_KV


def _sample_moba_gate(page_table, qb, k_pool):
    n_b, n_pages = page_table.shape
    page = k_pool.shape[1] // N_KV
    n_spans = n_pages * page // SPAN
    n_blocks = (n_pages * page + 1) // MOBA_BLOCK
    grid_spec = pltpu.PrefetchScalarGridSpec(
        num_scalar_prefetch=1,
        grid=(n_b, n_spans),
        in_specs=[pl.BlockSpec((1, N_HEADS, HEAD_DIM), lambda b, j, pt: (b, 0, 0)),
                  pl.BlockSpec(memory_space=pl.ANY)],
        out_specs=pl.BlockSpec((1, N_HEADS, LANE), lambda b, j, pt: (b, 0, 0)),
        scratch_shapes=[pltpu.VMEM((2, SPAN_ROWS, HEAD_DIM), F32), pltpu.SemaphoreType.DMA((2,)),
                        pltpu.VMEM((n_blocks * N_KV, HEAD_DIM), F32)],
    )
    return pl.pallas_call(
        functools.partial(_smoba_gate_kernel, n_spans=n_spans, page=page,
                          n_sel=min(MOBA_TOPK, n_blocks), n_blocks=n_blocks),
        grid_spec=grid_spec,
        out_shape=jax.ShapeDtypeStruct((n_b, N_HEADS, LANE), I32),
        compiler_params=_cparams(("arbitrary", "arbitrary")),
        name="sample_moba_gate",
    )(page_table, qb, k_pool)


def _smoba_attn_kernel(pt_ref, sel_ref, selv_ref, q_ref, kn_ref, vn_ref, tail_ref, far_ref, b0_ref,
                       kpool_ref, vpool_ref, o_ref, kbuf, vbuf, sem, *, n_sel, page, n_blocks):
    b = pl.program_id(0)
    n_b = pl.num_programs(0)
    ppb = MOBA_BLOCK // page
    slot = lax.rem(b, 2)
    n_keys = n_sel * MOBA_BLOCK

    def copies(bb, sl):
        out = []
        for h in range(N_HEADS):
            kh = h // GQA
            for r in range(n_sel):
                blk = sel_ref[(bb * N_HEADS + h) * n_sel + r]
                for p in range(ppb):
                    pg = pt_ref[bb, blk * ppb + p]
                    rows = pl.ds((r * ppb + p) * page, page)
                    out.append(pltpu.make_async_copy(kpool_ref.at[pg, :, kh, :],
                                                     kbuf.at[sl, h, rows], sem.at[0, sl]))
                    out.append(pltpu.make_async_copy(vpool_ref.at[pg, :, kh, :],
                                                     vbuf.at[sl, h, rows], sem.at[1, sl]))
        return out

    @pl.when(b == 0)
    def _():
        for cp in copies(b, slot):
            cp.start()

    @pl.when(b + 1 < n_b)
    def _():
        for cp in copies(b + 1, 1 - slot):
            cp.start()

    for cp in copies(b, slot):
        cp.wait()

    q = q_ref[0]
    row = lax.broadcasted_iota(I32, (N_HEADS, n_keys), 0)
    lg = jnp.zeros((N_HEADS, n_keys), F32)
    for h in range(N_HEADS):
        lg = jnp.where(row == h, _nt(q, kbuf[slot, h].astype(BF16)), lg)
    selv = selv_ref[0]
    tail = tail_ref[...]
    far = far_ref[...]
    bias = jnp.concatenate(
        [jnp.where(selv[:, r:r + 1] == n_blocks - 1, tail, far) for r in range(n_sel)], axis=1)
    lg = lg * ATTN_SCALE + bias
    qf = q.astype(F32)
    kn = _rows_per_kv(kn_ref[0].astype(BF16).astype(F32))
    lg_new = jnp.sum(qf * kn, axis=1, keepdims=True) * ATTN_SCALE + b0_ref[...]
    m = jnp.maximum(jnp.max(lg, axis=1, keepdims=True), lg_new)
    p = jnp.exp(lg - m)
    p_new = jnp.exp(lg_new - m)
    denom = jnp.sum(p, axis=1, keepdims=True) + p_new
    pb = p.astype(BF16)
    acc = p_new.astype(BF16).astype(F32) * _rows_per_kv(vn_ref[0].astype(BF16).astype(F32))
    row_o = lax.broadcasted_iota(I32, (N_HEADS, HEAD_DIM), 0)
    for h in range(N_HEADS):
        acc = acc + jnp.where(row_o == h, _dot(pb, vbuf[slot, h].astype(BF16)), 0.0)
    o_ref[0] = (acc * (1.0 / denom)).astype(BF16)


def _sample_moba_attn(page_table, sel, qb, kb_new, vb_new, tail, far, bias0, k_pool, v_pool):
    n_b, n_pages = page_table.shape
    page = k_pool.shape[1]
    n_blocks = (n_pages * page + 1) // MOBA_BLOCK
    n_sel = min(MOBA_TOPK, n_blocks)
    sel_flat = sel[:, :, :n_sel].reshape(-1)
    per_b = lambda shape: pl.BlockSpec((1,) + shape, lambda b, pt, sf: (b, 0, 0))
    const = lambda shape: pl.BlockSpec(shape, lambda b, pt, sf: (0, 0))
    grid_spec = pltpu.PrefetchScalarGridSpec(
        num_scalar_prefetch=2,
        grid=(n_b,),
        in_specs=[per_b((N_HEADS, LANE)), per_b((N_HEADS, HEAD_DIM)), per_b((1, KV_W)),
                  per_b((1, KV_W)), const((N_HEADS, MOBA_BLOCK)), const((N_HEADS, 1)),
                  const((N_HEADS, 1)),
                  pl.BlockSpec(memory_space=pl.ANY), pl.BlockSpec(memory_space=pl.ANY)],
        out_specs=per_b((N_HEADS, HEAD_DIM)),
        scratch_shapes=[pltpu.VMEM((2, N_HEADS, n_sel * MOBA_BLOCK, HEAD_DIM), F32),
                        pltpu.VMEM((2, N_HEADS, n_sel * MOBA_BLOCK, HEAD_DIM), F32),
                        pltpu.SemaphoreType.DMA((2, 2))],
    )
    return pl.pallas_call(
        functools.partial(_smoba_attn_kernel, n_sel=n_sel, page=page, n_blocks=n_blocks),
        grid_spec=grid_spec,
        out_shape=jax.ShapeDtypeStruct((n_b, N_HEADS, HEAD_DIM), BF16),
        compiler_params=_cparams(("arbitrary",)),
        name="sample_moba_attn",
    )(page_table, sel_flat, sel, qb, kb_new, vb_new, tail, far, bias0, k_pool, v_pool)


def _split_w_in(w_in):
    widths = (Q_W, KV_W, KV_W, N_IDX_HEADS * IDX_DIM, IDX_DIM, N_IDX_HEADS, Q_W, KV_W, KV_W,
              w_in.shape[0], w_in.shape[0])
    pts = [0]
    for w in widths:
        pts.append(pts[-1] + w)
    qa, ka, va, qi, ki, wi, qb, kb, vb, ga, gb = [w_in[:, pts[n]:pts[n + 1]] for n in range(11)]
    pad = jnp.zeros((w_in.shape[0], LANE - IDX_DIM - N_IDX_HEADS), w_in.dtype)
    w_a = jnp.concatenate([qa, ka, va, qi, ki, wi, pad], axis=1).astype(BF16)
    w_b = jnp.concatenate([qb, kb, vb], axis=1).astype(BF16)
    return w_a, w_b, ga.astype(BF16), gb.astype(BF16)


def kernel(x_prompt, x_sample, cache_ka, cache_va, cache_kidx, cache_kb, cache_vb, page_table,
           c_prompt, c_sample, rel_bias, w_ada, b_ada, g_norm1, w_in, g_qa, g_ka, g_kidx, g_qb,
           g_kb, w_pa, w_pb, w_out, g_norm2, w_ffn_in, w_ffn_out):
    depth = w_in.shape[0]
    n_b, n_s, d = x_prompt.shape
    n_db, n_ds, _ = x_sample.shape
    n_pool, page = cache_ka.shape[1], cache_ka.shape[2]
    past_len = page_table.shape[1] * page
    assert n_ds == 1, "the decode kernels handle one new token per sequence"
    assert n_s % MOBA_BLOCK == 0 and past_len % SPAN == 0 and SPAN % page == 0
    assert MOBA_BLOCK % page == 0 and past_len >= 4 * DSA_TOPK

    bias_mat, bias_tail = _bias_tables(rel_bias, SPAN)
    mat_a, mat_b = bias_mat[:, :N_HEADS], bias_mat[:, N_HEADS:]
    tail = bias_tail[:, 0, :]
    tail_a4 = jnp.repeat(tail[:N_HEADS], N_KV, axis=1)
    tail_b = tail[N_HEADS:, SPAN - MOBA_BLOCK:]
    far = rel_bias[N_BUCKETS - 1].reshape(-1, 1)
    bias0 = rel_bias[0].reshape(-1, 1)

    xp = x_prompt
    xs = x_sample.reshape(1, n_db, d)
    c_all = jnp.concatenate([c_prompt, c_sample], axis=0)
    rows_p, rows_s = [], []
    for l in range(depth):
        mod = _adaln(c_all, w_ada[l], b_ada[l])
        mod_p = [m.reshape(n_b, 1, d) for m in jnp.split(mod[:n_b], 6, axis=-1)]
        mod_s = [m.reshape(1, n_db, d) for m in jnp.split(mod[n_b:], 6, axis=-1)]
        w_a, w_b, w_ga, w_gb = _split_w_in(w_in[l])
        w_pa_l, w_pb_l, w_out_l = w_pa[l].astype(BF16), w_pb[l].astype(BF16), w_out[l].astype(BF16)
        w_fi, w_fo = w_ffn_in[l].astype(BF16), w_ffn_out[l].astype(BF16)
        g_kidx_pad = jnp.pad(g_kidx[l], (0, LANE - IDX_DIM)).reshape(1, LANE)

        def project(x, mods):
            h = _normmod(x, g_norm1[l], mods[0], mods[1])
            return h, _proj_a(h, w_a, g_qa[l], g_ka[l], g_kidx_pad), _proj_b(h, w_b, g_qb[l], g_kb[l])

        def finish(x, h, oa, ob, mods):
            m = _gate_merge(h, oa, ob, w_ga, w_gb, w_pa_l, w_pb_l)
            x2, h2 = _out_proj(m, x, w_out_l, mods[2], g_norm2[l], mods[3], mods[4])
            return _ffn(h2, x2, w_fi, w_fo, mods[5])

        h, (qa, ka, va, qi, ki, misc, kilo, kihi), (qb, kb, vb) = project(xp, mod_p)
        oa = _dsa_prompt(qi, misc, kilo, kihi, qa, ka, va, mat_a)
        ob = _moba_prompt(qb, kb, vb, mat_b)
        xp = finish(xp, h, oa, ob, mod_p)
        rows_p.append((ka.reshape(n_b, n_s, N_KV, HEAD_DIM), va.reshape(n_b, n_s, N_KV, HEAD_DIM),
                       ki, kb.reshape(n_b, n_s, N_KV, HEAD_DIM), vb.reshape(n_b, n_s, N_KV, HEAD_DIM)))

        h, (qa, ka, va, qi, ki, misc, kilo, kihi), (qb, kb, vb) = project(xs, mod_s)
        tok = lambda t: t.reshape(n_db, 1, t.shape[-1])
        heads = lambda t: t.reshape(n_db, N_HEADS, HEAD_DIM)
        rows_view = lambda c: c[l].reshape(n_pool, page * N_KV, HEAD_DIM)
        wi = misc[0, :, _WI_LANE:_WI_LANE + N_IDX_HEADS].reshape(n_db, N_IDX_HEADS, 1)
        mask, new_sel = _sample_index(page_table, qi.reshape(n_db, N_IDX_HEADS, IDX_DIM), wi,
                                      tok(ki[0]), jnp.swapaxes(cache_kidx[l], 1, 2))
        oa = _sample_dsa(page_table, jnp.repeat(mask, N_KV, axis=2), new_sel, heads(qa[0]),
                         tok(ka[0]), tok(va[0]), tail_a4, far[:N_HEADS], bias0[:N_HEADS],
                         rows_view(cache_ka), rows_view(cache_va))
        sel = _sample_moba_gate(page_table, heads(qb[0]), rows_view(cache_kb))
        ob = _sample_moba_attn(page_table, sel, heads(qb[0]), tok(kb[0]), tok(vb[0]),
                               tail_b, far[N_HEADS:], bias0[N_HEADS:], cache_kb[l], cache_vb[l])
        xs = finish(xs, h, oa.reshape(1, n_db, Q_W), ob.reshape(1, n_db, Q_W), mod_s)
        kv4 = lambda t: t.reshape(n_db, 1, N_KV, HEAD_DIM)
        rows_s.append((kv4(ka[0]), kv4(va[0]), ki.reshape(n_db, 1, IDX_DIM), kv4(kb[0]), kv4(vb[0])))

    outs_p = [jnp.stack(t) for t in zip(*rows_p)]
    outs_s = [jnp.stack(t) for t in zip(*rows_s)]
    return (xp, xs.reshape(n_db, n_ds, d), *outs_p, *outs_s)
```

```python
import functools
import math

import jax
import jax.numpy as jnp
from jax import lax
from jax.experimental import pallas as pl
from jax.experimental.pallas import tpu as pltpu

F32 = jnp.float32
BF16 = jnp.bfloat16
I32 = jnp.int32

HEAD_DIM = 128
N_HEADS = 8
N_KV = 4
GQA = N_HEADS // N_KV
N_IDX_HEADS = 16
IDX_DIM = 64
IDX_WEIGHT_SCALE = (N_IDX_HEADS * IDX_DIM) ** -0.5
DSA_TOPK = 256
MOBA_BLOCK = 256
MOBA_TOPK = 3
N_BUCKETS = 32
MAX_DISTANCE = 128
EPS = 1e-6
NEG = -1e30
ATTN_SCALE = HEAD_DIM ** -0.5

SUBLANE = 8
LANE = 128
INT_MIN = -(2 ** 31)
VMEM_LIMIT_V7X = 56 * 1024 * 1024

TQ = 128
TK = 128
TK2 = MOBA_BLOCK
CPP = TK2 // TK
KV_W = N_KV * HEAD_DIM
Q_W = N_HEADS * HEAD_DIM


def _cparams(sem):
    return pltpu.CompilerParams(dimension_semantics=sem, vmem_limit_bytes=VMEM_LIMIT_V7X)


def _nt(a, b):
    return lax.dot_general(a, b, (((1,), (1,)), ((), ())), preferred_element_type=F32)


def _dot(a, b):
    return jnp.dot(a, b, preferred_element_type=F32)


def _bucket_lower_bounds():
    max_exact = N_BUCKETS // 2

    def bucket(n):
        if n < max_exact:
            return n
        large = max_exact + int(math.log(n / max_exact) / math.log(MAX_DISTANCE / max_exact)
                                * (N_BUCKETS - max_exact))
        return min(large, N_BUCKETS - 1)

    lows = [None] * N_BUCKETS
    for n in range(0, 4 * MAX_DISTANCE):
        b = bucket(n)
        if lows[b] is None:
            lows[b] = n
    return lows


_BUCKET_LOW = _bucket_lower_bounds()


def _bias_of_distance(dist, table_ref, h):
    out = jnp.full(dist.shape, table_ref[0, h], F32)
    for b in range(1, N_BUCKETS):
        if _BUCKET_LOW[b] is None:
            continue
        out = jnp.where(dist >= _BUCKET_LOW[b], table_ref[b, h], out)
    return out


def _bias_kernel(table_ref, mat_ref, tail_ref, *, tail_w):
    h = pl.program_id(0)
    kl = lax.broadcasted_iota(I32, (TK, TQ), 0)
    ql = lax.broadcasted_iota(I32, (TK, TQ), 1)
    for dd in range(3):
        mat_ref[dd, 0] = _bias_of_distance(ql - kl + dd * TQ, table_ref, h)
    lane = lax.broadcasted_iota(I32, (1, tail_w), 1)
    tail_ref[0] = _bias_of_distance(tail_w - lane, table_ref, h)


def _bias_tables(rel_bias, tail_w):
    n_h = rel_bias.shape[1]
    return pl.pallas_call(
        functools.partial(_bias_kernel, tail_w=tail_w),
        grid=(n_h,),
        in_specs=[pl.BlockSpec(memory_space=pltpu.SMEM)],
        out_specs=[pl.BlockSpec((3, 1, TK, TQ), lambda h: (0, h, 0, 0)),
                   pl.BlockSpec((1, 1, tail_w), lambda h: (h, 0, 0))],
        out_shape=[jax.ShapeDtypeStruct((3, n_h, TK, TQ), F32),
                   jax.ShapeDtypeStruct((n_h, 1, tail_w), F32)],
        compiler_params=_cparams(("arbitrary",)),
        name="bias_tables",
    )(rel_bias)


def _ada_kernel(c_ref, w_ref, b_ref, o_ref):
    c = c_ref[...]
    s = (c / (1.0 + jnp.exp(-c))).astype(BF16)
    o_ref[...] = _dot(s, w_ref[...].astype(BF16)) + b_ref[...]


def _adaln(c, w, b):
    n, d = c.shape
    tn = min(1024, d)
    return pl.pallas_call(
        _ada_kernel,
        grid=(w.shape[1] // tn,),
        in_specs=[pl.BlockSpec((n, d), lambda j: (0, 0)),
                  pl.BlockSpec((d, tn), lambda j: (0, j)),
                  pl.BlockSpec((1, tn), lambda j: (0, j))],
        out_specs=pl.BlockSpec((n, tn), lambda j: (0, j)),
        out_shape=jax.ShapeDtypeStruct((n, w.shape[1]), F32),
        compiler_params=_cparams(("arbitrary",)),
        name="adaln",
    )(c, w, b.reshape(1, -1))


def _row_tile(r):
    return min(r, 512)


def _tok_spec(tr, w):
    return pl.BlockSpec((1, tr, w), lambda g, r: (g, r, 0))


def _mod_spec(mod, tr):
    if mod.shape[1] == 1:
        return pl.BlockSpec((1, 1, mod.shape[2]), lambda g, r: (g, 0, 0))
    return pl.BlockSpec((1, tr, mod.shape[2]), lambda g, r: (g, r, 0))


def _const_spec(shape):
    nd = len(shape)
    return pl.BlockSpec(shape, lambda g, r: (0,) * nd)


def _rms(x, g):
    return x * lax.rsqrt(jnp.mean(x * x, axis=-1, keepdims=True) + EPS) * g


def _normmod_kernel(x_ref, g_ref, sh_ref, sc_ref, o_ref):
    y = _rms(x_ref[0], g_ref[...])
    o_ref[0] = (y * (1.0 + sc_ref[0]) + sh_ref[0]).astype(BF16)


def _normmod(x, g, sh, sc):
    n_g, n_r, d = x.shape
    tr = _row_tile(n_r)
    return pl.pallas_call(
        _normmod_kernel,
        grid=(n_g, n_r // tr),
        in_specs=[_tok_spec(tr, d), _const_spec((1, d)), _mod_spec(sh, tr), _mod_spec(sc, tr)],
        out_specs=_tok_spec(tr, d),
        out_shape=jax.ShapeDtypeStruct((n_g, n_r, d), BF16),
        compiler_params=_cparams(("arbitrary", "arbitrary")),
        name="norm_modulate",
    )(x, g.reshape(1, d), sh, sc)


def _head_norm_store(z, g, o_ref, n, dtype):
    for hh in range(n):
        cols = slice(hh * HEAD_DIM, (hh + 1) * HEAD_DIM)
        o_ref[0, :, cols] = _rms(z[:, cols], g).astype(dtype)


def _kv_rows_store(z, o_ref, g=None):
    for kh in range(N_KV):
        v = z[:, kh * HEAD_DIM:(kh + 1) * HEAD_DIM]
        o_ref[0, pl.ds(kh, z.shape[0], stride=N_KV), :] = v if g is None else _rms(v, g)


def _kv_rows_spec(tr):
    return pl.BlockSpec((1, tr * N_KV, HEAD_DIM), lambda g, r: (g, r, 0))


_A_QA = (0, Q_W)
_A_KA = (_A_QA[1], _A_QA[1] + KV_W)
_A_VA = (_A_KA[1], _A_KA[1] + KV_W)
_A_QI = (_A_VA[1], _A_VA[1] + N_IDX_HEADS * IDX_DIM)
_A_MISC = (_A_QI[1], _A_QI[1] + LANE)
_WI_LANE = IDX_DIM


def _proj_a_kernel(h_ref, w_ref, gq_ref, gk_ref, gki_ref,
                   qa_o, ka_o, va_o, qi_o, ki_o, misc_o, kilo_o, kihi_o):
    h = h_ref[0]
    _head_norm_store(_dot(h, w_ref[:, _A_QA[0]:_A_QA[1]]), gq_ref[...], qa_o, N_HEADS, BF16)
    _kv_rows_store(_dot(h, w_ref[:, _A_KA[0]:_A_KA[1]]), ka_o, gk_ref[...])
    _kv_rows_store(_dot(h, w_ref[:, _A_VA[0]:_A_VA[1]]), va_o)
    qi_o[0] = _dot(h, w_ref[:, _A_QI[0]:_A_QI[1]]).astype(BF16)
    z = _dot(h, w_ref[:, _A_MISC[0]:_A_MISC[1]])
    is_ki = lax.broadcasted_iota(I32, z.shape, 1) < IDX_DIM
    ssq = jnp.sum(jnp.where(is_ki, z * z, 0.0), axis=-1, keepdims=True) * (1.0 / IDX_DIM)
    kin = jnp.where(is_ki, z * lax.rsqrt(ssq + EPS) * gki_ref[...], 0.0)
    ki_o[0] = kin[:, :IDX_DIM]
    misc_o[0] = jnp.where(is_ki, kin, z * IDX_WEIGHT_SCALE)
    kilo_o[0] = kin.astype(BF16)
    kihi_o[0] = pltpu.roll(kin, IDX_DIM, axis=1).astype(BF16)


def _proj_a(h, w_a, g_qa, g_ka, g_kidx_pad):
    n_g, n_r, d = h.shape
    tr = _row_tile(n_r)
    widths = [(Q_W, BF16), None, None, (N_IDX_HEADS * IDX_DIM, BF16),
              (IDX_DIM, F32), (LANE, F32), (LANE, BF16), (LANE, BF16)]
    kv_rows = jax.ShapeDtypeStruct((n_g, n_r * N_KV, HEAD_DIM), F32)
    return pl.pallas_call(
        _proj_a_kernel,
        grid=(n_g, n_r // tr),
        in_specs=[_tok_spec(tr, d), _const_spec(w_a.shape), _const_spec((1, HEAD_DIM)),
                  _const_spec((1, HEAD_DIM)), _const_spec((1, LANE))],
        out_specs=[_kv_rows_spec(tr) if w is None else _tok_spec(tr, w[0]) for w in widths],
        out_shape=[kv_rows if w is None else jax.ShapeDtypeStruct((n_g, n_r, w[0]), w[1])
                   for w in widths],
        compiler_params=_cparams(("arbitrary", "arbitrary")),
        name="proj_dsa",
    )(h, w_a, g_qa.reshape(1, -1), g_ka.reshape(1, -1), g_kidx_pad)


def _proj_b_kernel(h_ref, w_ref, gq_ref, gk_ref, qb_o, kb_o, vb_o):
    h = h_ref[0]
    _head_norm_store(_dot(h, w_ref[:, 0:Q_W]), gq_ref[...], qb_o, N_HEADS, BF16)
    _kv_rows_store(_dot(h, w_ref[:, Q_W:Q_W + KV_W]), kb_o, gk_ref[...])
    _kv_rows_store(_dot(h, w_ref[:, Q_W + KV_W:Q_W + 2 * KV_W]), vb_o)


def _proj_b(h, w_b, g_qb, g_kb):
    n_g, n_r, d = h.shape
    tr = _row_tile(n_r)
    kv_rows = jax.ShapeDtypeStruct((n_g, n_r * N_KV, HEAD_DIM), F32)
    return pl.pallas_call(
        _proj_b_kernel,
        grid=(n_g, n_r // tr),
        in_specs=[_tok_spec(tr, d), _const_spec(w_b.shape), _const_spec((1, HEAD_DIM)),
                  _const_spec((1, HEAD_DIM))],
        out_specs=[_tok_spec(tr, Q_W), _kv_rows_spec(tr), _kv_rows_spec(tr)],
        out_shape=[jax.ShapeDtypeStruct((n_g, n_r, Q_W), BF16), kv_rows, kv_rows],
        compiler_params=_cparams(("arbitrary", "arbitrary")),
        name="proj_moba",
    )(h, w_b, g_qb.reshape(1, -1), g_kb.reshape(1, -1))


def _sigmoid(x):
    return 1.0 / (1.0 + jnp.exp(-x))


def _gate_merge_kernel(h_ref, oa_ref, ob_ref, wga_ref, wgb_ref, wpa_ref, wpb_ref, m_o):
    h = h_ref[0]
    m = (_sigmoid(_dot(h, wga_ref[...])) * _dot(oa_ref[0], wpa_ref[...])
         + _sigmoid(_dot(h, wgb_ref[...])) * _dot(ob_ref[0], wpb_ref[...]))
    m_o[0] = m.astype(BF16)


def _gate_merge(h, oa, ob, w_ga, w_gb, w_pa, w_pb):
    n_g, n_r, d = h.shape
    tr = _row_tile(n_r)
    tn = d // 2
    tok = lambda w: pl.BlockSpec((1, tr, w), lambda c, g, r: (g, r, 0))
    col = lambda k: pl.BlockSpec((k, tn), lambda c, g, r: (0, c))
    return pl.pallas_call(
        _gate_merge_kernel,
        grid=(d // tn, n_g, n_r // tr),
        in_specs=[tok(d), tok(Q_W), tok(Q_W), col(d), col(d), col(Q_W), col(Q_W)],
        out_specs=pl.BlockSpec((1, tr, tn), lambda c, g, r: (g, r, c)),
        out_shape=jax.ShapeDtypeStruct((n_g, n_r, d), BF16),
        compiler_params=_cparams(("arbitrary", "arbitrary", "arbitrary")),
        name="gate_merge",
    )(h, oa, ob, w_ga, w_gb, w_pa, w_pb)


def _out_proj_kernel(m_ref, x_ref, w_ref, gt_ref, g2_ref, sh_ref, sc_ref, x2_o, h2_o):
    x2 = x_ref[0] + gt_ref[0] * _dot(m_ref[0], w_ref[...])
    x2_o[0] = x2
    h2_o[0] = (_rms(x2, g2_ref[...]) * (1.0 + sc_ref[0]) + sh_ref[0]).astype(BF16)


def _out_proj(m, x, w_out, gt, g2, sh, sc):
    n_g, n_r, d = x.shape
    tr = _row_tile(n_r)
    return pl.pallas_call(
        _out_proj_kernel,
        grid=(n_g, n_r // tr),
        in_specs=[_tok_spec(tr, d), _tok_spec(tr, d), _const_spec(w_out.shape), _mod_spec(gt, tr),
                  _const_spec((1, d)), _mod_spec(sh, tr), _mod_spec(sc, tr)],
        out_specs=[_tok_spec(tr, d), _tok_spec(tr, d)],
        out_shape=[jax.ShapeDtypeStruct((n_g, n_r, d), F32),
                   jax.ShapeDtypeStruct((n_g, n_r, d), BF16)],
        compiler_params=_cparams(("arbitrary", "arbitrary")),
        name="out_proj",
    )(m, x, w_out, gt, g2.reshape(1, d), sh, sc)


def _ffn_kernel(h_ref, x_ref, wa_ref, wu_ref, wo_ref, gt_ref, y_o, acc_s):
    j = pl.program_id(2)

    @pl.when(j == 0)
    def _():
        acc_s[...] = jnp.zeros_like(acc_s)

    h = h_ref[0]
    a = _dot(h, wa_ref[...])
    u = _dot(h, wu_ref[...])
    act = (a * _sigmoid(a) * u).astype(BF16)
    acc_s[...] += _dot(act, wo_ref[...])

    @pl.when(j == pl.num_programs(2) - 1)
    def _():
        y_o[0] = x_ref[0] + gt_ref[0] * acc_s[...]


def _ffn(h2, x2, w_in, w_out, gt):
    n_g, n_r, d = x2.shape
    d_ff = w_out.shape[0]
    tr = _row_tile(n_r)
    tf = 512 if d_ff % 512 == 0 else 256
    n_f = d_ff // tf
    tok = lambda w: pl.BlockSpec((1, tr, w), lambda g, r, j: (g, r, 0))
    if gt.shape[1] == 1:
        gt_spec = pl.BlockSpec((1, 1, d), lambda g, r, j: (g, 0, 0))
    else:
        gt_spec = pl.BlockSpec((1, tr, d), lambda g, r, j: (g, r, 0))
    return pl.pallas_call(
        _ffn_kernel,
        grid=(n_g, n_r // tr, n_f),
        in_specs=[tok(d), tok(d),
                  pl.BlockSpec((d, tf), lambda g, r, j: (0, j)),
                  pl.BlockSpec((d, tf), lambda g, r, j: (0, j + n_f)),
                  pl.BlockSpec((tf, d), lambda g, r, j: (j, 0)),
                  gt_spec],
        out_specs=tok(d),
        out_shape=jax.ShapeDtypeStruct((n_g, n_r, d), F32),
        scratch_shapes=[pltpu.VMEM((tr, d), F32)],
        compiler_params=_cparams(("arbitrary", "arbitrary", "arbitrary")),
        name="swiglu_ffn",
    )(h2, x2, w_in, w_in, w_out, gt)


def _sortable(x):
    bits = pltpu.bitcast(x, I32)
    return bits ^ (lax.shift_right_arithmetic(bits, 31) & 0x7FFFFFFF)


def _stage_kv(k_ref, v_ref, k_s, vt_s, n_blocks):
    def body(c, carry):
        rows = pl.ds(pl.multiple_of(c * TK2, TK2), TK2)
        for kh in range(N_KV):
            cols = slice(kh * HEAD_DIM, (kh + 1) * HEAD_DIM)
            src = pl.ds(c * (TK2 * N_KV) + kh, TK2, stride=N_KV)
            k_s[rows, cols] = k_ref[0, src, :].astype(BF16)
            vt_s[c, cols, :] = v_ref[0, src, :].T.astype(BF16)
        return carry
    lax.fori_loop(0, n_blocks, body, 0)


def _fold8(x, op):
    r, w = x.shape
    return op(x.reshape(r // SUBLANE, SUBLANE, w), axis=0)


def _loop_pairs(n, body, init):
    carry = lax.fori_loop(0, n // 2, lambda j, c: body(2 * j + 1, body(2 * j, c)), init)
    return lax.cond(n % 2 == 1, lambda c: body(n - 1, c), lambda c: c, carry)


def _attend(i, n_pc, mask_of, q_ref, k_s, vt_s, bias_ref, lg_s, acc_s, o_ref):
    def logits_body(c2, mx):
        rows = pl.ds(pl.multiple_of(c2 * TK2, TK2), TK2)
        out = []
        for kh in range(N_KV):
            q2 = jnp.concatenate(
                [q_ref[0, :, (kh * GQA + g) * HEAD_DIM:(kh * GQA + g + 1) * HEAD_DIM]
                 for g in range(GQA)], axis=0)
            lg = _nt(k_s[rows, kh * HEAD_DIM:(kh + 1) * HEAD_DIM], q2) * ATTN_SCALE
            best = mx[kh]
            for a in range(CPP):
                dd = jnp.clip(i - (c2 * CPP + a), 0, 2)
                folded = []
                for g in range(GQA):
                    h = kh * GQA + g
                    t = lg[a * TK:(a + 1) * TK, g * TQ:(g + 1) * TQ] + bias_ref[dd, h]
                    t = jnp.where(mask_of(c2, a, h), t, NEG)
                    lg_s[kh, c2, a * TK:(a + 1) * TK, g * TQ:(g + 1) * TQ] = t
                    folded.append(_fold8(t, jnp.max))
                best = jnp.maximum(best, jnp.concatenate(folded, axis=1))
            out.append(best)
        return tuple(out)

    mx0 = tuple(jnp.full((SUBLANE, GQA * TQ), NEG, F32) for _ in range(N_KV))
    mx = _loop_pairs(n_pc, logits_body, mx0)
    m = [jnp.max(x, axis=0, keepdims=True) for x in mx]

    acc_s[...] = jnp.zeros(acc_s.shape, F32)

    def pv_body(c2, ls):
        out = []
        for kh in range(N_KV):
            p = jnp.exp(lg_s[kh, c2] - m[kh])
            out.append(ls[kh] + _fold8(p, jnp.sum))
            acc_s[kh] += _dot(vt_s[c2, kh * HEAD_DIM:(kh + 1) * HEAD_DIM, :], p.astype(BF16))
        return tuple(out)

    ls0 = tuple(jnp.zeros((SUBLANE, GQA * TQ), F32) for _ in range(N_KV))
    ls = _loop_pairs(n_pc, pv_body, ls0)
    for kh in range(N_KV):
        o2 = acc_s[kh] * (1.0 / jnp.sum(ls[kh], axis=0, keepdims=True))
        for g in range(GQA):
            h = kh * GQA + g
            o_ref[0, :, h * HEAD_DIM:(h + 1) * HEAD_DIM] = o2[:, g * TQ:(g + 1) * TQ].T.astype(BF16)


def _dsa_kernel(qi_ref, misc_ref, kilo_ref, kihi_ref, qa_ref, ka_ref, va_ref, bias_ref, o_ref,
                k_s, vt_s, sk_s, w_s, lg_s, acc_s, *, n_blocks, top_k, pos_bits):
    i = pl.program_id(1)
    n_pc = i // CPP + 1
    kl = lax.broadcasted_iota(I32, (TK, TQ), 0)
    ql = lax.broadcasted_iota(I32, (TK, TQ), 1)
    qpos = i * TQ + lax.broadcasted_iota(I32, (1, TQ), 1)

    @pl.when(i == 0)
    def _():
        _stage_kv(ka_ref, va_ref, k_s, vt_s, n_blocks)

    w_s[...] = misc_ref[0].T

    def score_body(c2, carry):
        rows = pl.ds(pl.multiple_of(c2 * TK2, TK2), TK2)
        k2 = jnp.concatenate([kilo_ref[0, rows, :], kihi_ref[0, rows, :]], axis=0)
        acc = [jnp.zeros((TK, TQ), F32) for _ in range(CPP)]
        for p2 in range(N_IDX_HEADS // 4):
            qp = jnp.concatenate([qi_ref[0, :, (2 * p2 + pair) * LANE:(2 * p2 + pair + 1) * LANE]
                                  for pair in range(2)], axis=0)
            s = jnp.maximum(_nt(k2, qp), 0.0)
            for half in range(2):
                for pair in range(2):
                    r = _WI_LANE + 4 * p2 + 2 * pair + half
                    w = w_s[r:r + 1, :]
                    for a in range(CPP):
                        r0 = half * TK2 + a * TK
                        acc[a] = acc[a] + w * s[r0:r0 + TK, pair * TQ:(pair + 1) * TQ]
        for a in range(CPP):
            c = c2 * CPP + a
            admissible = (c * TK + kl) <= (i * TQ + ql)
            sk_s[c] = jnp.where(admissible, _sortable(acc[a]), INT_MIN)
        return carry
    lax.fori_loop(0, n_pc, score_body, 0)

    def count(pred):
        def body(c2, acc):
            for a in range(CPP):
                c = c2 * CPP + a
                acc = acc + _fold8(jnp.where(pred(c, sk_s[c]), 1.0, 0.0), jnp.sum)
            return acc
        acc = _loop_pairs(n_pc, body, jnp.zeros((SUBLANE, TQ), F32))
        return jnp.sum(acc, axis=0, keepdims=True)

    kf = float(top_k)
    few = qpos + 1 <= top_k

    def threshold():
        zero = jnp.zeros((1, TQ), I32)
        t0 = jnp.where(count(lambda c, s: s >= zero) >= kf, zero, INT_MIN)

        def bit_body(s, t):
            cand = t + lax.shift_left(jnp.int32(1), 30 - s)
            return jnp.where(count(lambda c, sk: sk >= cand) >= kf, cand, t)
        return lax.fori_loop(0, 31, bit_body, t0)

    t = lax.cond((i + 1) * TQ > top_k, threshold, lambda: jnp.full((1, TQ), INT_MIN + 1, I32))
    t = jnp.where(few, INT_MIN + 1, t)

    @pl.when((i + 1) * TQ > top_k)
    def _():
        need = kf - count(lambda c, sk: sk > t)
        n_eq = count(lambda c, sk: sk == t)
        tied = jnp.max(jnp.where(jnp.logical_and(n_eq > need, jnp.logical_not(few)), 1.0, 0.0))

        @pl.when(tied > 0.0)
        def _():
            def pos_body(s, lo):
                cand = lo + lax.shift_left(jnp.int32(1), pos_bits - 1 - s)
                below = count(lambda c, sk: jnp.logical_and(sk == t, c * TK + kl < cand))
                return jnp.where(below < need, cand, lo)
            last = lax.fori_loop(0, pos_bits, pos_body, jnp.zeros((1, TQ), I32))

            def drop_body(c, carry):
                sk = sk_s[c]
                drop = jnp.logical_and(sk == t, c * TK + kl > last)
                sk_s[c] = jnp.where(drop, INT_MIN, sk)
                return carry
            lax.fori_loop(0, i + 1, drop_body, 0)

    _attend(i, n_pc, lambda c2, a, h: sk_s[c2 * CPP + a] >= t, qa_ref, k_s, vt_s, bias_ref,
            lg_s, acc_s, o_ref)


def _attn_scratch(n_blocks):
    return [pltpu.VMEM((n_blocks * TK2, KV_W), BF16),
            pltpu.VMEM((n_blocks, KV_W, TK2), BF16)]


def _softmax_scratch(n_blocks):
    return [pltpu.VMEM((N_KV, n_blocks, TK2, GQA * TQ), F32),
            pltpu.VMEM((N_KV, HEAD_DIM, GQA * TQ), F32)]


def _dsa_prompt(qi, misc, kilo, kihi, qa, ka, va, bias):
    n_b, n_s, _ = qa.shape
    n_blocks = n_s // TK2
    top_k = min(DSA_TOPK, n_s // 4)
    qblk = lambda w: pl.BlockSpec((1, TQ, w), lambda b, i: (b, i, 0))
    seq = lambda w: pl.BlockSpec((1, n_s, w), lambda b, i: (b, 0, 0))
    kv_rows = pl.BlockSpec((1, n_s * N_KV, HEAD_DIM), lambda b, i: (b, 0, 0))
    return pl.pallas_call(
        functools.partial(_dsa_kernel, n_blocks=n_blocks, top_k=top_k,
                          pos_bits=max(1, (n_s - 1).bit_length())),
        grid=(n_b, n_s // TQ),
        in_specs=[qblk(N_IDX_HEADS * IDX_DIM), qblk(LANE), seq(LANE), seq(LANE), qblk(Q_W),
                  kv_rows, kv_rows,
                  pl.BlockSpec((3, N_HEADS, TK, TQ), lambda b, i: (0, 0, 0, 0))],
        out_specs=qblk(Q_W),
        out_shape=jax.ShapeDtypeStruct((n_b, n_s, Q_W), BF16),
        scratch_shapes=_attn_scratch(n_blocks) + [
            pltpu.VMEM((n_s // TK, TK, TQ), I32),
            pltpu.VMEM((LANE, TQ), F32),
        ] + _softmax_scratch(n_blocks),
        compiler_params=_cparams(("arbitrary", "arbitrary")),
        name="dsa_prompt",
    )(qi, misc, kilo, kihi, qa, ka, va, bias)


def _split_hi_lo(x):
    hi = x.astype(BF16)
    return hi, (x - hi.astype(F32)).astype(BF16)


def _moba_kernel(qb_ref, kb_ref, vb_ref, bias_ref, o_ref,
                 k_s, vt_s, kmean_s, sel_s, lg_s, acc_s, *, n_blocks, n_sel):
    i = pl.program_id(1)
    cur = (i * TQ) // MOBA_BLOCK
    kl = lax.broadcasted_iota(I32, (TK, TQ), 0)
    ql = lax.broadcasted_iota(I32, (TK, TQ), 1)

    @pl.when(i == 0)
    def _():
        _stage_kv(kb_ref, vb_ref, k_s, vt_s, n_blocks)
        for n in range(n_blocks):
            blk = kb_ref[0, n * MOBA_BLOCK * N_KV:(n + 1) * MOBA_BLOCK * N_KV, :]
            s8 = _fold8(blk, jnp.sum)
            kmean_s[n] = (s8[:N_KV] + s8[N_KV:]) * (1.0 / MOBA_BLOCK)

    blk_id = lax.broadcasted_iota(I32, (n_blocks, TQ), 0)
    for kh in range(N_KV):
        km_hi, km_lo = _split_hi_lo(kmean_s[:, kh, :])
        for g in range(GQA):
            h = kh * GQA + g
            qh = qb_ref[0, :, h * HEAD_DIM:(h + 1) * HEAD_DIM]
            gate = _nt(km_hi, qh) + _nt(km_lo, qh)
            gate = jnp.where(blk_id < cur, gate, NEG)
            for n in range(n_blocks):
                gn = gate[n:n + 1, :]
                beats = jnp.logical_or(gate > gn, jnp.logical_and(gate == gn, blk_id < n))
                rank = jnp.sum(jnp.where(beats, 1.0, 0.0), axis=0, keepdims=True)
                keep = jnp.where(rank < n_sel, 1.0, 0.0)
                sel_s[h, n:n + 1, :] = jnp.where(n < cur, keep, 0.0)

    def mask_of(c2, a, h):
        causal = (c2 * TK2 + a * TK + kl) <= (i * TQ + ql)
        row = jnp.where(c2 == cur, 1.0, sel_s[h, pl.ds(c2, 1), :])
        return jnp.where(causal, jnp.broadcast_to(row, (TK, TQ)), 0.0) > 0.0

    _attend(i, cur + 1, mask_of, qb_ref, k_s, vt_s, bias_ref, lg_s, acc_s, o_ref)


def _moba_prompt(qb, kb, vb, bias):
    n_b, n_s, _ = qb.shape
    n_blocks = n_s // MOBA_BLOCK
    qblk = lambda w: pl.BlockSpec((1, TQ, w), lambda b, i: (b, i, 0))
    kv_rows = pl.BlockSpec((1, n_s * N_KV, HEAD_DIM), lambda b, i: (b, 0, 0))
    return pl.pallas_call(
        functools.partial(_moba_kernel, n_blocks=n_blocks, n_sel=min(MOBA_TOPK, n_blocks)),
        grid=(n_b, n_s // TQ),
        in_specs=[qblk(Q_W), kv_rows, kv_rows,
                  pl.BlockSpec((3, N_HEADS, TK, TQ), lambda b, i: (0, 0, 0, 0))],
        out_specs=qblk(Q_W),
        out_shape=jax.ShapeDtypeStruct((n_b, n_s, Q_W), BF16),
        scratch_shapes=_attn_scratch(n_blocks) + [
            pltpu.VMEM((n_blocks, N_KV, HEAD_DIM), F32),
            pltpu.VMEM((N_HEADS, n_blocks, TQ), F32),
        ] + _softmax_scratch(n_blocks),
        compiler_params=_cparams(("arbitrary", "arbitrary")),
        name="moba_prompt",
    )(qb, kb, vb, bias)


SPAN = 2048
SPAN_ROWS = SPAN * N_KV


def _page_copies(pt_ref, b, first_page, n_pages, pool_ref, buf, sem, rows):
    return [pltpu.make_async_copy(pool_ref.at[pt_ref[b, first_page + p]],
                                  buf.at[pl.ds(p * rows, rows)], sem)
            for p in range(n_pages)]


def _own_kv_lane(shape):
    row = lax.broadcasted_iota(I32, shape, 0)
    lane = lax.broadcasted_iota(I32, shape, 1)
    return (lane & (N_KV - 1)) == row // GQA


def _sidx_kernel(pt_ref, qi_ref, wi_ref, kin_ref, pool_ref, mask_o, new_o,
                 xbuf, sem, s_s, *, n_spans, n_pages, page, top_k):
    b = pl.program_id(0)
    n_b = pl.num_programs(0)
    slot = lax.rem(b, 2)

    def copies(bb, sl):
        return [pltpu.make_async_copy(pool_ref.at[pt_ref[bb, p]],
                                      xbuf.at[sl, :, pl.ds(p * page, page)], sem.at[sl])
                for p in range(n_pages)]

    @pl.when(b == 0)
    def _():
        for cp in copies(b, slot):
            cp.start()

    @pl.when(b + 1 < n_b)
    def _():
        for cp in copies(b + 1, 1 - slot):
            cp.start()

    for cp in copies(b, slot):
        cp.wait()

    q = qi_ref[0]
    w = wi_ref[0]
    for j in range(n_spans):
        x = xbuf[slot, :, j * SPAN:(j + 1) * SPAN].astype(BF16)
        s = jnp.sum(w * jnp.maximum(_dot(q, x), 0.0), axis=0, keepdims=True)
        s_s[j:j + 1, :] = _sortable(s + 0.0)
    kin = kin_ref[0].astype(BF16).astype(F32)
    s_new = jnp.sum(q.astype(F32) * kin, axis=1, keepdims=True)
    s_new = jnp.sum(w * jnp.maximum(s_new, 0.0), axis=0, keepdims=True)
    key_new = _sortable(s_new + 0.0)
    sk = s_s[...]

    def count(pred_past, pred_new):
        c = jnp.sum(jnp.where(pred_past, 1.0, 0.0), axis=1, keepdims=True)
        return jnp.sum(c, axis=0, keepdims=True) + jnp.where(pred_new, 1.0, 0.0)

    kf = float(top_k)
    zero = jnp.zeros((1, 1), I32)
    t0 = jnp.where(count(sk >= zero, key_new >= zero) >= kf, zero, INT_MIN)

    def ge(cand):
        return count(sk >= cand, key_new >= cand) >= kf

    def bit_pair_body(s, t):
        step = lax.shift_left(jnp.int32(1), 29 - 2 * s)
        c1, c2, c3 = t + step, t + 2 * step, t + 3 * step
        return jnp.where(ge(c3), c3, jnp.where(ge(c2), c2, jnp.where(ge(c1), c1, t)))
    t = lax.fori_loop(0, 15, bit_pair_body, t0)
    t = jnp.where(ge(t + 1), t + 1, t)

    need = kf - count(sk > t, key_new > t)
    eq = sk == t
    n_eq_past = count(eq, False)
    mask_o[0] = jnp.where(sk >= t, 1.0, 0.0)
    new_o[0] = jnp.broadcast_to(jnp.where(key_new >= t, 1.0, 0.0), (1, LANE))

    n_eq = n_eq_past + jnp.where(key_new == t, 1.0, 0.0)

    @pl.when(jnp.max(n_eq - need) > 0.0)
    def _():
        pos = (lax.broadcasted_iota(I32, sk.shape, 0) * SPAN
               + lax.broadcasted_iota(I32, sk.shape, 1))
        pos_bits = max(1, (n_spans * SPAN - 1).bit_length())

        def pos_body(s, lo):
            cand = lo + lax.shift_left(jnp.int32(1), pos_bits - 1 - s)
            below = count(jnp.logical_and(eq, pos < cand), False)
            return jnp.where(below < need, cand, lo)
        last = lax.fori_loop(0, pos_bits, pos_body, jnp.zeros((1, 1), I32))
        sel = jnp.logical_or(sk > t, jnp.logical_and(eq, pos <= last))
        mask_o[0] = jnp.where(sel, 1.0, 0.0)
        new_sel = jnp.logical_or(key_new > t, jnp.logical_and(key_new == t, n_eq_past < need))
        new_o[0] = jnp.broadcast_to(jnp.where(new_sel, 1.0, 0.0), (1, LANE))


def _sample_index(page_table, qi, wi, ki_new, pool_t):
    n_b, n_pages = page_table.shape
    page = pool_t.shape[2]
    n_spans = n_pages * page // SPAN
    top_k = min(DSA_TOPK, (n_pages * page + 1) // 4)
    grid_spec = pltpu.PrefetchScalarGridSpec(
        num_scalar_prefetch=1,
        grid=(n_b,),
        in_specs=[pl.BlockSpec((1, N_IDX_HEADS, IDX_DIM), lambda b, pt: (b, 0, 0)),
                  pl.BlockSpec((1, N_IDX_HEADS, 1), lambda b, pt: (b, 0, 0)),
                  pl.BlockSpec((1, 1, IDX_DIM), lambda b, pt: (b, 0, 0)),
                  pl.BlockSpec(memory_space=pl.ANY)],
        out_specs=[pl.BlockSpec((1, n_spans, SPAN), lambda b, pt: (b, 0, 0)),
                   pl.BlockSpec((1, 1, LANE), lambda b, pt: (b, 0, 0))],
        scratch_shapes=[pltpu.VMEM((2, IDX_DIM, n_pages * page), F32),
                        pltpu.SemaphoreType.DMA((2,)),
                        pltpu.VMEM((n_spans, SPAN), I32)],
    )
    return pl.pallas_call(
        functools.partial(_sidx_kernel, n_spans=n_spans, n_pages=n_pages, page=page, top_k=top_k),
        grid_spec=grid_spec,
        out_shape=[jax.ShapeDtypeStruct((n_b, n_spans, SPAN), F32),
                   jax.ShapeDtypeStruct((n_b, 1, LANE), F32)],
        compiler_params=_cparams(("arbitrary",)),
        name="sample_indexer",
    )(page_table, qi, wi, ki_new, pool_t)


def _rows_per_kv(x):
    return jnp.concatenate(
        [jnp.broadcast_to(x[:, kh * HEAD_DIM:(kh + 1) * HEAD_DIM], (GQA, HEAD_DIM))
         for kh in range(N_KV)], axis=0)


def _sdsa_kernel(pt_ref, mask_ref, new_ref, q_ref, kn_ref, vn_ref, tail_ref, far_ref, b0_ref,
                 kpool_ref, vpool_ref, o_ref, kbuf, vbuf, sem, m_s, l_s, acc_s, *, n_spans, page):
    b = pl.program_id(0)
    j = pl.program_id(1)
    n_b = pl.num_programs(0)
    pps = SPAN // page
    step = b * n_spans + j
    slot = lax.rem(step, 2)

    def copies(bb, jj, sl):
        rows = page * N_KV
        return (_page_copies(pt_ref, bb, jj * pps, pps, kpool_ref, kbuf.at[sl], sem.at[0, sl], rows)
                + _page_copies(pt_ref, bb, jj * pps, pps, vpool_ref, vbuf.at[sl], sem.at[1, sl], rows))

    @pl.when(step == 0)
    def _():
        for cp in copies(b, j, slot):
            cp.start()

    @pl.when(step + 1 < n_b * n_spans)
    def _():
        nxt = step + 1
        for cp in copies(nxt // n_spans, lax.rem(nxt, n_spans), 1 - slot):
            cp.start()

    q = q_ref[0]

    @pl.when(j == 0)
    def _():
        qf = q.astype(F32)
        kn = _rows_per_kv(kn_ref[0].astype(BF16).astype(F32))
        lg = jnp.sum(qf * kn, axis=1, keepdims=True) * ATTN_SCALE + b0_ref[...]
        keep = new_ref[0][:, 0:1] > 0.0
        m_s[...] = jnp.where(keep, lg, NEG)
        l_s[...] = jnp.where(keep, jnp.ones_like(lg), 0.0)
        vn = _rows_per_kv(vn_ref[0].astype(BF16).astype(F32))
        acc_s[...] = jnp.where(keep, vn, 0.0)

    for cp in copies(b, j, slot):
        cp.wait()

    kc = kbuf[slot].astype(BF16)
    vc = vbuf[slot].astype(BF16)
    bias = jnp.where(j == n_spans - 1, tail_ref[...], far_ref[...])
    mk = jnp.logical_and(mask_ref[0, pl.ds(j, 1), :] > 0.0, _own_kv_lane((N_HEADS, SPAN_ROWS)))
    lg = jnp.where(mk, _nt(q, kc) * ATTN_SCALE + bias, NEG)
    m_old = m_s[...]
    m_new = jnp.maximum(m_old, jnp.max(lg, axis=1, keepdims=True))
    alpha = jnp.exp(m_old - m_new)
    p = jnp.where(mk, jnp.exp(lg - m_new), 0.0)
    l_s[...] = alpha * l_s[...] + jnp.sum(p, axis=1, keepdims=True)
    acc_s[...] = alpha * acc_s[...] + _dot(p.astype(BF16), vc)
    m_s[...] = m_new

    @pl.when(j == n_spans - 1)
    def _():
        o_ref[0] = (acc_s[...] * (1.0 / l_s[...])).astype(BF16)


def _sample_dsa(page_table, mask4, new_sel, qa, ka_new, va_new, tail4, far, bias0, k_pool, v_pool):
    n_b, n_pages = page_table.shape
    page = k_pool.shape[1] // N_KV
    n_spans = n_pages * page // SPAN
    per_b = lambda shape: pl.BlockSpec((1,) + shape, lambda b, j, pt: (b, 0, 0))
    const = lambda shape: pl.BlockSpec(shape, lambda b, j, pt: (0, 0))
    grid_spec = pltpu.PrefetchScalarGridSpec(
        num_scalar_prefetch=1,
        grid=(n_b, n_spans),
        in_specs=[per_b((n_spans, SPAN_ROWS)), per_b((1, LANE)), per_b((N_HEADS, HEAD_DIM)),
                  per_b((1, KV_W)), per_b((1, KV_W)),
                  const((N_HEADS, SPAN_ROWS)), const((N_HEADS, 1)), const((N_HEADS, 1)),
                  pl.BlockSpec(memory_space=pl.ANY), pl.BlockSpec(memory_space=pl.ANY)],
        out_specs=per_b((N_HEADS, HEAD_DIM)),
        scratch_shapes=[pltpu.VMEM((2, SPAN_ROWS, HEAD_DIM), F32),
                        pltpu.VMEM((2, SPAN_ROWS, HEAD_DIM), F32),
                        pltpu.SemaphoreType.DMA((2, 2)),
                        pltpu.VMEM((N_HEADS, 1), F32), pltpu.VMEM((N_HEADS, 1), F32),
                        pltpu.VMEM((N_HEADS, HEAD_DIM), F32)],
    )
    return pl.pallas_call(
        functools.partial(_sdsa_kernel, n_spans=n_spans, page=page),
        grid_spec=grid_spec,
        out_shape=jax.ShapeDtypeStruct((n_b, N_HEADS, HEAD_DIM), BF16),
        compiler_params=_cparams(("arbitrary", "arbitrary")),
        name="sample_dsa",
    )(page_table, mask4, new_sel, qa, ka_new, va_new, tail4, far, bias0, k_pool, v_pool)


def _smoba_gate_kernel(pt_ref, q_ref, kpool_ref, sel_o, kbuf, sem, kmean_s,
                       *, n_spans, page, n_sel, n_blocks):
    b = pl.program_id(0)
    j = pl.program_id(1)
    n_b = pl.num_programs(0)
    pps = SPAN // page
    bps = SPAN // MOBA_BLOCK
    blk_rows = MOBA_BLOCK * N_KV
    step = b * n_spans + j
    slot = lax.rem(step, 2)

    def copies(bb, jj, sl):
        return _page_copies(pt_ref, bb, jj * pps, pps, kpool_ref, kbuf.at[sl], sem.at[sl],
                            page * N_KV)

    @pl.when(step == 0)
    def _():
        for cp in copies(b, j, slot):
            cp.start()

    @pl.when(step + 1 < n_b * n_spans)
    def _():
        nxt = step + 1
        for cp in copies(nxt // n_spans, lax.rem(nxt, n_spans), 1 - slot):
            cp.start()

    for cp in copies(b, j, slot):
        cp.wait()

    means = []
    for n in range(bps):
        s8 = _fold8(kbuf[slot, n * blk_rows:(n + 1) * blk_rows, :], jnp.sum)
        means.append((s8[:N_KV] + s8[N_KV:]) * (1.0 / MOBA_BLOCK))
    kmean_s[pl.ds(pl.multiple_of(j * bps * N_KV, bps * N_KV), bps * N_KV), :] = (
        jnp.concatenate(means, axis=0))

    @pl.when(j == n_spans - 1)
    def _():
        km_hi, km_lo = _split_hi_lo(kmean_s[...])
        q = q_ref[0]
        gate = _nt(q, km_hi) + _nt(q, km_lo)
        gate = jnp.where(_own_kv_lane(gate.shape), gate, -jnp.inf)
        lane = lax.broadcasted_iota(I32, gate.shape, 1).astype(F32)
        out_lane = lax.broadcasted_iota(I32, (N_HEADS, LANE), 1)
        picked = jnp.zeros((N_HEADS, LANE), F32)
        for r in range(n_sel):
            best = jnp.max(gate, axis=1, keepdims=True)
            idx = jnp.min(jnp.where(gate == best, lane, float(n_blocks * N_KV)), axis=1, keepdims=True)
            picked = jnp.where(out_lane == r, idx, picked)
            gate = jnp.where(lane == idx, -jnp.inf, gate)
        sel_o[0] = picked.astype(I32) // N_KV


def _sample_moba_gate(page_table, qb, k_pool):
    n_b, n_pages = page_table.shape
    page = k_pool.shape[1] // N_KV
    n_spans = n_pages * page // SPAN
    n_blocks = (n_pages * page + 1) // MOBA_BLOCK
    grid_spec = pltpu.PrefetchScalarGridSpec(
        num_scalar_prefetch=1,
        grid=(n_b, n_spans),
        in_specs=[pl.BlockSpec((1, N_HEADS, HEAD_DIM), lambda b, j, pt: (b, 0, 0)),
                  pl.BlockSpec(memory_space=pl.ANY)],
        out_specs=pl.BlockSpec((1, N_HEADS, LANE), lambda b, j, pt: (b, 0, 0)),
        scratch_shapes=[pltpu.VMEM((2, SPAN_ROWS, HEAD_DIM), F32), pltpu.SemaphoreType.DMA((2,)),
                        pltpu.VMEM((n_blocks * N_KV, HEAD_DIM), F32)],
    )
    return pl.pallas_call(
        functools.partial(_smoba_gate_kernel, n_spans=n_spans, page=page,
                          n_sel=min(MOBA_TOPK, n_blocks), n_blocks=n_blocks),
        grid_spec=grid_spec,
        out_shape=jax.ShapeDtypeStruct((n_b, N_HEADS, LANE), I32),
        compiler_params=_cparams(("arbitrary", "arbitrary")),
        name="sample_moba_gate",
    )(page_table, qb, k_pool)


def _smoba_attn_kernel(pt_ref, sel_ref, selv_ref, q_ref, kn_ref, vn_ref, tail_ref, far_ref, b0_ref,
                       kpool_ref, vpool_ref, o_ref, kbuf, vbuf, sem, *, n_sel, page, n_blocks):
    b = pl.program_id(0)
    n_b = pl.num_programs(0)
    ppb = MOBA_BLOCK // page
    slot = lax.rem(b, 2)
    n_keys = n_sel * MOBA_BLOCK

    def copies(bb, sl):
        out = []
        for h in range(N_HEADS):
            kh = h // GQA
            for r in range(n_sel):
                blk = sel_ref[(bb * N_HEADS + h) * n_sel + r]
                for p in range(ppb):
                    pg = pt_ref[bb, blk * ppb + p]
                    rows = pl.ds((r * ppb + p) * page, page)
                    out.append(pltpu.make_async_copy(kpool_ref.at[pg, :, kh, :],
                                                     kbuf.at[sl, h, rows], sem.at[0, sl]))
                    out.append(pltpu.make_async_copy(vpool_ref.at[pg, :, kh, :],
                                                     vbuf.at[sl, h, rows], sem.at[1, sl]))
        return out

    @pl.when(b == 0)
    def _():
        for cp in copies(b, slot):
            cp.start()

    @pl.when(b + 1 < n_b)
    def _():
        for cp in copies(b + 1, 1 - slot):
            cp.start()

    for cp in copies(b, slot):
        cp.wait()

    q = q_ref[0]
    row = lax.broadcasted_iota(I32, (N_HEADS, n_keys), 0)
    lg = jnp.zeros((N_HEADS, n_keys), F32)
    for h in range(N_HEADS):
        lg = jnp.where(row == h, _nt(q, kbuf[slot, h].astype(BF16)), lg)
    selv = selv_ref[0]
    tail = tail_ref[...]
    far = far_ref[...]
    bias = jnp.concatenate(
        [jnp.where(selv[:, r:r + 1] == n_blocks - 1, tail, far) for r in range(n_sel)], axis=1)
    lg = lg * ATTN_SCALE + bias
    qf = q.astype(F32)
    kn = _rows_per_kv(kn_ref[0].astype(BF16).astype(F32))
    lg_new = jnp.sum(qf * kn, axis=1, keepdims=True) * ATTN_SCALE + b0_ref[...]
    m = jnp.maximum(jnp.max(lg, axis=1, keepdims=True), lg_new)
    p = jnp.exp(lg - m)
    p_new = jnp.exp(lg_new - m)
    denom = jnp.sum(p, axis=1, keepdims=True) + p_new
    pb = p.astype(BF16)
    acc = p_new.astype(BF16).astype(F32) * _rows_per_kv(vn_ref[0].astype(BF16).astype(F32))
    row_o = lax.broadcasted_iota(I32, (N_HEADS, HEAD_DIM), 0)
    for h in range(N_HEADS):
        acc = acc + jnp.where(row_o == h, _dot(pb, vbuf[slot, h].astype(BF16)), 0.0)
    o_ref[0] = (acc * (1.0 / denom)).astype(BF16)


def _sample_moba_attn(page_table, sel, qb, kb_new, vb_new, tail, far, bias0, k_pool, v_pool):
    n_b, n_pages = page_table.shape
    page = k_pool.shape[1]
    n_blocks = (n_pages * page + 1) // MOBA_BLOCK
    n_sel = min(MOBA_TOPK, n_blocks)
    sel_flat = sel[:, :, :n_sel].reshape(-1)
    per_b = lambda shape: pl.BlockSpec((1,) + shape, lambda b, pt, sf: (b, 0, 0))
    const = lambda shape: pl.BlockSpec(shape, lambda b, pt, sf: (0, 0))
    grid_spec = pltpu.PrefetchScalarGridSpec(
        num_scalar_prefetch=2,
        grid=(n_b,),
        in_specs=[per_b((N_HEADS, LANE)), per_b((N_HEADS, HEAD_DIM)), per_b((1, KV_W)),
                  per_b((1, KV_W)), const((N_HEADS, MOBA_BLOCK)), const((N_HEADS, 1)),
                  const((N_HEADS, 1)),
                  pl.BlockSpec(memory_space=pl.ANY), pl.BlockSpec(memory_space=pl.ANY)],
        out_specs=per_b((N_HEADS, HEAD_DIM)),
        scratch_shapes=[pltpu.VMEM((2, N_HEADS, n_sel * MOBA_BLOCK, HEAD_DIM), F32),
                        pltpu.VMEM((2, N_HEADS, n_sel * MOBA_BLOCK, HEAD_DIM), F32),
                        pltpu.SemaphoreType.DMA((2, 2))],
    )
    return pl.pallas_call(
        functools.partial(_smoba_attn_kernel, n_sel=n_sel, page=page, n_blocks=n_blocks),
        grid_spec=grid_spec,
        out_shape=jax.ShapeDtypeStruct((n_b, N_HEADS, HEAD_DIM), BF16),
        compiler_params=_cparams(("arbitrary",)),
        name="sample_moba_attn",
    )(page_table, sel_flat, sel, qb, kb_new, vb_new, tail, far, bias0, k_pool, v_pool)


def _split_w_in(w_in):
    widths = (Q_W, KV_W, KV_W, N_IDX_HEADS * IDX_DIM, IDX_DIM, N_IDX_HEADS, Q_W, KV_W, KV_W,
              w_in.shape[0], w_in.shape[0])
    pts = [0]
    for w in widths:
        pts.append(pts[-1] + w)
    qa, ka, va, qi, ki, wi, qb, kb, vb, ga, gb = [w_in[:, pts[n]:pts[n + 1]] for n in range(11)]
    pad = jnp.zeros((w_in.shape[0], LANE - IDX_DIM - N_IDX_HEADS), w_in.dtype)
    w_a = jnp.concatenate([qa, ka, va, qi, ki, wi, pad], axis=1).astype(BF16)
    w_b = jnp.concatenate([qb, kb, vb], axis=1).astype(BF16)
    return w_a, w_b, ga.astype(BF16), gb.astype(BF16)


def kernel(x_prompt, x_sample, cache_ka, cache_va, cache_kidx, cache_kb, cache_vb, page_table,
           c_prompt, c_sample, rel_bias, w_ada, b_ada, g_norm1, w_in, g_qa, g_ka, g_kidx, g_qb,
           g_kb, w_pa, w_pb, w_out, g_norm2, w_ffn_in, w_ffn_out):
    depth = w_in.shape[0]
    n_b, n_s, d = x_prompt.shape
    n_db, n_ds, _ = x_sample.shape
    n_pool, page = cache_ka.shape[1], cache_ka.shape[2]
    past_len = page_table.shape[1] * page
    assert n_ds == 1, "the decode kernels handle one new token per sequence"
    assert n_s % MOBA_BLOCK == 0 and past_len % SPAN == 0 and SPAN % page == 0
    assert MOBA_BLOCK % page == 0 and past_len >= 4 * DSA_TOPK

    bias_mat, bias_tail = _bias_tables(rel_bias, SPAN)
    mat_a, mat_b = bias_mat[:, :N_HEADS], bias_mat[:, N_HEADS:]
    tail = bias_tail[:, 0, :]
    tail_a4 = jnp.repeat(tail[:N_HEADS], N_KV, axis=1)
    tail_b = tail[N_HEADS:, SPAN - MOBA_BLOCK:]
    far = rel_bias[N_BUCKETS - 1].reshape(-1, 1)
    bias0 = rel_bias[0].reshape(-1, 1)

    xp = x_prompt
    xs = x_sample.reshape(1, n_db, d)
    c_all = jnp.concatenate([c_prompt, c_sample], axis=0)
    rows_p, rows_s = [], []
    for l in range(depth):
        mod = _adaln(c_all, w_ada[l], b_ada[l])
        mod_p = [m.reshape(n_b, 1, d) for m in jnp.split(mod[:n_b], 6, axis=-1)]
        mod_s = [m.reshape(1, n_db, d) for m in jnp.split(mod[n_b:], 6, axis=-1)]
        w_a, w_b, w_ga, w_gb = _split_w_in(w_in[l])
        w_pa_l, w_pb_l, w_out_l = w_pa[l].astype(BF16), w_pb[l].astype(BF16), w_out[l].astype(BF16)
        w_fi, w_fo = w_ffn_in[l].astype(BF16), w_ffn_out[l].astype(BF16)
        g_kidx_pad = jnp.pad(g_kidx[l], (0, LANE - IDX_DIM)).reshape(1, LANE)

        def project(x, mods):
            h = _normmod(x, g_norm1[l], mods[0], mods[1])
            return h, _proj_a(h, w_a, g_qa[l], g_ka[l], g_kidx_pad), _proj_b(h, w_b, g_qb[l], g_kb[l])

        def finish(x, h, oa, ob, mods):
            m = _gate_merge(h, oa, ob, w_ga, w_gb, w_pa_l, w_pb_l)
            x2, h2 = _out_proj(m, x, w_out_l, mods[2], g_norm2[l], mods[3], mods[4])
            return _ffn(h2, x2, w_fi, w_fo, mods[5])

        h, (qa, ka, va, qi, ki, misc, kilo, kihi), (qb, kb, vb) = project(xp, mod_p)
        oa = _dsa_prompt(qi, misc, kilo, kihi, qa, ka, va, mat_a)
        ob = _moba_prompt(qb, kb, vb, mat_b)
        xp = finish(xp, h, oa, ob, mod_p)
        rows_p.append((ka.reshape(n_b, n_s, N_KV, HEAD_DIM), va.reshape(n_b, n_s, N_KV, HEAD_DIM),
                       ki, kb.reshape(n_b, n_s, N_KV, HEAD_DIM), vb.reshape(n_b, n_s, N_KV, HEAD_DIM)))

        h, (qa, ka, va, qi, ki, misc, kilo, kihi), (qb, kb, vb) = project(xs, mod_s)
        tok = lambda t: t.reshape(n_db, 1, -1)
        heads = lambda t: t.reshape(n_db, N_HEADS, HEAD_DIM)
        rows_view = lambda c: c[l].reshape(n_pool, page * N_KV, HEAD_DIM)
        wi = misc[0, :, _WI_LANE:_WI_LANE + N_IDX_HEADS].reshape(n_db, N_IDX_HEADS, 1)
        mask, new_sel = _sample_index(page_table, qi.reshape(n_db, N_IDX_HEADS, IDX_DIM), wi,
                                      tok(ki[0]), jnp.swapaxes(cache_kidx[l], 1, 2))
        oa = _sample_dsa(page_table, jnp.repeat(mask, N_KV, axis=2), new_sel, heads(qa[0]),
                         tok(ka[0]), tok(va[0]), tail_a4, far[:N_HEADS], bias0[:N_HEADS],
                         rows_view(cache_ka), rows_view(cache_va))
        sel = _sample_moba_gate(page_table, heads(qb[0]), rows_view(cache_kb))
        ob = _sample_moba_attn(page_table, sel, heads(qb[0]), tok(kb[0]), tok(vb[0]),
                               tail_b, far[N_HEADS:], bias0[N_HEADS:], cache_kb[l], cache_vb[l])
        xs = finish(xs, h, oa.reshape(1, n_db, Q_W), ob.reshape(1, n_db, Q_W), mod_s)
        kv4 = lambda t: t.reshape(n_db, 1, N_KV, HEAD_DIM)
        rows_s.append((kv4(ka[0]), kv4(va[0]), ki.reshape(n_db, 1, IDX_DIM), kv4(kb[0]), kv4(vb[0])))

    outs_p = [jnp.stack(t) for t in zip(*rows_p)]
    outs_s = [jnp.stack(t) for t in zip(*rows_s)]
    return (xp, xs.reshape(n_db, n_ds, d), *outs_p, *outs_s)
```

```python
import functools
import math

import jax
import jax.numpy as jnp
from jax import lax
from jax.experimental import pallas as pl
from jax.experimental.pallas import tpu as pltpu

F32 = jnp.float32
BF16 = jnp.bfloat16
I32 = jnp.int32

HEAD_DIM = 128
N_HEADS = 8
N_KV = 4
GQA = N_HEADS // N_KV
N_IDX_HEADS = 16
IDX_DIM = 64
IDX_WEIGHT_SCALE = (N_IDX_HEADS * IDX_DIM) ** -0.5
DSA_TOPK = 256
MOBA_BLOCK = 256
MOBA_TOPK = 3
N_BUCKETS = 32
MAX_DISTANCE = 128
EPS = 1e-6
NEG = -1e30
ATTN_SCALE = HEAD_DIM ** -0.5

SUBLANE = 8
LANE = 128
INT_MIN = -(2 ** 31)
VMEM_LIMIT_V7X = 56 * 1024 * 1024

TQ = 128
TK = 128
TK2 = MOBA_BLOCK
CPP = TK2 // TK
KV_W = N_KV * HEAD_DIM
Q_W = N_HEADS * HEAD_DIM


def _cparams(sem):
    return pltpu.CompilerParams(dimension_semantics=sem, vmem_limit_bytes=VMEM_LIMIT_V7X)


def _nt(a, b):
    return lax.dot_general(a, b, (((1,), (1,)), ((), ())), preferred_element_type=F32)


def _dot(a, b):
    return jnp.dot(a, b, preferred_element_type=F32)


def _bucket_lower_bounds():
    max_exact = N_BUCKETS // 2

    def bucket(n):
        if n < max_exact:
            return n
        large = max_exact + int(math.log(n / max_exact) / math.log(MAX_DISTANCE / max_exact)
                                * (N_BUCKETS - max_exact))
        return min(large, N_BUCKETS - 1)

    lows = [None] * N_BUCKETS
    for n in range(0, 4 * MAX_DISTANCE):
        b = bucket(n)
        if lows[b] is None:
            lows[b] = n
    return lows


_BUCKET_LOW = _bucket_lower_bounds()


def _bias_of_distance(dist, table_ref, h):
    out = jnp.full(dist.shape, table_ref[0, h], F32)
    for b in range(1, N_BUCKETS):
        if _BUCKET_LOW[b] is None:
            continue
        out = jnp.where(dist >= _BUCKET_LOW[b], table_ref[b, h], out)
    return out


def _bias_kernel(table_ref, mat_ref, tail_ref, *, tail_w):
    h = pl.program_id(0)
    kl = lax.broadcasted_iota(I32, (TK, TQ), 0)
    ql = lax.broadcasted_iota(I32, (TK, TQ), 1)
    for dd in range(3):
        mat_ref[dd, 0] = _bias_of_distance(ql - kl + dd * TQ, table_ref, h)
    lane = lax.broadcasted_iota(I32, (1, tail_w), 1)
    tail_ref[0] = _bias_of_distance(tail_w - lane, table_ref, h)


def _bias_tables(rel_bias, tail_w):
    n_h = rel_bias.shape[1]
    return pl.pallas_call(
        functools.partial(_bias_kernel, tail_w=tail_w),
        grid=(n_h,),
        in_specs=[pl.BlockSpec(memory_space=pltpu.SMEM)],
        out_specs=[pl.BlockSpec((3, 1, TK, TQ), lambda h: (0, h, 0, 0)),
                   pl.BlockSpec((1, 1, tail_w), lambda h: (h, 0, 0))],
        out_shape=[jax.ShapeDtypeStruct((3, n_h, TK, TQ), F32),
                   jax.ShapeDtypeStruct((n_h, 1, tail_w), F32)],
        compiler_params=_cparams(("arbitrary",)),
        name="bias_tables",
    )(rel_bias)


def _ada_kernel(c_ref, w_ref, b_ref, o_ref):
    c = c_ref[...]
    s = (c / (1.0 + jnp.exp(-c))).astype(BF16)
    o_ref[...] = _dot(s, w_ref[...].astype(BF16)) + b_ref[...]


def _adaln(c, w, b):
    n, d = c.shape
    tn = min(1024, d)
    return pl.pallas_call(
        _ada_kernel,
        grid=(w.shape[1] // tn,),
        in_specs=[pl.BlockSpec((n, d), lambda j: (0, 0)),
                  pl.BlockSpec((d, tn), lambda j: (0, j)),
                  pl.BlockSpec((1, tn), lambda j: (0, j))],
        out_specs=pl.BlockSpec((n, tn), lambda j: (0, j)),
        out_shape=jax.ShapeDtypeStruct((n, w.shape[1]), F32),
        compiler_params=_cparams(("arbitrary",)),
        name="adaln",
    )(c, w, b.reshape(1, -1))


def _row_tile(r):
    return min(r, 512)


def _tok_spec(tr, w):
    return pl.BlockSpec((1, tr, w), lambda g, r: (g, r, 0))


def _mod_spec(mod, tr):
    if mod.shape[1] == 1:
        return pl.BlockSpec((1, 1, mod.shape[2]), lambda g, r: (g, 0, 0))
    return pl.BlockSpec((1, tr, mod.shape[2]), lambda g, r: (g, r, 0))


def _const_spec(shape):
    nd = len(shape)
    return pl.BlockSpec(shape, lambda g, r: (0,) * nd, pipeline_mode=pl.Buffered(1))


def _rms(x, g):
    return x * lax.rsqrt(jnp.mean(x * x, axis=-1, keepdims=True) + EPS) * g


def _head_norm_store(z, g, o_ref, n, dtype):
    for hh in range(n):
        cols = slice(hh * HEAD_DIM, (hh + 1) * HEAD_DIM)
        o_ref[0, :, cols] = _rms(z[:, cols], g).astype(dtype)


def _kv_rows_store(z, o_ref, g=None):
    for kh in range(N_KV):
        v = z[:, kh * HEAD_DIM:(kh + 1) * HEAD_DIM]
        o_ref[0, pl.ds(kh, z.shape[0], stride=N_KV), :] = v if g is None else _rms(v, g)


def _kv_rows_spec(tr):
    return pl.BlockSpec((1, tr * N_KV, HEAD_DIM), lambda g, r: (g, r, 0))


_A_QA = (0, Q_W)
_A_KA = (_A_QA[1], _A_QA[1] + KV_W)
_A_VA = (_A_KA[1], _A_KA[1] + KV_W)
_A_QI = (_A_VA[1], _A_VA[1] + N_IDX_HEADS * IDX_DIM)
_A_MISC = (_A_QI[1], _A_QI[1] + LANE)
_WI_LANE = IDX_DIM


def _proj_a_kernel(x_ref, g1_ref, sh_ref, sc_ref, w_ref, gq_ref, gk_ref, gki_ref,
                   h_o, qa_o, ka_o, va_o, qi_o, ki_o, misc_o, kilo_o, kihi_o):
    h = (_rms(x_ref[0], g1_ref[...]) * (1.0 + sc_ref[0]) + sh_ref[0]).astype(BF16)
    h_o[0] = h
    _head_norm_store(_dot(h, w_ref[:, _A_QA[0]:_A_QA[1]]), gq_ref[...], qa_o, N_HEADS, BF16)
    _kv_rows_store(_dot(h, w_ref[:, _A_KA[0]:_A_KA[1]]), ka_o, gk_ref[...])
    _kv_rows_store(_dot(h, w_ref[:, _A_VA[0]:_A_VA[1]]), va_o)
    qi_o[0] = _dot(h, w_ref[:, _A_QI[0]:_A_QI[1]]).astype(BF16)
    z = _dot(h, w_ref[:, _A_MISC[0]:_A_MISC[1]])
    is_ki = lax.broadcasted_iota(I32, z.shape, 1) < IDX_DIM
    ssq = jnp.sum(jnp.where(is_ki, z * z, 0.0), axis=-1, keepdims=True) * (1.0 / IDX_DIM)
    kin = jnp.where(is_ki, z * lax.rsqrt(ssq + EPS) * gki_ref[...], 0.0)
    ki_o[0] = kin[:, :IDX_DIM]
    misc_o[0] = jnp.where(is_ki, kin, z * IDX_WEIGHT_SCALE)
    kilo_o[0] = kin.astype(BF16)
    kihi_o[0] = pltpu.roll(kin, IDX_DIM, axis=1).astype(BF16)


def _proj_a(x, g1, sh, sc, w_a, g_qa, g_ka, g_kidx_pad):
    n_g, n_r, d = x.shape
    tr = _row_tile(n_r)
    widths = [(d, BF16), (Q_W, BF16), None, None, (N_IDX_HEADS * IDX_DIM, BF16),
              (IDX_DIM, F32), (LANE, F32), (LANE, BF16), (LANE, BF16)]
    kv_rows = jax.ShapeDtypeStruct((n_g, n_r * N_KV, HEAD_DIM), F32)
    return pl.pallas_call(
        _proj_a_kernel,
        grid=(n_g, n_r // tr),
        in_specs=[_tok_spec(tr, d), _const_spec((1, d)), _mod_spec(sh, tr), _mod_spec(sc, tr),
                  _const_spec(w_a.shape), _const_spec((1, HEAD_DIM)),
                  _const_spec((1, HEAD_DIM)), _const_spec((1, LANE))],
        out_specs=[_kv_rows_spec(tr) if w is None else _tok_spec(tr, w[0]) for w in widths],
        out_shape=[kv_rows if w is None else jax.ShapeDtypeStruct((n_g, n_r, w[0]), w[1])
                   for w in widths],
        compiler_params=_cparams(("arbitrary", "arbitrary")),
        name="proj_dsa",
    )(x, g1.reshape(1, d), sh, sc, w_a, g_qa.reshape(1, -1), g_ka.reshape(1, -1), g_kidx_pad)


def _proj_b_kernel(h_ref, w_ref, gq_ref, gk_ref, qb_o, kb_o, vb_o):
    h = h_ref[0]
    _head_norm_store(_dot(h, w_ref[:, 0:Q_W]), gq_ref[...], qb_o, N_HEADS, BF16)
    _kv_rows_store(_dot(h, w_ref[:, Q_W:Q_W + KV_W]), kb_o, gk_ref[...])
    _kv_rows_store(_dot(h, w_ref[:, Q_W + KV_W:Q_W + 2 * KV_W]), vb_o)


def _proj_b(h, w_b, g_qb, g_kb):
    n_g, n_r, d = h.shape
    tr = _row_tile(n_r)
    kv_rows = jax.ShapeDtypeStruct((n_g, n_r * N_KV, HEAD_DIM), F32)
    return pl.pallas_call(
        _proj_b_kernel,
        grid=(n_g, n_r // tr),
        in_specs=[_tok_spec(tr, d), _const_spec(w_b.shape), _const_spec((1, HEAD_DIM)),
                  _const_spec((1, HEAD_DIM))],
        out_specs=[_tok_spec(tr, Q_W), _kv_rows_spec(tr), _kv_rows_spec(tr)],
        out_shape=[jax.ShapeDtypeStruct((n_g, n_r, Q_W), BF16), kv_rows, kv_rows],
        compiler_params=_cparams(("arbitrary", "arbitrary")),
        name="proj_moba",
    )(h, w_b, g_qb.reshape(1, -1), g_kb.reshape(1, -1))


def _sigmoid(x):
    return 1.0 / (1.0 + jnp.exp(-x))


def _gate_merge_kernel(h_ref, oa_ref, ob_ref, wga_ref, wgb_ref, wpa_ref, wpb_ref, m_o):
    h = h_ref[0]
    m = (_sigmoid(_dot(h, wga_ref[...])) * _dot(oa_ref[0], wpa_ref[...])
         + _sigmoid(_dot(h, wgb_ref[...])) * _dot(ob_ref[0], wpb_ref[...]))
    m_o[0] = m.astype(BF16)


def _gate_merge(h, oa, ob, w_ga, w_gb, w_pa, w_pb):
    n_g, n_r, d = h.shape
    tr = _row_tile(n_r)
    tn = d // 2
    tok = lambda w: pl.BlockSpec((1, tr, w), lambda c, g, r: (g, r, 0))
    col = lambda k: pl.BlockSpec((k, tn), lambda c, g, r: (0, c))
    return pl.pallas_call(
        _gate_merge_kernel,
        grid=(d // tn, n_g, n_r // tr),
        in_specs=[tok(d), tok(Q_W), tok(Q_W), col(d), col(d), col(Q_W), col(Q_W)],
        out_specs=pl.BlockSpec((1, tr, tn), lambda c, g, r: (g, r, c)),
        out_shape=jax.ShapeDtypeStruct((n_g, n_r, d), BF16),
        compiler_params=_cparams(("arbitrary", "arbitrary", "arbitrary")),
        name="gate_merge",
    )(h, oa, ob, w_ga, w_gb, w_pa, w_pb)


def _out_proj_kernel(m_ref, x_ref, w_ref, gt_ref, g2_ref, sh_ref, sc_ref, x2_o, h2_o):
    x2 = x_ref[0] + gt_ref[0] * _dot(m_ref[0], w_ref[...])
    x2_o[0] = x2
    h2_o[0] = (_rms(x2, g2_ref[...]) * (1.0 + sc_ref[0]) + sh_ref[0]).astype(BF16)


def _out_proj(m, x, w_out, gt, g2, sh, sc):
    n_g, n_r, d = x.shape
    tr = _row_tile(n_r)
    return pl.pallas_call(
        _out_proj_kernel,
        grid=(n_g, n_r // tr),
        in_specs=[_tok_spec(tr, d), _tok_spec(tr, d), _const_spec(w_out.shape), _mod_spec(gt, tr),
                  _const_spec((1, d)), _mod_spec(sh, tr), _mod_spec(sc, tr)],
        out_specs=[_tok_spec(tr, d), _tok_spec(tr, d)],
        out_shape=[jax.ShapeDtypeStruct((n_g, n_r, d), F32),
                   jax.ShapeDtypeStruct((n_g, n_r, d), BF16)],
        compiler_params=_cparams(("arbitrary", "arbitrary")),
        name="out_proj",
    )(m, x, w_out, gt, g2.reshape(1, d), sh, sc)


def _ffn_kernel(h_ref, x_ref, wa_ref, wu_ref, wo_ref, gt_ref, y_o):
    j = pl.program_id(2)

    @pl.when(j == 0)
    def _():
        y_o[...] = jnp.zeros_like(y_o)

    h = h_ref[0]
    a = _dot(h, wa_ref[...])
    u = _dot(h, wu_ref[...])
    act = (a * _sigmoid(a) * u).astype(BF16)
    y_o[0] += _dot(act, wo_ref[...])

    @pl.when(j == pl.num_programs(2) - 1)
    def _():
        y_o[0] = x_ref[0] + gt_ref[0] * y_o[0]


def _ffn(h2, x2, w_in, w_out, gt):
    n_g, n_r, d = x2.shape
    d_ff = w_out.shape[0]
    tr = min(n_r, 1024)
    tf = 512 if d_ff % 512 == 0 else 256
    n_f = d_ff // tf
    tok = lambda w: pl.BlockSpec((1, tr, w), lambda g, r, j: (g, r, 0))
    if gt.shape[1] == 1:
        gt_spec = pl.BlockSpec((1, 1, d), lambda g, r, j: (g, 0, 0))
    else:
        gt_spec = pl.BlockSpec((1, tr, d), lambda g, r, j: (g, r, 0))
    return pl.pallas_call(
        _ffn_kernel,
        grid=(n_g, n_r // tr, n_f),
        in_specs=[tok(d),
                  pl.BlockSpec((1, tr, d), lambda g, r, j: (g, r, 0), pipeline_mode=pl.Buffered(1)),
                  pl.BlockSpec((d, tf), lambda g, r, j: (0, j)),
                  pl.BlockSpec((d, tf), lambda g, r, j: (0, j + n_f)),
                  pl.BlockSpec((tf, d), lambda g, r, j: (j, 0)),
                  gt_spec],
        out_specs=tok(d),
        out_shape=jax.ShapeDtypeStruct((n_g, n_r, d), F32),
        compiler_params=_cparams(("arbitrary", "arbitrary", "arbitrary")),
        name="swiglu_ffn",
    )(h2, x2, w_in, w_in, w_out, gt)


def _sortable(x):
    bits = pltpu.bitcast(x, I32)
    return bits ^ (lax.shift_right_arithmetic(bits, 31) & 0x7FFFFFFF)


def _stage_kv(k_ref, v_ref, k_s, vt_s, n_blocks):
    def body(c, carry):
        rows = pl.ds(pl.multiple_of(c * TK2, TK2), TK2)
        for kh in range(N_KV):
            cols = slice(kh * HEAD_DIM, (kh + 1) * HEAD_DIM)
            src = pl.ds(c * (TK2 * N_KV) + kh, TK2, stride=N_KV)
            k_s[rows, cols] = k_ref[0, src, :].astype(BF16)
            vt_s[c, cols, :] = v_ref[0, src, :].T.astype(BF16)
        return carry
    lax.fori_loop(0, n_blocks, body, 0)


def _fold8(x, op):
    r, w = x.shape
    return op(x.reshape(r // SUBLANE, SUBLANE, w), axis=0)


def _loop_pairs(n, body, init):
    carry = lax.fori_loop(0, n // 2, lambda j, c: body(2 * j + 1, body(2 * j, c)), init)
    return lax.cond(n % 2 == 1, lambda c: body(n - 1, c), lambda c: c, carry)


def _attend(i, n_pc, mask_of, q_ref, k_s, vt_s, bias_ref, lg_s, acc_s, o_ref):
    def logits_body(c2, mx):
        rows = pl.ds(pl.multiple_of(c2 * TK2, TK2), TK2)
        out = []
        for kh in range(N_KV):
            q2 = jnp.concatenate(
                [q_ref[0, :, (kh * GQA + g) * HEAD_DIM:(kh * GQA + g + 1) * HEAD_DIM]
                 for g in range(GQA)], axis=0)
            lg = _nt(k_s[rows, kh * HEAD_DIM:(kh + 1) * HEAD_DIM], q2) * ATTN_SCALE
            best = mx[kh]
            for a in range(CPP):
                dd = jnp.clip(i - (c2 * CPP + a), 0, 2)
                folded = []
                for g in range(GQA):
                    h = kh * GQA + g
                    t = lg[a * TK:(a + 1) * TK, g * TQ:(g + 1) * TQ] + bias_ref[dd, h]
                    t = jnp.where(mask_of(c2, a, h), t, NEG)
                    lg_s[kh, c2, a * TK:(a + 1) * TK, g * TQ:(g + 1) * TQ] = t
                    folded.append(_fold8(t, jnp.max))
                best = jnp.maximum(best, jnp.concatenate(folded, axis=1))
            out.append(best)
        return tuple(out)

    mx0 = tuple(jnp.full((SUBLANE, GQA * TQ), NEG, F32) for _ in range(N_KV))
    mx = _loop_pairs(n_pc, logits_body, mx0)
    m = [jnp.max(x, axis=0, keepdims=True) for x in mx]

    acc_s[...] = jnp.zeros(acc_s.shape, F32)

    def pv_body(c2, ls):
        out = []
        for kh in range(N_KV):
            p = jnp.exp(lg_s[kh, c2] - m[kh])
            out.append(ls[kh] + _fold8(p, jnp.sum))
            acc_s[kh] += _dot(vt_s[c2, kh * HEAD_DIM:(kh + 1) * HEAD_DIM, :], p.astype(BF16))
        return tuple(out)

    ls0 = tuple(jnp.zeros((SUBLANE, GQA * TQ), F32) for _ in range(N_KV))
    ls = _loop_pairs(n_pc, pv_body, ls0)
    for kh in range(N_KV):
        o2 = acc_s[kh] * (1.0 / jnp.sum(ls[kh], axis=0, keepdims=True))
        for g in range(GQA):
            h = kh * GQA + g
            o_ref[0, :, h * HEAD_DIM:(h + 1) * HEAD_DIM] = o2[:, g * TQ:(g + 1) * TQ].T.astype(BF16)


def _dsa_kernel(qi_ref, misc_ref, kilo_ref, kihi_ref, qa_ref, ka_ref, va_ref, bias_ref, o_ref,
                k_s, vt_s, sk_s, w_s, lg_s, acc_s, *, n_blocks, top_k, pos_bits):
    i = pl.program_id(1)
    n_pc = i // CPP + 1
    kl = lax.broadcasted_iota(I32, (TK, TQ), 0)
    ql = lax.broadcasted_iota(I32, (TK, TQ), 1)
    qpos = i * TQ + lax.broadcasted_iota(I32, (1, TQ), 1)

    @pl.when(i == 0)
    def _():
        _stage_kv(ka_ref, va_ref, k_s, vt_s, n_blocks)

    w_s[...] = misc_ref[0].T

    def score_body(c2, carry):
        rows = pl.ds(pl.multiple_of(c2 * TK2, TK2), TK2)
        k2 = jnp.concatenate([kilo_ref[0, rows, :], kihi_ref[0, rows, :]], axis=0)
        acc = [jnp.zeros((TK, TQ), F32) for _ in range(CPP)]
        for p2 in range(N_IDX_HEADS // 4):
            qp = jnp.concatenate([qi_ref[0, :, (2 * p2 + pair) * LANE:(2 * p2 + pair + 1) * LANE]
                                  for pair in range(2)], axis=0)
            s = jnp.maximum(_nt(k2, qp), 0.0)
            for half in range(2):
                for pair in range(2):
                    r = _WI_LANE + 4 * p2 + 2 * pair + half
                    w = w_s[r:r + 1, :]
                    for a in range(CPP):
                        r0 = half * TK2 + a * TK
                        acc[a] = acc[a] + w * s[r0:r0 + TK, pair * TQ:(pair + 1) * TQ]
        for a in range(CPP):
            c = c2 * CPP + a
            admissible = (c * TK + kl) <= (i * TQ + ql)
            sk_s[c] = jnp.where(admissible, _sortable(acc[a]), INT_MIN)
        return carry
    _loop_pairs(n_pc, score_body, 0)

    def count(pred):
        def body(c2, acc):
            for a in range(CPP):
                c = c2 * CPP + a
                acc = acc + _fold8(jnp.where(pred(c, sk_s[c]), 1.0, 0.0), jnp.sum)
            return acc
        acc = _loop_pairs(n_pc, body, jnp.zeros((SUBLANE, TQ), F32))
        return jnp.sum(acc, axis=0, keepdims=True)

    kf = float(top_k)
    few = qpos + 1 <= top_k

    def threshold():
        zero = jnp.zeros((1, TQ), I32)
        t0 = jnp.where(count(lambda c, s: s >= zero) >= kf, zero, INT_MIN)

        def bit_body(s, t):
            cand = t + lax.shift_left(jnp.int32(1), 30 - s)
            return jnp.where(count(lambda c, sk: sk >= cand) >= kf, cand, t)
        return lax.fori_loop(0, 31, bit_body, t0)

    t = lax.cond((i + 1) * TQ > top_k, threshold, lambda: jnp.full((1, TQ), INT_MIN + 1, I32))
    t = jnp.where(few, INT_MIN + 1, t)

    @pl.when((i + 1) * TQ > top_k)
    def _():
        need = kf - count(lambda c, sk: sk > t)
        n_eq = count(lambda c, sk: sk == t)
        tied = jnp.max(jnp.where(jnp.logical_and(n_eq > need, jnp.logical_not(few)), 1.0, 0.0))

        @pl.when(tied > 0.0)
        def _():
            def pos_body(s, lo):
                cand = lo + lax.shift_left(jnp.int32(1), pos_bits - 1 - s)
                below = count(lambda c, sk: jnp.logical_and(sk == t, c * TK + kl < cand))
                return jnp.where(below < need, cand, lo)
            last = lax.fori_loop(0, pos_bits, pos_body, jnp.zeros((1, TQ), I32))

            def drop_body(c, carry):
                sk = sk_s[c]
                drop = jnp.logical_and(sk == t, c * TK + kl > last)
                sk_s[c] = jnp.where(drop, INT_MIN, sk)
                return carry
            lax.fori_loop(0, i + 1, drop_body, 0)

    _attend(i, n_pc, lambda c2, a, h: sk_s[c2 * CPP + a] >= t, qa_ref, k_s, vt_s, bias_ref,
            lg_s, acc_s, o_ref)


def _attn_scratch(n_blocks):
    return [pltpu.VMEM((n_blocks * TK2, KV_W), BF16),
            pltpu.VMEM((n_blocks, KV_W, TK2), BF16)]


def _softmax_scratch(n_blocks):
    return [pltpu.VMEM((N_KV, n_blocks, TK2, GQA * TQ), F32),
            pltpu.VMEM((N_KV, HEAD_DIM, GQA * TQ), F32)]


def _dsa_prompt(qi, misc, kilo, kihi, qa, ka, va, bias):
    n_b, n_s, _ = qa.shape
    n_blocks = n_s // TK2
    top_k = min(DSA_TOPK, n_s // 4)
    qblk = lambda w: pl.BlockSpec((1, TQ, w), lambda b, i: (b, i, 0))
    seq = lambda w: pl.BlockSpec((1, n_s, w), lambda b, i: (b, 0, 0))
    kv_rows = pl.BlockSpec((1, n_s * N_KV, HEAD_DIM), lambda b, i: (b, 0, 0))
    return pl.pallas_call(
        functools.partial(_dsa_kernel, n_blocks=n_blocks, top_k=top_k,
                          pos_bits=max(1, (n_s - 1).bit_length())),
        grid=(n_b, n_s // TQ),
        in_specs=[qblk(N_IDX_HEADS * IDX_DIM), qblk(LANE), seq(LANE), seq(LANE), qblk(Q_W),
                  kv_rows, kv_rows,
                  pl.BlockSpec((3, N_HEADS, TK, TQ), lambda b, i: (0, 0, 0, 0))],
        out_specs=qblk(Q_W),
        out_shape=jax.ShapeDtypeStruct((n_b, n_s, Q_W), BF16),
        scratch_shapes=_attn_scratch(n_blocks) + [
            pltpu.VMEM((n_s // TK, TK, TQ), I32),
            pltpu.VMEM((LANE, TQ), F32),
        ] + _softmax_scratch(n_blocks),
        compiler_params=_cparams(("arbitrary", "arbitrary")),
        name="dsa_prompt",
    )(qi, misc, kilo, kihi, qa, ka, va, bias)


def _roll_rows(x, r):
    if x.shape[0] % SUBLANE == 0:
        return pltpu.roll(x, r, axis=0)
    return jnp.roll(x, r, axis=0)


def _split_hi_lo(x):
    hi = x.astype(BF16)
    return hi, (x - hi.astype(F32)).astype(BF16)


def _moba_kernel(qb_ref, kb_ref, vb_ref, bias_ref, o_ref,
                 k_s, vt_s, kmean_s, sel_s, lg_s, acc_s, *, n_blocks, n_sel):
    i = pl.program_id(1)
    cur = (i * TQ) // MOBA_BLOCK
    kl = lax.broadcasted_iota(I32, (TK, TQ), 0)
    ql = lax.broadcasted_iota(I32, (TK, TQ), 1)

    @pl.when(i == 0)
    def _():
        _stage_kv(kb_ref, vb_ref, k_s, vt_s, n_blocks)
        for n in range(n_blocks):
            blk = kb_ref[0, n * MOBA_BLOCK * N_KV:(n + 1) * MOBA_BLOCK * N_KV, :]
            s8 = _fold8(blk, jnp.sum)
            kmean_s[n] = (s8[:N_KV] + s8[N_KV:]) * (1.0 / MOBA_BLOCK)

    blk_id = lax.broadcasted_iota(I32, (n_blocks, TQ), 0)
    for kh in range(N_KV):
        km_hi, km_lo = _split_hi_lo(kmean_s[:, kh, :])
        for g in range(GQA):
            h = kh * GQA + g
            qh = qb_ref[0, :, h * HEAD_DIM:(h + 1) * HEAD_DIM]
            gate = _nt(km_hi, qh) + _nt(km_lo, qh)
            gate = jnp.where(blk_id < cur, gate, NEG)
            rank = jnp.zeros((n_blocks, TQ), F32)
            for r in range(1, n_blocks):
                other = _roll_rows(gate, r)
                beats = jnp.logical_or(other > gate,
                                       jnp.logical_and(other == gate, blk_id >= r))
                rank = rank + jnp.where(beats, 1.0, 0.0)
            keep = jnp.logical_and(rank < n_sel, blk_id < cur)
            sel_s[h] = jnp.where(keep, 1.0, 0.0)

    def mask_of(c2, a, h):
        causal = (c2 * TK2 + a * TK + kl) <= (i * TQ + ql)
        row = jnp.where(c2 == cur, 1.0, sel_s[h, pl.ds(c2, 1), :])
        return jnp.where(causal, jnp.broadcast_to(row, (TK, TQ)), 0.0) > 0.0

    _attend(i, cur + 1, mask_of, qb_ref, k_s, vt_s, bias_ref, lg_s, acc_s, o_ref)


def _moba_prompt(qb, kb, vb, bias):
    n_b, n_s, _ = qb.shape
    n_blocks = n_s // MOBA_BLOCK
    qblk = lambda w: pl.BlockSpec((1, TQ, w), lambda b, i: (b, i, 0))
    kv_rows = pl.BlockSpec((1, n_s * N_KV, HEAD_DIM), lambda b, i: (b, 0, 0))
    return pl.pallas_call(
        functools.partial(_moba_kernel, n_blocks=n_blocks, n_sel=min(MOBA_TOPK, n_blocks)),
        grid=(n_b, n_s // TQ),
        in_specs=[qblk(Q_W), kv_rows, kv_rows,
                  pl.BlockSpec((3, N_HEADS, TK, TQ), lambda b, i: (0, 0, 0, 0))],
        out_specs=qblk(Q_W),
        out_shape=jax.ShapeDtypeStruct((n_b, n_s, Q_W), BF16),
        scratch_shapes=_attn_scratch(n_blocks) + [
            pltpu.VMEM((n_blocks, N_KV, HEAD_DIM), F32),
            pltpu.VMEM((N_HEADS, n_blocks, TQ), F32),
        ] + _softmax_scratch(n_blocks),
        compiler_params=_cparams(("arbitrary", "arbitrary")),
        name="moba_prompt",
    )(qb, kb, vb, bias)


SPAN = 2048
SPAN_ROWS = SPAN * N_KV


def _page_copies(pt_ref, b, first_page, n_pages, pool_ref, buf, sem, rows):
    return [pltpu.make_async_copy(pool_ref.at[pt_ref[b, first_page + p]],
                                  buf.at[pl.ds(p * rows, rows)], sem)
            for p in range(n_pages)]


def _own_kv_lane(shape):
    row = lax.broadcasted_iota(I32, shape, 0)
    lane = lax.broadcasted_iota(I32, shape, 1)
    return (lane & (N_KV - 1)) == row // GQA


def _sidx_kernel(pt_ref, qi_ref, wi_ref, kin_ref, pool_ref, mask_o, new_o,
                 xbuf, sem, s_s, *, n_spans, n_pages, page, top_k):
    b = pl.program_id(0)
    n_b = pl.num_programs(0)
    slot = lax.rem(b, 2)

    def copies(bb, sl):
        return [pltpu.make_async_copy(pool_ref.at[pt_ref[bb, p]],
                                      xbuf.at[sl, :, pl.ds(p * page, page)], sem.at[sl])
                for p in range(n_pages)]

    @pl.when(b == 0)
    def _():
        for cp in copies(b, slot):
            cp.start()

    @pl.when(b + 1 < n_b)
    def _():
        for cp in copies(b + 1, 1 - slot):
            cp.start()

    for cp in copies(b, slot):
        cp.wait()

    q = qi_ref[0]
    w = wi_ref[0]
    for j in range(n_spans):
        x = xbuf[slot, :, j * SPAN:(j + 1) * SPAN].astype(BF16)
        s = jnp.sum(w * jnp.maximum(_dot(q, x), 0.0), axis=0, keepdims=True)
        s_s[j:j + 1, :] = _sortable(s + 0.0)
    kin = kin_ref[0].astype(BF16).astype(F32)
    s_new = jnp.sum(q.astype(F32) * kin, axis=1, keepdims=True)
    s_new = jnp.sum(w * jnp.maximum(s_new, 0.0), axis=0, keepdims=True)
    key_new = _sortable(s_new + 0.0)
    sk = s_s[...]

    def count(pred_past, pred_new):
        c = jnp.sum(jnp.where(pred_past, 1.0, 0.0), axis=1, keepdims=True)
        return jnp.sum(c, axis=0, keepdims=True) + jnp.where(pred_new, 1.0, 0.0)

    kf = float(top_k)
    zero = jnp.zeros((1, 1), I32)
    t0 = jnp.where(count(sk >= zero, key_new >= zero) >= kf, zero, INT_MIN)

    def ge(cand):
        return count(sk >= cand, key_new >= cand) >= kf

    def bit_pair_body(s, t):
        step = lax.shift_left(jnp.int32(1), 29 - 2 * s)
        c1, c2, c3 = t + step, t + 2 * step, t + 3 * step
        return jnp.where(ge(c3), c3, jnp.where(ge(c2), c2, jnp.where(ge(c1), c1, t)))
    t = lax.fori_loop(0, 15, bit_pair_body, t0)
    t = jnp.where(ge(t + 1), t + 1, t)

    need = kf - count(sk > t, key_new > t)
    eq = sk == t
    n_eq_past = count(eq, False)
    mask_o[0] = jnp.where(sk >= t, 1.0, 0.0)
    new_o[0] = jnp.broadcast_to(jnp.where(key_new >= t, 1.0, 0.0), (1, LANE))

    n_eq = n_eq_past + jnp.where(key_new == t, 1.0, 0.0)

    @pl.when(jnp.max(n_eq - need) > 0.0)
    def _():
        pos = (lax.broadcasted_iota(I32, sk.shape, 0) * SPAN
               + lax.broadcasted_iota(I32, sk.shape, 1))
        pos_bits = max(1, (n_spans * SPAN - 1).bit_length())

        def pos_body(s, lo):
            cand = lo + lax.shift_left(jnp.int32(1), pos_bits - 1 - s)
            below = count(jnp.logical_and(eq, pos < cand), False)
            return jnp.where(below < need, cand, lo)
        last = lax.fori_loop(0, pos_bits, pos_body, jnp.zeros((1, 1), I32))
        sel = jnp.logical_or(sk > t, jnp.logical_and(eq, pos <= last))
        mask_o[0] = jnp.where(sel, 1.0, 0.0)
        new_sel = jnp.logical_or(key_new > t, jnp.logical_and(key_new == t, n_eq_past < need))
        new_o[0] = jnp.broadcast_to(jnp.where(new_sel, 1.0, 0.0), (1, LANE))


def _sample_index(page_table, qi, wi, ki_new, pool_t):
    n_b, n_pages = page_table.shape
    page = pool_t.shape[2]
    n_spans = n_pages * page // SPAN
    top_k = min(DSA_TOPK, (n_pages * page + 1) // 4)
    grid_spec = pltpu.PrefetchScalarGridSpec(
        num_scalar_prefetch=1,
        grid=(n_b,),
        in_specs=[pl.BlockSpec((1, N_IDX_HEADS, IDX_DIM), lambda b, pt: (b, 0, 0)),
                  pl.BlockSpec((1, N_IDX_HEADS, 1), lambda b, pt: (b, 0, 0)),
                  pl.BlockSpec((1, 1, IDX_DIM), lambda b, pt: (b, 0, 0)),
                  pl.BlockSpec(memory_space=pl.ANY)],
        out_specs=[pl.BlockSpec((1, n_spans, SPAN), lambda b, pt: (b, 0, 0)),
                   pl.BlockSpec((1, 1, LANE), lambda b, pt: (b, 0, 0))],
        scratch_shapes=[pltpu.VMEM((2, IDX_DIM, n_pages * page), F32),
                        pltpu.SemaphoreType.DMA((2,)),
                        pltpu.VMEM((n_spans, SPAN), I32)],
    )
    return pl.pallas_call(
        functools.partial(_sidx_kernel, n_spans=n_spans, n_pages=n_pages, page=page, top_k=top_k),
        grid_spec=grid_spec,
        out_shape=[jax.ShapeDtypeStruct((n_b, n_spans, SPAN), F32),
                   jax.ShapeDtypeStruct((n_b, 1, LANE), F32)],
        compiler_params=_cparams(("arbitrary",)),
        name="sample_indexer",
    )(page_table, qi, wi, ki_new, pool_t)


def _rows_per_kv(x):
    return jnp.concatenate(
        [jnp.broadcast_to(x[:, kh * HEAD_DIM:(kh + 1) * HEAD_DIM], (GQA, HEAD_DIM))
         for kh in range(N_KV)], axis=0)


def _sdsa_kernel(pt_ref, mask_ref, new_ref, q_ref, kn_ref, vn_ref, tail_ref, far_ref, b0_ref,
                 kpool_ref, vpool_ref, o_ref, kbuf, vbuf, sem, m_s, l_s, acc_s, *, n_spans, page):
    b = pl.program_id(0)
    j = pl.program_id(1)
    n_b = pl.num_programs(0)
    pps = SPAN // page
    step = b * n_spans + j
    slot = lax.rem(step, 2)

    def copies(bb, jj, sl):
        rows = page * N_KV
        return (_page_copies(pt_ref, bb, jj * pps, pps, kpool_ref, kbuf.at[sl], sem.at[0, sl], rows)
                + _page_copies(pt_ref, bb, jj * pps, pps, vpool_ref, vbuf.at[sl], sem.at[1, sl], rows))

    @pl.when(step == 0)
    def _():
        for cp in copies(b, j, slot):
            cp.start()

    @pl.when(step + 1 < n_b * n_spans)
    def _():
        nxt = step + 1
        for cp in copies(nxt // n_spans, lax.rem(nxt, n_spans), 1 - slot):
            cp.start()

    q = q_ref[0]

    @pl.when(j == 0)
    def _():
        qf = q.astype(F32)
        kn = _rows_per_kv(kn_ref[0].astype(BF16).astype(F32))
        lg = jnp.sum(qf * kn, axis=1, keepdims=True) * ATTN_SCALE + b0_ref[...]
        keep = new_ref[0][:, 0:1] > 0.0
        m_s[...] = jnp.where(keep, lg, NEG)
        l_s[...] = jnp.where(keep, jnp.ones_like(lg), 0.0)
        vn = _rows_per_kv(vn_ref[0].astype(BF16).astype(F32))
        acc_s[...] = jnp.where(keep, vn, 0.0)

    for cp in copies(b, j, slot):
        cp.wait()

    kc = kbuf[slot].astype(BF16)
    vc = vbuf[slot].astype(BF16)
    bias = jnp.where(j == n_spans - 1, tail_ref[...], far_ref[...])
    mk = jnp.logical_and(mask_ref[0, pl.ds(j, 1), :] > 0.0, _own_kv_lane((N_HEADS, SPAN_ROWS)))
    lg = jnp.where(mk, _nt(q, kc) * ATTN_SCALE + bias, NEG)
    m_old = m_s[...]
    m_new = jnp.maximum(m_old, jnp.max(lg, axis=1, keepdims=True))
    alpha = jnp.exp(m_old - m_new)
    p = jnp.where(mk, jnp.exp(lg - m_new), 0.0)
    l_s[...] = alpha * l_s[...] + jnp.sum(p, axis=1, keepdims=True)
    acc_s[...] = alpha * acc_s[...] + _dot(p.astype(BF16), vc)
    m_s[...] = m_new

    @pl.when(j == n_spans - 1)
    def _():
        o_ref[0] = (acc_s[...] * (1.0 / l_s[...])).astype(BF16)


def _sample_dsa(page_table, mask4, new_sel, qa, ka_new, va_new, tail4, far, bias0, k_pool, v_pool):
    n_b, n_pages = page_table.shape
    page = k_pool.shape[1] // N_KV
    n_spans = n_pages * page // SPAN
    per_b = lambda shape: pl.BlockSpec((1,) + shape, lambda b, j, pt: (b, 0, 0))
    const = lambda shape: pl.BlockSpec(shape, lambda b, j, pt: (0, 0))
    grid_spec = pltpu.PrefetchScalarGridSpec(
        num_scalar_prefetch=1,
        grid=(n_b, n_spans),
        in_specs=[per_b((n_spans, SPAN_ROWS)), per_b((1, LANE)), per_b((N_HEADS, HEAD_DIM)),
                  per_b((1, KV_W)), per_b((1, KV_W)),
                  const((N_HEADS, SPAN_ROWS)), const((N_HEADS, 1)), const((N_HEADS, 1)),
                  pl.BlockSpec(memory_space=pl.ANY), pl.BlockSpec(memory_space=pl.ANY)],
        out_specs=per_b((N_HEADS, HEAD_DIM)),
        scratch_shapes=[pltpu.VMEM((2, SPAN_ROWS, HEAD_DIM), F32),
                        pltpu.VMEM((2, SPAN_ROWS, HEAD_DIM), F32),
                        pltpu.SemaphoreType.DMA((2, 2)),
                        pltpu.VMEM((N_HEADS, 1), F32), pltpu.VMEM((N_HEADS, 1), F32),
                        pltpu.VMEM((N_HEADS, HEAD_DIM), F32)],
    )
    return pl.pallas_call(
        functools.partial(_sdsa_kernel, n_spans=n_spans, page=page),
        grid_spec=grid_spec,
        out_shape=jax.ShapeDtypeStruct((n_b, N_HEADS, HEAD_DIM), BF16),
        compiler_params=_cparams(("arbitrary", "arbitrary")),
        name="sample_dsa",
    )(page_table, mask4, new_sel, qa, ka_new, va_new, tail4, far, bias0, k_pool, v_pool)


def _smoba_gate_kernel(pt_ref, q_ref, kpool_ref, sel_o, kbuf, sem, kmean_s,
                       *, n_spans, page, n_sel, n_blocks):
    b = pl.program_id(0)
    j = pl.program_id(1)
    n_b = pl.num_programs(0)
    pps = SPAN // page
    bps = SPAN // MOBA_BLOCK
    blk_rows = MOBA_BLOCK * N_KV
    step = b * n_spans + j
    slot = lax.rem(step, 2)

    def copies(bb, jj, sl):
        return _page_copies(pt_ref, bb, jj * pps, pps, kpool_ref, kbuf.at[sl], sem.at[sl],
                            page * N_KV)

    @pl.when(step == 0)
    def _():
        for cp in copies(b, j, slot):
            cp.start()

    @pl.when(step + 1 < n_b * n_spans)
    def _():
        nxt = step + 1
        for cp in copies(nxt // n_spans, lax.rem(nxt, n_spans), 1 - slot):
            cp.start()

    for cp in copies(b, j, slot):
        cp.wait()

    means = []
    for n in range(bps):
        s8 = _fold8(kbuf[slot, n * blk_rows:(n + 1) * blk_rows, :], jnp.sum)
        means.append((s8[:N_KV] + s8[N_KV:]) * (1.0 / MOBA_BLOCK))
    kmean_s[pl.ds(pl.multiple_of(j * bps * N_KV, bps * N_KV), bps * N_KV), :] = (
        jnp.concatenate(means, axis=0))

    @pl.when(j == n_spans - 1)
    def _():
        km_hi, km_lo = _split_hi_lo(kmean_s[...])
        q = q_ref[0]
        gate = _nt(q, km_hi) + _nt(q, km_lo)
        gate = jnp.where(_own_kv_lane(gate.shape), gate, -jnp.inf)
        lane = lax.broadcasted_iota(I32, gate.shape, 1).astype(F32)
        out_lane = lax.broadcasted_iota(I32, (N_HEADS, LANE), 1)
        picked = jnp.zeros((N_HEADS, LANE), F32)
        for r in range(n_sel):
            best = jnp.max(gate, axis=1, keepdims=True)
            idx = jnp.min(jnp.where(gate == best, lane, float(n_blocks * N_KV)), axis=1, keepdims=True)
            picked = jnp.where(out_lane == r, idx, picked)
            gate = jnp.where(lane == idx, -jnp.inf, gate)
        sel_o[0] = picked.astype(I32) // N_KV


def _sample_moba_gate(page_table, qb, k_pool):
    n_b, n_pages = page_table.shape
    page = k_pool.shape[1] // N_KV
    n_spans = n_pages * page // SPAN
    n_blocks = (n_pages * page + 1) // MOBA_BLOCK
    grid_spec = pltpu.PrefetchScalarGridSpec(
        num_scalar_prefetch=1,
        grid=(n_b, n_spans),
        in_specs=[pl.BlockSpec((1, N_HEADS, HEAD_DIM), lambda b, j, pt: (b, 0, 0)),
                  pl.BlockSpec(memory_space=pl.ANY)],
        out_specs=pl.BlockSpec((1, N_HEADS, LANE), lambda b, j, pt: (b, 0, 0)),
        scratch_shapes=[pltpu.VMEM((2, SPAN_ROWS, HEAD_DIM), F32), pltpu.SemaphoreType.DMA((2,)),
                        pltpu.VMEM((n_blocks * N_KV, HEAD_DIM), F32)],
    )
    return pl.pallas_call(
        functools.partial(_smoba_gate_kernel, n_spans=n_spans, page=page,
                          n_sel=min(MOBA_TOPK, n_blocks), n_blocks=n_blocks),
        grid_spec=grid_spec,
        out_shape=jax.ShapeDtypeStruct((n_b, N_HEADS, LANE), I32),
        compiler_params=_cparams(("arbitrary", "arbitrary")),
        name="sample_moba_gate",
    )(page_table, qb, k_pool)


def _smoba_attn_kernel(pt_ref, sel_ref, selv_ref, q_ref, kn_ref, vn_ref, tail_ref, far_ref, b0_ref,
                       kpool_ref, vpool_ref, o_ref, kbuf, vbuf, sem, *, n_sel, page, n_blocks):
    b = pl.program_id(0)
    n_b = pl.num_programs(0)
    ppb = MOBA_BLOCK // page
    slot = lax.rem(b, 2)
    n_keys = n_sel * MOBA_BLOCK

    def copies(bb, sl):
        out = []
        for h in range(N_HEADS):
            kh = h // GQA
            for r in range(n_sel):
                blk = sel_ref[(bb * N_HEADS + h) * n_sel + r]
                for p in range(ppb):
                    pg = pt_ref[bb, blk * ppb + p]
                    rows = pl.ds((r * ppb + p) * page, page)
                    out.append(pltpu.make_async_copy(kpool_ref.at[pg, :, kh, :],
                                                     kbuf.at[sl, h, rows], sem.at[0, sl]))
                    out.append(pltpu.make_async_copy(vpool_ref.at[pg, :, kh, :],
                                                     vbuf.at[sl, h, rows], sem.at[1, sl]))
        return out

    @pl.when(b == 0)
    def _():
        for cp in copies(b, slot):
            cp.start()

    @pl.when(b + 1 < n_b)
    def _():
        for cp in copies(b + 1, 1 - slot):
            cp.start()

    for cp in copies(b, slot):
        cp.wait()

    q = q_ref[0]
    row = lax.broadcasted_iota(I32, (N_HEADS, n_keys), 0)
    lg = jnp.zeros((N_HEADS, n_keys), F32)
    for h in range(N_HEADS):
        lg = jnp.where(row == h, _nt(q, kbuf[slot, h].astype(BF16)), lg)
    selv = selv_ref[0]
    tail = tail_ref[...]
    far = far_ref[...]
    bias = jnp.concatenate(
        [jnp.where(selv[:, r:r + 1] == n_blocks - 1, tail, far) for r in range(n_sel)], axis=1)
    lg = lg * ATTN_SCALE + bias
    qf = q.astype(F32)
    kn = _rows_per_kv(kn_ref[0].astype(BF16).astype(F32))
    lg_new = jnp.sum(qf * kn, axis=1, keepdims=True) * ATTN_SCALE + b0_ref[...]
    m = jnp.maximum(jnp.max(lg, axis=1, keepdims=True), lg_new)
    p = jnp.exp(lg - m)
    p_new = jnp.exp(lg_new - m)
    denom = jnp.sum(p, axis=1, keepdims=True) + p_new
    pb = p.astype(BF16)
    acc = p_new.astype(BF16).astype(F32) * _rows_per_kv(vn_ref[0].astype(BF16).astype(F32))
    row_o = lax.broadcasted_iota(I32, (N_HEADS, HEAD_DIM), 0)
    for h in range(N_HEADS):
        acc = acc + jnp.where(row_o == h, _dot(pb, vbuf[slot, h].astype(BF16)), 0.0)
    o_ref[0] = (acc * (1.0 / denom)).astype(BF16)


def _sample_moba_attn(page_table, sel, qb, kb_new, vb_new, tail, far, bias0, k_pool, v_pool):
    n_b, n_pages = page_table.shape
    page = k_pool.shape[1]
    n_blocks = (n_pages * page + 1) // MOBA_BLOCK
    n_sel = min(MOBA_TOPK, n_blocks)
    sel_flat = sel[:, :, :n_sel].reshape(-1)
    per_b = lambda shape: pl.BlockSpec((1,) + shape, lambda b, pt, sf: (b, 0, 0))
    const = lambda shape: pl.BlockSpec(shape, lambda b, pt, sf: (0, 0))
    grid_spec = pltpu.PrefetchScalarGridSpec(
        num_scalar_prefetch=2,
        grid=(n_b,),
        in_specs=[per_b((N_HEADS, LANE)), per_b((N_HEADS, HEAD_DIM)), per_b((1, KV_W)),
                  per_b((1, KV_W)), const((N_HEADS, MOBA_BLOCK)), const((N_HEADS, 1)),
                  const((N_HEADS, 1)),
                  pl.BlockSpec(memory_space=pl.ANY), pl.BlockSpec(memory_space=pl.ANY)],
        out_specs=per_b((N_HEADS, HEAD_DIM)),
        scratch_shapes=[pltpu.VMEM((2, N_HEADS, n_sel * MOBA_BLOCK, HEAD_DIM), F32),
                        pltpu.VMEM((2, N_HEADS, n_sel * MOBA_BLOCK, HEAD_DIM), F32),
                        pltpu.SemaphoreType.DMA((2, 2))],
    )
    return pl.pallas_call(
        functools.partial(_smoba_attn_kernel, n_sel=n_sel, page=page, n_blocks=n_blocks),
        grid_spec=grid_spec,
        out_shape=jax.ShapeDtypeStruct((n_b, N_HEADS, HEAD_DIM), BF16),
        compiler_params=_cparams(("arbitrary",)),
        name="sample_moba_attn",
    )(page_table, sel_flat, sel, qb, kb_new, vb_new, tail, far, bias0, k_pool, v_pool)


def _split_w_in(w_in):
    widths = (Q_W, KV_W, KV_W, N_IDX_HEADS * IDX_DIM, IDX_DIM, N_IDX_HEADS, Q_W, KV_W, KV_W,
              w_in.shape[0], w_in.shape[0])
    pts = [0]
    for w in widths:
        pts.append(pts[-1] + w)
    qa, ka, va, qi, ki, wi, qb, kb, vb, ga, gb = [w_in[:, pts[n]:pts[n + 1]] for n in range(11)]
    pad = jnp.zeros((w_in.shape[0], LANE - IDX_DIM - N_IDX_HEADS), w_in.dtype)
    w_a = jnp.concatenate([qa, ka, va, qi, ki, wi, pad], axis=1).astype(BF16)
    w_b = jnp.concatenate([qb, kb, vb], axis=1).astype(BF16)
    return w_a, w_b, ga.astype(BF16), gb.astype(BF16)


def kernel(x_prompt, x_sample, cache_ka, cache_va, cache_kidx, cache_kb, cache_vb, page_table,
           c_prompt, c_sample, rel_bias, w_ada, b_ada, g_norm1, w_in, g_qa, g_ka, g_kidx, g_qb,
           g_kb, w_pa, w_pb, w_out, g_norm2, w_ffn_in, w_ffn_out):
    depth = w_in.shape[0]
    n_b, n_s, d = x_prompt.shape
    n_db, n_ds, _ = x_sample.shape
    n_pool, page = cache_ka.shape[1], cache_ka.shape[2]
    past_len = page_table.shape[1] * page
    assert n_ds == 1, "the decode kernels handle one new token per sequence"
    assert n_s % MOBA_BLOCK == 0 and past_len % SPAN == 0 and SPAN % page == 0
    assert MOBA_BLOCK % page == 0 and past_len >= 4 * DSA_TOPK

    bias_mat, bias_tail = _bias_tables(rel_bias, SPAN)
    mat_a, mat_b = bias_mat[:, :N_HEADS], bias_mat[:, N_HEADS:]
    tail = bias_tail[:, 0, :]
    tail_a4 = jnp.repeat(tail[:N_HEADS], N_KV, axis=1)
    tail_b = tail[N_HEADS:, SPAN - MOBA_BLOCK:]
    far = rel_bias[N_BUCKETS - 1].reshape(-1, 1)
    bias0 = rel_bias[0].reshape(-1, 1)

    xp = x_prompt
    xs = x_sample.reshape(1, n_db, d)
    c_all = jnp.concatenate([c_prompt, c_sample], axis=0)
    rows_p, rows_s = [], []
    for l in range(depth):
        mod = _adaln(c_all, w_ada[l], b_ada[l])
        mod_p = [m.reshape(n_b, 1, d) for m in jnp.split(mod[:n_b], 6, axis=-1)]
        mod_s = [m.reshape(1, n_db, d) for m in jnp.split(mod[n_b:], 6, axis=-1)]
        w_a, w_b, w_ga, w_gb = _split_w_in(w_in[l])
        w_pa_l, w_pb_l, w_out_l = w_pa[l].astype(BF16), w_pb[l].astype(BF16), w_out[l].astype(BF16)
        w_fi, w_fo = w_ffn_in[l].astype(BF16), w_ffn_out[l].astype(BF16)
        g_kidx_pad = jnp.pad(g_kidx[l], (0, LANE - IDX_DIM)).reshape(1, LANE)

        def project(x, mods):
            h, *dsa_side = _proj_a(x, g_norm1[l], mods[0], mods[1], w_a, g_qa[l], g_ka[l], g_kidx_pad)
            return h, dsa_side, _proj_b(h, w_b, g_qb[l], g_kb[l])

        def finish(x, h, oa, ob, mods):
            m = _gate_merge(h, oa, ob, w_ga, w_gb, w_pa_l, w_pb_l)
            x2, h2 = _out_proj(m, x, w_out_l, mods[2], g_norm2[l], mods[3], mods[4])
            return _ffn(h2, x2, w_fi, w_fo, mods[5])

        h, (qa, ka, va, qi, ki, misc, kilo, kihi), (qb, kb, vb) = project(xp, mod_p)
        oa = _dsa_prompt(qi, misc, kilo, kihi, qa, ka, va, mat_a)
        ob = _moba_prompt(qb, kb, vb, mat_b)
        xp = finish(xp, h, oa, ob, mod_p)
        rows_p.append((ka.reshape(n_b, n_s, N_KV, HEAD_DIM), va.reshape(n_b, n_s, N_KV, HEAD_DIM),
                       ki, kb.reshape(n_b, n_s, N_KV, HEAD_DIM), vb.reshape(n_b, n_s, N_KV, HEAD_DIM)))

        h, (qa, ka, va, qi, ki, misc, kilo, kihi), (qb, kb, vb) = project(xs, mod_s)
        tok = lambda t: t.reshape(n_db, 1, -1)
        heads = lambda t: t.reshape(n_db, N_HEADS, HEAD_DIM)
        rows_view = lambda c: c[l].reshape(n_pool, page * N_KV, HEAD_DIM)
        wi = misc[0, :, _WI_LANE:_WI_LANE + N_IDX_HEADS].reshape(n_db, N_IDX_HEADS, 1)
        mask, new_sel = _sample_index(page_table, qi.reshape(n_db, N_IDX_HEADS, IDX_DIM), wi,
                                      tok(ki[0]), jnp.swapaxes(cache_kidx[l], 1, 2))
        oa = _sample_dsa(page_table, jnp.repeat(mask, N_KV, axis=2), new_sel, heads(qa[0]),
                         tok(ka[0]), tok(va[0]), tail_a4, far[:N_HEADS], bias0[:N_HEADS],
                         rows_view(cache_ka), rows_view(cache_va))
        sel = _sample_moba_gate(page_table, heads(qb[0]), rows_view(cache_kb))
        ob = _sample_moba_attn(page_table, sel, heads(qb[0]), tok(kb[0]), tok(vb[0]),
                               tail_b, far[N_HEADS:], bias0[N_HEADS:], cache_kb[l], cache_vb[l])
        xs = finish(xs, h, oa.reshape(1, n_db, Q_W), ob.reshape(1, n_db, Q_W), mod_s)
        kv4 = lambda t: t.reshape(n_db, 1, N_KV, HEAD_DIM)
        rows_s.append((kv4(ka[0]), kv4(va[0]), ki.reshape(n_db, 1, IDX_DIM), kv4(kb[0]), kv4(vb[0])))

    outs_p = [jnp.stack(t) for t in zip(*rows_p)]
    outs_s = [jnp.stack(t) for t in zip(*rows_s)]
    return (xp, xs.reshape(n_db, n_ds, d), *outs_p, *outs_s)
```

```python
import functools
import math

import jax
import jax.numpy as jnp
from jax import lax
from jax.experimental import pallas as pl
from jax.experimental.pallas import tpu as pltpu

F32 = jnp.float32
BF16 = jnp.bfloat16
I32 = jnp.int32

HEAD_DIM = 128
N_HEADS = 8
N_KV = 4
GQA = N_HEADS // N_KV
N_IDX_HEADS = 16
IDX_DIM = 64
IDX_WEIGHT_SCALE = (N_IDX_HEADS * IDX_DIM) ** -0.5
DSA_TOPK = 256
MOBA_BLOCK = 256
MOBA_TOPK = 3
N_BUCKETS = 32
MAX_DISTANCE = 128
EPS = 1e-6
NEG = -1e30
ATTN_SCALE = HEAD_DIM ** -0.5

SUBLANE = 8
LANE = 128
INT_MIN = -(2 ** 31)
VMEM_LIMIT_V7X = 56 * 1024 * 1024

TQ = 128
TK = 128
TK2 = MOBA_BLOCK
CPP = TK2 // TK
KV_W = N_KV * HEAD_DIM
Q_W = N_HEADS * HEAD_DIM


def _cparams(sem):
    return pltpu.CompilerParams(dimension_semantics=sem, vmem_limit_bytes=VMEM_LIMIT_V7X)


def _nt(a, b):
    return lax.dot_general(a, b, (((1,), (1,)), ((), ())), preferred_element_type=F32)


def _dot(a, b):
    return jnp.dot(a, b, preferred_element_type=F32)


def _bucket_lower_bounds():
    max_exact = N_BUCKETS // 2

    def bucket(n):
        if n < max_exact:
            return n
        large = max_exact + int(math.log(n / max_exact) / math.log(MAX_DISTANCE / max_exact)
                                * (N_BUCKETS - max_exact))
        return min(large, N_BUCKETS - 1)

    lows = [None] * N_BUCKETS
    for n in range(0, 4 * MAX_DISTANCE):
        b = bucket(n)
        if lows[b] is None:
            lows[b] = n
    return lows


_BUCKET_LOW = _bucket_lower_bounds()


def _bias_of_distance(dist, table_ref, h):
    out = jnp.full(dist.shape, table_ref[0, h], F32)
    for b in range(1, N_BUCKETS):
        if _BUCKET_LOW[b] is None:
            continue
        out = jnp.where(dist >= _BUCKET_LOW[b], table_ref[b, h], out)
    return out


def _bias_kernel(table_ref, mat_ref, tail_ref, *, tail_w):
    h = pl.program_id(0)
    kl = lax.broadcasted_iota(I32, (TK, TQ), 0)
    ql = lax.broadcasted_iota(I32, (TK, TQ), 1)
    for dd in range(3):
        mat_ref[dd, 0] = _bias_of_distance(ql - kl + dd * TQ, table_ref, h)
    lane = lax.broadcasted_iota(I32, (1, tail_w), 1)
    tail_ref[0] = _bias_of_distance(tail_w - lane, table_ref, h)


def _bias_tables(rel_bias, tail_w):
    n_h = rel_bias.shape[1]
    return pl.pallas_call(
        functools.partial(_bias_kernel, tail_w=tail_w),
        grid=(n_h,),
        in_specs=[pl.BlockSpec(memory_space=pltpu.SMEM)],
        out_specs=[pl.BlockSpec((3, 1, TK, TQ), lambda h: (0, h, 0, 0)),
                   pl.BlockSpec((1, 1, tail_w), lambda h: (h, 0, 0))],
        out_shape=[jax.ShapeDtypeStruct((3, n_h, TK, TQ), F32),
                   jax.ShapeDtypeStruct((n_h, 1, tail_w), F32)],
        compiler_params=_cparams(("arbitrary",)),
        name="bias_tables",
    )(rel_bias)


def _ada_kernel(c_ref, w_ref, b_ref, o_ref):
    c = c_ref[...]
    s = (c / (1.0 + jnp.exp(-c))).astype(BF16)
    o_ref[...] = _dot(s, w_ref[...].astype(BF16)) + b_ref[...]


def _adaln(c, w, b):
    n, d = c.shape
    tn = min(1024, d)
    return pl.pallas_call(
        _ada_kernel,
        grid=(w.shape[1] // tn,),
        in_specs=[pl.BlockSpec((n, d), lambda j: (0, 0)),
                  pl.BlockSpec((d, tn), lambda j: (0, j)),
                  pl.BlockSpec((1, tn), lambda j: (0, j))],
        out_specs=pl.BlockSpec((n, tn), lambda j: (0, j)),
        out_shape=jax.ShapeDtypeStruct((n, w.shape[1]), F32),
        compiler_params=_cparams(("arbitrary",)),
        name="adaln",
    )(c, w, b.reshape(1, -1))


def _row_tile(r):
    return min(r, 512)


def _tok_spec(tr, w):
    return pl.BlockSpec((1, tr, w), lambda g, r: (g, r, 0))


def _mod_spec(mod, tr):
    if mod.shape[1] == 1:
        return pl.BlockSpec((1, 1, mod.shape[2]), lambda g, r: (g, 0, 0))
    return pl.BlockSpec((1, tr, mod.shape[2]), lambda g, r: (g, r, 0))


def _const_spec(shape):
    nd = len(shape)
    return pl.BlockSpec(shape, lambda g, r: (0,) * nd, pipeline_mode=pl.Buffered(1))


def _rms(x, g):
    return x * lax.rsqrt(jnp.mean(x * x, axis=-1, keepdims=True) + EPS) * g


def _head_norm_store(z, g, o_ref, n, dtype):
    for hh in range(n):
        cols = slice(hh * HEAD_DIM, (hh + 1) * HEAD_DIM)
        o_ref[0, :, cols] = _rms(z[:, cols], g).astype(dtype)


def _kv_rows_store(z, o_ref, g=None):
    for kh in range(N_KV):
        v = z[:, kh * HEAD_DIM:(kh + 1) * HEAD_DIM]
        o_ref[0, pl.ds(kh, z.shape[0], stride=N_KV), :] = v if g is None else _rms(v, g)


def _kv_rows_spec(tr):
    return pl.BlockSpec((1, tr * N_KV, HEAD_DIM), lambda g, r: (g, r, 0))


_A_QA = (0, Q_W)
_A_KA = (_A_QA[1], _A_QA[1] + KV_W)
_A_VA = (_A_KA[1], _A_KA[1] + KV_W)
_A_QI = (_A_VA[1], _A_VA[1] + N_IDX_HEADS * IDX_DIM)
_A_MISC = (_A_QI[1], _A_QI[1] + LANE)
_WI_LANE = IDX_DIM


def _proj_a_kernel(x_ref, g1_ref, sh_ref, sc_ref, w_ref, gq_ref, gk_ref, gki_ref,
                   h_o, qa_o, ka_o, va_o, qi_o, ki_o, misc_o, kilo_o, kihi_o):
    h = (_rms(x_ref[0], g1_ref[...]) * (1.0 + sc_ref[0]) + sh_ref[0]).astype(BF16)
    h_o[0] = h
    _head_norm_store(_dot(h, w_ref[:, _A_QA[0]:_A_QA[1]]), gq_ref[...], qa_o, N_HEADS, BF16)
    _kv_rows_store(_dot(h, w_ref[:, _A_KA[0]:_A_KA[1]]), ka_o, gk_ref[...])
    _kv_rows_store(_dot(h, w_ref[:, _A_VA[0]:_A_VA[1]]), va_o)
    qi_o[0] = _dot(h, w_ref[:, _A_QI[0]:_A_QI[1]]).astype(BF16)
    z = _dot(h, w_ref[:, _A_MISC[0]:_A_MISC[1]])
    is_ki = lax.broadcasted_iota(I32, z.shape, 1) < IDX_DIM
    ssq = jnp.sum(jnp.where(is_ki, z * z, 0.0), axis=-1, keepdims=True) * (1.0 / IDX_DIM)
    kin = jnp.where(is_ki, z * lax.rsqrt(ssq + EPS) * gki_ref[...], 0.0)
    ki_o[0] = kin[:, :IDX_DIM]
    misc_o[0] = jnp.where(is_ki, kin, z * IDX_WEIGHT_SCALE)
    kilo_o[0] = kin.astype(BF16)
    kihi_o[0] = pltpu.roll(kin, IDX_DIM, axis=1).astype(BF16)


def _proj_a(x, g1, sh, sc, w_a, g_qa, g_ka, g_kidx_pad):
    n_g, n_r, d = x.shape
    tr = _row_tile(n_r)
    widths = [(d, BF16), (Q_W, BF16), None, None, (N_IDX_HEADS * IDX_DIM, BF16),
              (IDX_DIM, F32), (LANE, F32), (LANE, BF16), (LANE, BF16)]
    kv_rows = jax.ShapeDtypeStruct((n_g, n_r * N_KV, HEAD_DIM), F32)
    return pl.pallas_call(
        _proj_a_kernel,
        grid=(n_g, n_r // tr),
        in_specs=[_tok_spec(tr, d), _const_spec((1, d)), _mod_spec(sh, tr), _mod_spec(sc, tr),
                  _const_spec(w_a.shape), _const_spec((1, HEAD_DIM)),
                  _const_spec((1, HEAD_DIM)), _const_spec((1, LANE))],
        out_specs=[_kv_rows_spec(tr) if w is None else _tok_spec(tr, w[0]) for w in widths],
        out_shape=[kv_rows if w is None else jax.ShapeDtypeStruct((n_g, n_r, w[0]), w[1])
                   for w in widths],
        compiler_params=_cparams(("arbitrary", "arbitrary")),
        name="proj_dsa",
    )(x, g1.reshape(1, d), sh, sc, w_a, g_qa.reshape(1, -1), g_ka.reshape(1, -1), g_kidx_pad)


def _proj_b_kernel(h_ref, w_ref, gq_ref, gk_ref, qb_o, kb_o, vb_o):
    h = h_ref[0]
    _head_norm_store(_dot(h, w_ref[:, 0:Q_W]), gq_ref[...], qb_o, N_HEADS, BF16)
    _kv_rows_store(_dot(h, w_ref[:, Q_W:Q_W + KV_W]), kb_o, gk_ref[...])
    _kv_rows_store(_dot(h, w_ref[:, Q_W + KV_W:Q_W + 2 * KV_W]), vb_o)


def _proj_b(h, w_b, g_qb, g_kb):
    n_g, n_r, d = h.shape
    tr = _row_tile(n_r)
    kv_rows = jax.ShapeDtypeStruct((n_g, n_r * N_KV, HEAD_DIM), F32)
    return pl.pallas_call(
        _proj_b_kernel,
        grid=(n_g, n_r // tr),
        in_specs=[_tok_spec(tr, d), _const_spec(w_b.shape), _const_spec((1, HEAD_DIM)),
                  _const_spec((1, HEAD_DIM))],
        out_specs=[_tok_spec(tr, Q_W), _kv_rows_spec(tr), _kv_rows_spec(tr)],
        out_shape=[jax.ShapeDtypeStruct((n_g, n_r, Q_W), BF16), kv_rows, kv_rows],
        compiler_params=_cparams(("arbitrary", "arbitrary")),
        name="proj_moba",
    )(h, w_b, g_qb.reshape(1, -1), g_kb.reshape(1, -1))


def _sigmoid(x):
    return 1.0 / (1.0 + jnp.exp(-x))


def _gate_merge_kernel(h_ref, oa_ref, ob_ref, wga_ref, wgb_ref, wpa_ref, wpb_ref, m_o):
    h = h_ref[0]
    m = (_sigmoid(_dot(h, wga_ref[...])) * _dot(oa_ref[0], wpa_ref[...])
         + _sigmoid(_dot(h, wgb_ref[...])) * _dot(ob_ref[0], wpb_ref[...]))
    m_o[0] = m.astype(BF16)


def _gate_merge(h, oa, ob, w_ga, w_gb, w_pa, w_pb):
    n_g, n_r, d = h.shape
    tr = _row_tile(n_r)
    tn = d // 2
    tok = lambda w: pl.BlockSpec((1, tr, w), lambda c, g, r: (g, r, 0))
    col = lambda k: pl.BlockSpec((k, tn), lambda c, g, r: (0, c))
    return pl.pallas_call(
        _gate_merge_kernel,
        grid=(d // tn, n_g, n_r // tr),
        in_specs=[tok(d), tok(Q_W), tok(Q_W), col(d), col(d), col(Q_W), col(Q_W)],
        out_specs=pl.BlockSpec((1, tr, tn), lambda c, g, r: (g, r, c)),
        out_shape=jax.ShapeDtypeStruct((n_g, n_r, d), BF16),
        compiler_params=_cparams(("arbitrary", "arbitrary", "arbitrary")),
        name="gate_merge",
    )(h, oa, ob, w_ga, w_gb, w_pa, w_pb)


def _out_proj_kernel(m_ref, x_ref, w_ref, gt_ref, g2_ref, sh_ref, sc_ref, x2_o, h2_o):
    x2 = x_ref[0] + gt_ref[0] * _dot(m_ref[0], w_ref[...])
    x2_o[0] = x2
    h2_o[0] = (_rms(x2, g2_ref[...]) * (1.0 + sc_ref[0]) + sh_ref[0]).astype(BF16)


def _out_proj(m, x, w_out, gt, g2, sh, sc):
    n_g, n_r, d = x.shape
    tr = _row_tile(n_r)
    return pl.pallas_call(
        _out_proj_kernel,
        grid=(n_g, n_r // tr),
        in_specs=[_tok_spec(tr, d), _tok_spec(tr, d), _const_spec(w_out.shape), _mod_spec(gt, tr),
                  _const_spec((1, d)), _mod_spec(sh, tr), _mod_spec(sc, tr)],
        out_specs=[_tok_spec(tr, d), _tok_spec(tr, d)],
        out_shape=[jax.ShapeDtypeStruct((n_g, n_r, d), F32),
                   jax.ShapeDtypeStruct((n_g, n_r, d), BF16)],
        compiler_params=_cparams(("arbitrary", "arbitrary")),
        name="out_proj",
    )(m, x, w_out, gt, g2.reshape(1, d), sh, sc)


def _ffn_kernel(h_ref, x_ref, wa_ref, wu_ref, wo_ref, gt_ref, y_o, acc_s):
    j = pl.program_id(2)

    @pl.when(j == 0)
    def _():
        acc_s[...] = jnp.zeros_like(acc_s)

    h = h_ref[0]
    a = _dot(h, wa_ref[...])
    u = _dot(h, wu_ref[...])
    act = (a * _sigmoid(a) * u).astype(BF16)
    acc_s[...] += _dot(act, wo_ref[...])

    @pl.when(j == pl.num_programs(2) - 1)
    def _():
        y_o[0] = x_ref[0] + gt_ref[0] * acc_s[...]


def _ffn(h2, x2, w_in, w_out, gt):
    n_g, n_r, d = x2.shape
    d_ff = w_out.shape[0]
    tr = _row_tile(n_r)
    tf = 512 if d_ff % 512 == 0 else 256
    n_f = d_ff // tf
    tok = lambda w: pl.BlockSpec((1, tr, w), lambda g, r, j: (g, r, 0))
    if gt.shape[1] == 1:
        gt_spec = pl.BlockSpec((1, 1, d), lambda g, r, j: (g, 0, 0))
    else:
        gt_spec = pl.BlockSpec((1, tr, d), lambda g, r, j: (g, r, 0))
    return pl.pallas_call(
        _ffn_kernel,
        grid=(n_g, n_r // tr, n_f),
        in_specs=[tok(d), tok(d),
                  pl.BlockSpec((d, tf), lambda g, r, j: (0, j)),
                  pl.BlockSpec((d, tf), lambda g, r, j: (0, j + n_f)),
                  pl.BlockSpec((tf, d), lambda g, r, j: (j, 0)),
                  gt_spec],
        out_specs=tok(d),
        out_shape=jax.ShapeDtypeStruct((n_g, n_r, d), F32),
        scratch_shapes=[pltpu.VMEM((tr, d), F32)],
        compiler_params=_cparams(("arbitrary", "arbitrary", "arbitrary")),
        name="swiglu_ffn",
    )(h2, x2, w_in, w_in, w_out, gt)


def _sortable(x):
    bits = pltpu.bitcast(x, I32)
    return bits ^ (lax.shift_right_arithmetic(bits, 31) & 0x7FFFFFFF)


def _stage_kv(k_ref, v_ref, k_s, vt_s, n_blocks):
    def body(c, carry):
        rows = pl.ds(pl.multiple_of(c * TK2, TK2), TK2)
        for kh in range(N_KV):
            cols = slice(kh * HEAD_DIM, (kh + 1) * HEAD_DIM)
            src = pl.ds(c * (TK2 * N_KV) + kh, TK2, stride=N_KV)
            k_s[rows, cols] = k_ref[0, src, :].astype(BF16)
            vt_s[c, cols, :] = v_ref[0, src, :].T.astype(BF16)
        return carry
    lax.fori_loop(0, n_blocks, body, 0)


def _fold8(x, op):
    r, w = x.shape
    return op(x.reshape(r // SUBLANE, SUBLANE, w), axis=0)


def _loop_pairs(n, body, init):
    carry = lax.fori_loop(0, n // 2, lambda j, c: body(2 * j + 1, body(2 * j, c)), init)
    return lax.cond(n % 2 == 1, lambda c: body(n - 1, c), lambda c: c, carry)


def _loop_quads(n, body, init):
    def quad(j, c):
        for r in range(4):
            c = body(4 * j + r, c)
        return c
    carry = lax.fori_loop(0, n // 4, quad, init)
    base = (n // 4) * 4
    carry = lax.cond(n % 4 >= 2, lambda c: body(base + 1, body(base, c)), lambda c: c, carry)
    return lax.cond(n % 2 == 1, lambda c: body(n - 1, c), lambda c: c, carry)


def _attend(i, n_past, mask_of, own_mask_of, q_ref, k_s, vt_s, bias_ref, lg_s, acc_s, o_ref):
    n_pc = n_past if own_mask_of is None else n_past + 1

    def logits_body(c2, mx, mask_of=mask_of):
        rows = pl.ds(pl.multiple_of(c2 * TK2, TK2), TK2)
        out = []
        for kh in range(N_KV):
            q2 = jnp.concatenate(
                [q_ref[0, :, (kh * GQA + g) * HEAD_DIM:(kh * GQA + g + 1) * HEAD_DIM]
                 for g in range(GQA)], axis=0)
            lg = _nt(k_s[rows, kh * HEAD_DIM:(kh + 1) * HEAD_DIM], q2) * ATTN_SCALE
            best = mx[kh]
            for a in range(CPP):
                dd = jnp.clip(i - (c2 * CPP + a), 0, 2)
                folded = []
                for g in range(GQA):
                    h = kh * GQA + g
                    t = lg[a * TK:(a + 1) * TK, g * TQ:(g + 1) * TQ] + bias_ref[dd, h]
                    t = jnp.where(mask_of(c2, a, h), t, NEG)
                    lg_s[kh, c2, a * TK:(a + 1) * TK, g * TQ:(g + 1) * TQ] = t
                    folded.append(_fold8(t, jnp.max))
                best = jnp.maximum(best, jnp.concatenate(folded, axis=1))
            out.append(best)
        return tuple(out)

    mx0 = tuple(jnp.full((SUBLANE, GQA * TQ), NEG, F32) for _ in range(N_KV))
    mx = _loop_quads(n_past, logits_body, mx0)
    if own_mask_of is not None:
        mx = logits_body(n_past, mx, own_mask_of)
    m = [jnp.max(x, axis=0, keepdims=True) for x in mx]

    acc_s[...] = jnp.zeros(acc_s.shape, F32)

    def pv_body(c2, ls):
        out = []
        for kh in range(N_KV):
            p = jnp.exp(lg_s[kh, c2] - m[kh])
            out.append(ls[kh] + _fold8(p, jnp.sum))
            acc_s[kh] += _dot(vt_s[c2, kh * HEAD_DIM:(kh + 1) * HEAD_DIM, :], p.astype(BF16))
        return tuple(out)

    ls0 = tuple(jnp.zeros((SUBLANE, GQA * TQ), F32) for _ in range(N_KV))
    ls = _loop_quads(n_pc, pv_body, ls0)
    for kh in range(N_KV):
        o2 = acc_s[kh] * (1.0 / jnp.sum(ls[kh], axis=0, keepdims=True))
        for g in range(GQA):
            h = kh * GQA + g
            o_ref[0, :, h * HEAD_DIM:(h + 1) * HEAD_DIM] = o2[:, g * TQ:(g + 1) * TQ].T.astype(BF16)


def _dsa_kernel(qi_ref, misc_ref, kilo_ref, kihi_ref, qa_ref, ka_ref, va_ref, bias_ref, o_ref,
                k_s, vt_s, sk_s, w_s, lg_s, acc_s, *, n_blocks, top_k, pos_bits):
    i = pl.program_id(1)
    n_pc = i // CPP + 1
    kl = lax.broadcasted_iota(I32, (TK, TQ), 0)
    ql = lax.broadcasted_iota(I32, (TK, TQ), 1)
    qpos = i * TQ + lax.broadcasted_iota(I32, (1, TQ), 1)

    @pl.when(i == 0)
    def _():
        _stage_kv(ka_ref, va_ref, k_s, vt_s, n_blocks)

    w_s[...] = misc_ref[0].T

    def score_body(c2, carry):
        rows = pl.ds(pl.multiple_of(c2 * TK2, TK2), TK2)
        k2 = jnp.concatenate([kilo_ref[0, rows, :], kihi_ref[0, rows, :]], axis=0)
        acc = [jnp.zeros((TK, TQ), F32) for _ in range(CPP)]
        for p2 in range(N_IDX_HEADS // 4):
            qp = jnp.concatenate([qi_ref[0, :, (2 * p2 + pair) * LANE:(2 * p2 + pair + 1) * LANE]
                                  for pair in range(2)], axis=0)
            s = jnp.maximum(_nt(k2, qp), 0.0)
            for half in range(2):
                for pair in range(2):
                    r = _WI_LANE + 4 * p2 + 2 * pair + half
                    w = w_s[r:r + 1, :]
                    for a in range(CPP):
                        r0 = half * TK2 + a * TK
                        acc[a] = acc[a] + w * s[r0:r0 + TK, pair * TQ:(pair + 1) * TQ]
        for a in range(CPP):
            c = c2 * CPP + a
            admissible = (c * TK + kl) <= (i * TQ + ql)
            sk_s[c] = jnp.where(admissible, _sortable(acc[a]), INT_MIN)
        return carry
    _loop_pairs(n_pc, score_body, 0)

    def count(pred):
        def body(c2, acc):
            for a in range(CPP):
                c = c2 * CPP + a
                acc = acc + _fold8(jnp.where(pred(c, sk_s[c]), 1.0, 0.0), jnp.sum)
            return acc
        acc = _loop_pairs(n_pc, body, jnp.zeros((SUBLANE, TQ), F32))
        return jnp.sum(acc, axis=0, keepdims=True)

    kf = float(top_k)
    few = qpos + 1 <= top_k

    def threshold():
        zero = jnp.zeros((1, TQ), I32)
        t0 = jnp.where(count(lambda c, s: s >= zero) >= kf, zero, INT_MIN)

        def bit_body(s, t):
            cand = t + lax.shift_left(jnp.int32(1), 30 - s)
            return jnp.where(count(lambda c, sk: sk >= cand) >= kf, cand, t)
        return lax.fori_loop(0, 31, bit_body, t0)

    t = lax.cond((i + 1) * TQ > top_k, threshold, lambda: jnp.full((1, TQ), INT_MIN + 1, I32))
    t = jnp.where(few, INT_MIN + 1, t)

    @pl.when((i + 1) * TQ > top_k)
    def _():
        need = kf - count(lambda c, sk: sk > t)
        n_eq = count(lambda c, sk: sk == t)
        tied = jnp.max(jnp.where(jnp.logical_and(n_eq > need, jnp.logical_not(few)), 1.0, 0.0))

        @pl.when(tied > 0.0)
        def _():
            def pos_body(s, lo):
                cand = lo + lax.shift_left(jnp.int32(1), pos_bits - 1 - s)
                below = count(lambda c, sk: jnp.logical_and(sk == t, c * TK + kl < cand))
                return jnp.where(below < need, cand, lo)
            last = lax.fori_loop(0, pos_bits, pos_body, jnp.zeros((1, TQ), I32))

            def drop_body(c, carry):
                sk = sk_s[c]
                drop = jnp.logical_and(sk == t, c * TK + kl > last)
                sk_s[c] = jnp.where(drop, INT_MIN, sk)
                return carry
            lax.fori_loop(0, i + 1, drop_body, 0)

    _attend(i, n_pc, lambda c2, a, h: sk_s[c2 * CPP + a] >= t, None, qa_ref, k_s, vt_s, bias_ref,
            lg_s, acc_s, o_ref)


def _attn_scratch(n_blocks):
    return [pltpu.VMEM((n_blocks * TK2, KV_W), BF16),
            pltpu.VMEM((n_blocks, KV_W, TK2), BF16)]


def _softmax_scratch(n_blocks):
    return [pltpu.VMEM((N_KV, n_blocks, TK2, GQA * TQ), F32),
            pltpu.VMEM((N_KV, HEAD_DIM, GQA * TQ), F32)]


def _dsa_prompt(qi, misc, kilo, kihi, qa, ka, va, bias):
    n_b, n_s, _ = qa.shape
    n_blocks = n_s // TK2
    top_k = min(DSA_TOPK, n_s // 4)
    qblk = lambda w: pl.BlockSpec((1, TQ, w), lambda b, i: (b, i, 0))
    seq = lambda w: pl.BlockSpec((1, n_s, w), lambda b, i: (b, 0, 0))
    kv_rows = pl.BlockSpec((1, n_s * N_KV, HEAD_DIM), lambda b, i: (b, 0, 0))
    return pl.pallas_call(
        functools.partial(_dsa_kernel, n_blocks=n_blocks, top_k=top_k,
                          pos_bits=max(1, (n_s - 1).bit_length())),
        grid=(n_b, n_s // TQ),
        in_specs=[qblk(N_IDX_HEADS * IDX_DIM), qblk(LANE), seq(LANE), seq(LANE), qblk(Q_W),
                  kv_rows, kv_rows,
                  pl.BlockSpec((3, N_HEADS, TK, TQ), lambda b, i: (0, 0, 0, 0))],
        out_specs=qblk(Q_W),
        out_shape=jax.ShapeDtypeStruct((n_b, n_s, Q_W), BF16),
        scratch_shapes=_attn_scratch(n_blocks) + [
            pltpu.VMEM((n_s // TK, TK, TQ), I32),
            pltpu.VMEM((LANE, TQ), F32),
        ] + _softmax_scratch(n_blocks),
        compiler_params=_cparams(("arbitrary", "arbitrary")),
        name="dsa_prompt",
    )(qi, misc, kilo, kihi, qa, ka, va, bias)


def _roll_rows(x, r):
    if x.shape[0] % SUBLANE == 0:
        return pltpu.roll(x, r, axis=0)
    return jnp.roll(x, r, axis=0)


def _split_hi_lo(x):
    hi = x.astype(BF16)
    return hi, (x - hi.astype(F32)).astype(BF16)


def _moba_kernel(qb_ref, kb_ref, vb_ref, bias_ref, o_ref,
                 k_s, vt_s, kmean_s, sel_s, lg_s, acc_s, *, n_blocks, n_sel):
    i = pl.program_id(1)
    cur = (i * TQ) // MOBA_BLOCK
    kl = lax.broadcasted_iota(I32, (TK, TQ), 0)
    ql = lax.broadcasted_iota(I32, (TK, TQ), 1)

    @pl.when(i == 0)
    def _():
        _stage_kv(kb_ref, vb_ref, k_s, vt_s, n_blocks)
        for n in range(n_blocks):
            blk = kb_ref[0, n * MOBA_BLOCK * N_KV:(n + 1) * MOBA_BLOCK * N_KV, :]
            s8 = _fold8(blk, jnp.sum)
            kmean_s[n] = (s8[:N_KV] + s8[N_KV:]) * (1.0 / MOBA_BLOCK)

    blk_id = lax.broadcasted_iota(I32, (n_blocks, TQ), 0)
    for kh in range(N_KV):
        km_hi, km_lo = _split_hi_lo(kmean_s[:, kh, :])
        for g in range(GQA):
            h = kh * GQA + g
            qh = qb_ref[0, :, h * HEAD_DIM:(h + 1) * HEAD_DIM]
            gate = _nt(km_hi, qh) + _nt(km_lo, qh)
            gate = jnp.where(blk_id < cur, gate, NEG)
            rank = jnp.zeros((n_blocks, TQ), F32)
            for r in range(1, n_blocks):
                other = _roll_rows(gate, r)
                beats = jnp.logical_or(other > gate,
                                       jnp.logical_and(other == gate, blk_id >= r))
                rank = rank + jnp.where(beats, 1.0, 0.0)
            keep = jnp.logical_and(rank < n_sel, blk_id < cur)
            sel_s[h] = jnp.where(keep, 1.0, 0.0)

    def past_mask_of(c2, a, h):
        return jnp.broadcast_to(sel_s[h, pl.ds(c2, 1), :], (TK, TQ)) > 0.0

    def own_mask_of(c2, a, h):
        return (c2 * TK2 + a * TK + kl) <= (i * TQ + ql)

    _attend(i, cur, past_mask_of, own_mask_of, qb_ref, k_s, vt_s, bias_ref, lg_s, acc_s, o_ref)


def _moba_prompt(qb, kb, vb, bias):
    n_b, n_s, _ = qb.shape
    n_blocks = n_s // MOBA_BLOCK
    qblk = lambda w: pl.BlockSpec((1, TQ, w), lambda b, i: (b, i, 0))
    kv_rows = pl.BlockSpec((1, n_s * N_KV, HEAD_DIM), lambda b, i: (b, 0, 0))
    return pl.pallas_call(
        functools.partial(_moba_kernel, n_blocks=n_blocks, n_sel=min(MOBA_TOPK, n_blocks)),
        grid=(n_b, n_s // TQ),
        in_specs=[qblk(Q_W), kv_rows, kv_rows,
                  pl.BlockSpec((3, N_HEADS, TK, TQ), lambda b, i: (0, 0, 0, 0))],
        out_specs=qblk(Q_W),
        out_shape=jax.ShapeDtypeStruct((n_b, n_s, Q_W), BF16),
        scratch_shapes=_attn_scratch(n_blocks) + [
            pltpu.VMEM((n_blocks, N_KV, HEAD_DIM), F32),
            pltpu.VMEM((N_HEADS, n_blocks, TQ), F32),
        ] + _softmax_scratch(n_blocks),
        compiler_params=_cparams(("arbitrary", "arbitrary")),
        name="moba_prompt",
    )(qb, kb, vb, bias)


SPAN = 2048
SPAN_ROWS = SPAN * N_KV


def _page_copies(pt_ref, b, first_page, n_pages, pool_ref, buf, sem, rows):
    return [pltpu.make_async_copy(pool_ref.at[pt_ref[b, first_page + p]],
                                  buf.at[pl.ds(p * rows, rows)], sem)
            for p in range(n_pages)]


def _own_kv_lane(shape):
    row = lax.broadcasted_iota(I32, shape, 0)
    lane = lax.broadcasted_iota(I32, shape, 1)
    return (lane & (N_KV - 1)) == row // GQA


def _sidx_kernel(pt_ref, qi_ref, wi_ref, kin_ref, pool_ref, mask_o, new_o,
                 xbuf, sem, s_s, *, n_spans, n_pages, page, top_k):
    b = pl.program_id(0)
    n_b = pl.num_programs(0)
    slot = lax.rem(b, 2)

    def copies(bb, sl):
        return [pltpu.make_async_copy(pool_ref.at[pt_ref[bb, p]],
                                      xbuf.at[sl, :, pl.ds(p * page, page)], sem.at[sl])
                for p in range(n_pages)]

    @pl.when(b == 0)
    def _():
        for cp in copies(b, slot):
            cp.start()

    @pl.when(b + 1 < n_b)
    def _():
        for cp in copies(b + 1, 1 - slot):
            cp.start()

    for cp in copies(b, slot):
        cp.wait()

    q = qi_ref[0]
    w = wi_ref[0]
    for j in range(n_spans):
        x = xbuf[slot, :, j * SPAN:(j + 1) * SPAN].astype(BF16)
        s = jnp.sum(w * jnp.maximum(_dot(q, x), 0.0), axis=0, keepdims=True)
        s_s[j:j + 1, :] = _sortable(s + 0.0)
    kin = kin_ref[0].astype(BF16).astype(F32)
    s_new = jnp.sum(q.astype(F32) * kin, axis=1, keepdims=True)
    s_new = jnp.sum(w * jnp.maximum(s_new, 0.0), axis=0, keepdims=True)
    key_new = _sortable(s_new + 0.0)
    sk = s_s[...]

    def count(pred_past, pred_new):
        c = jnp.sum(jnp.where(pred_past, 1.0, 0.0), axis=1, keepdims=True)
        return jnp.sum(c, axis=0, keepdims=True) + jnp.where(pred_new, 1.0, 0.0)

    kf = float(top_k)
    zero = jnp.zeros((1, 1), I32)
    t0 = jnp.where(count(sk >= zero, key_new >= zero) >= kf, zero, INT_MIN)

    def ge(cand):
        return count(sk >= cand, key_new >= cand) >= kf

    def bit_pair_body(s, t):
        step = lax.shift_left(jnp.int32(1), 29 - 2 * s)
        c1, c2, c3 = t + step, t + 2 * step, t + 3 * step
        return jnp.where(ge(c3), c3, jnp.where(ge(c2), c2, jnp.where(ge(c1), c1, t)))
    t = lax.fori_loop(0, 15, bit_pair_body, t0)
    t = jnp.where(ge(t + 1), t + 1, t)

    need = kf - count(sk > t, key_new > t)
    eq = sk == t
    n_eq_past = count(eq, False)
    mask_o[0] = jnp.where(sk >= t, 1.0, 0.0)
    new_o[0] = jnp.broadcast_to(jnp.where(key_new >= t, 1.0, 0.0), (1, LANE))

    n_eq = n_eq_past + jnp.where(key_new == t, 1.0, 0.0)

    @pl.when(jnp.max(n_eq - need) > 0.0)
    def _():
        pos = (lax.broadcasted_iota(I32, sk.shape, 0) * SPAN
               + lax.broadcasted_iota(I32, sk.shape, 1))
        pos_bits = max(1, (n_spans * SPAN - 1).bit_length())

        def pos_body(s, lo):
            cand = lo + lax.shift_left(jnp.int32(1), pos_bits - 1 - s)
            below = count(jnp.logical_and(eq, pos < cand), False)
            return jnp.where(below < need, cand, lo)
        last = lax.fori_loop(0, pos_bits, pos_body, jnp.zeros((1, 1), I32))
        sel = jnp.logical_or(sk > t, jnp.logical_and(eq, pos <= last))
        mask_o[0] = jnp.where(sel, 1.0, 0.0)
        new_sel = jnp.logical_or(key_new > t, jnp.logical_and(key_new == t, n_eq_past < need))
        new_o[0] = jnp.broadcast_to(jnp.where(new_sel, 1.0, 0.0), (1, LANE))


def _sample_index(page_table, qi, wi, ki_new, pool_t):
    n_b, n_pages = page_table.shape
    page = pool_t.shape[2]
    n_spans = n_pages * page // SPAN
    top_k = min(DSA_TOPK, (n_pages * page + 1) // 4)
    grid_spec = pltpu.PrefetchScalarGridSpec(
        num_scalar_prefetch=1,
        grid=(n_b,),
        in_specs=[pl.BlockSpec((1, N_IDX_HEADS, IDX_DIM), lambda b, pt: (b, 0, 0)),
                  pl.BlockSpec((1, N_IDX_HEADS, 1), lambda b, pt: (b, 0, 0)),
                  pl.BlockSpec((1, 1, IDX_DIM), lambda b, pt: (b, 0, 0)),
                  pl.BlockSpec(memory_space=pl.ANY)],
        out_specs=[pl.BlockSpec((1, n_spans, SPAN), lambda b, pt: (b, 0, 0)),
                   pl.BlockSpec((1, 1, LANE), lambda b, pt: (b, 0, 0))],
        scratch_shapes=[pltpu.VMEM((2, IDX_DIM, n_pages * page), F32),
                        pltpu.SemaphoreType.DMA((2,)),
                        pltpu.VMEM((n_spans, SPAN), I32)],
    )
    return pl.pallas_call(
        functools.partial(_sidx_kernel, n_spans=n_spans, n_pages=n_pages, page=page, top_k=top_k),
        grid_spec=grid_spec,
        out_shape=[jax.ShapeDtypeStruct((n_b, n_spans, SPAN), F32),
                   jax.ShapeDtypeStruct((n_b, 1, LANE), F32)],
        compiler_params=_cparams(("arbitrary",)),
        name="sample_indexer",
    )(page_table, qi, wi, ki_new, pool_t)


def _rows_per_kv(x):
    return jnp.concatenate(
        [jnp.broadcast_to(x[:, kh * HEAD_DIM:(kh + 1) * HEAD_DIM], (GQA, HEAD_DIM))
         for kh in range(N_KV)], axis=0)


def _sdsa_kernel(pt_ref, mask_ref, new_ref, q_ref, kn_ref, vn_ref, tail_ref, far_ref, b0_ref,
                 kpool_ref, vpool_ref, o_ref, kbuf, vbuf, sem, m_s, l_s, acc_s, *, n_spans, page):
    b = pl.program_id(0)
    j = pl.program_id(1)
    n_b = pl.num_programs(0)
    pps = SPAN // page
    step = b * n_spans + j
    slot = lax.rem(step, 2)

    def copies(bb, jj, sl):
        rows = page * N_KV
        return (_page_copies(pt_ref, bb, jj * pps, pps, kpool_ref, kbuf.at[sl], sem.at[0, sl], rows)
                + _page_copies(pt_ref, bb, jj * pps, pps, vpool_ref, vbuf.at[sl], sem.at[1, sl], rows))

    @pl.when(step == 0)
    def _():
        for cp in copies(b, j, slot):
            cp.start()

    @pl.when(step + 1 < n_b * n_spans)
    def _():
        nxt = step + 1
        for cp in copies(nxt // n_spans, lax.rem(nxt, n_spans), 1 - slot):
            cp.start()

    q = q_ref[0]

    @pl.when(j == 0)
    def _():
        qf = q.astype(F32)
        kn = _rows_per_kv(kn_ref[0].astype(BF16).astype(F32))
        lg = jnp.sum(qf * kn, axis=1, keepdims=True) * ATTN_SCALE + b0_ref[...]
        keep = new_ref[0][:, 0:1] > 0.0
        m_s[...] = jnp.where(keep, lg, NEG)
        l_s[...] = jnp.where(keep, jnp.ones_like(lg), 0.0)
        vn = _rows_per_kv(vn_ref[0].astype(BF16).astype(F32))
        acc_s[...] = jnp.where(keep, vn, 0.0)

    for cp in copies(b, j, slot):
        cp.wait()

    kc = kbuf[slot].astype(BF16)
    vc = vbuf[slot].astype(BF16)
    bias = jnp.where(j == n_spans - 1, tail_ref[...], far_ref[...])
    mk = jnp.logical_and(mask_ref[0, pl.ds(j, 1), :] > 0.0, _own_kv_lane((N_HEADS, SPAN_ROWS)))
    lg = jnp.where(mk, _nt(q, kc) * ATTN_SCALE + bias, NEG)
    m_old = m_s[...]
    m_new = jnp.maximum(m_old, jnp.max(lg, axis=1, keepdims=True))
    alpha = jnp.exp(m_old - m_new)
    p = jnp.where(mk, jnp.exp(lg - m_new), 0.0)
    l_s[...] = alpha * l_s[...] + jnp.sum(p, axis=1, keepdims=True)
    acc_s[...] = alpha * acc_s[...] + _dot(p.astype(BF16), vc)
    m_s[...] = m_new

    @pl.when(j == n_spans - 1)
    def _():
        o_ref[0] = (acc_s[...] * (1.0 / l_s[...])).astype(BF16)


def _sample_dsa(page_table, mask4, new_sel, qa, ka_new, va_new, tail4, far, bias0, k_pool, v_pool):
    n_b, n_pages = page_table.shape
    page = k_pool.shape[1] // N_KV
    n_spans = n_pages * page // SPAN
    per_b = lambda shape: pl.BlockSpec((1,) + shape, lambda b, j, pt: (b, 0, 0))
    const = lambda shape: pl.BlockSpec(shape, lambda b, j, pt: (0, 0))
    grid_spec = pltpu.PrefetchScalarGridSpec(
        num_scalar_prefetch=1,
        grid=(n_b, n_spans),
        in_specs=[per_b((n_spans, SPAN_ROWS)), per_b((1, LANE)), per_b((N_HEADS, HEAD_DIM)),
                  per_b((1, KV_W)), per_b((1, KV_W)),
                  const((N_HEADS, SPAN_ROWS)), const((N_HEADS, 1)), const((N_HEADS, 1)),
                  pl.BlockSpec(memory_space=pl.ANY), pl.BlockSpec(memory_space=pl.ANY)],
        out_specs=per_b((N_HEADS, HEAD_DIM)),
        scratch_shapes=[pltpu.VMEM((2, SPAN_ROWS, HEAD_DIM), F32),
                        pltpu.VMEM((2, SPAN_ROWS, HEAD_DIM), F32),
                        pltpu.SemaphoreType.DMA((2, 2)),
                        pltpu.VMEM((N_HEADS, 1), F32), pltpu.VMEM((N_HEADS, 1), F32),
                        pltpu.VMEM((N_HEADS, HEAD_DIM), F32)],
    )
    return pl.pallas_call(
        functools.partial(_sdsa_kernel, n_spans=n_spans, page=page),
        grid_spec=grid_spec,
        out_shape=jax.ShapeDtypeStruct((n_b, N_HEADS, HEAD_DIM), BF16),
        compiler_params=_cparams(("arbitrary", "arbitrary")),
        name="sample_dsa",
    )(page_table, mask4, new_sel, qa, ka_new, va_new, tail4, far, bias0, k_pool, v_pool)


def _smoba_gate_kernel(pt_ref, q_ref, kpool_ref, sel_o, kbuf, sem, kmean_s,
                       *, n_spans, page, n_sel, n_blocks):
    b = pl.program_id(0)
    j = pl.program_id(1)
    n_b = pl.num_programs(0)
    pps = SPAN // page
    bps = SPAN // MOBA_BLOCK
    blk_rows = MOBA_BLOCK * N_KV
    step = b * n_spans + j
    slot = lax.rem(step, 2)

    def copies(bb, jj, sl):
        return _page_copies(pt_ref, bb, jj * pps, pps, kpool_ref, kbuf.at[sl], sem.at[sl],
                            page * N_KV)

    @pl.when(step == 0)
    def _():
        for cp in copies(b, j, slot):
            cp.start()

    @pl.when(step + 1 < n_b * n_spans)
    def _():
        nxt = step + 1
        for cp in copies(nxt // n_spans, lax.rem(nxt, n_spans), 1 - slot):
            cp.start()

    for cp in copies(b, j, slot):
        cp.wait()

    means = []
    for n in range(bps):
        s8 = _fold8(kbuf[slot, n * blk_rows:(n + 1) * blk_rows, :], jnp.sum)
        means.append((s8[:N_KV] + s8[N_KV:]) * (1.0 / MOBA_BLOCK))
    kmean_s[pl.ds(pl.multiple_of(j * bps * N_KV, bps * N_KV), bps * N_KV), :] = (
        jnp.concatenate(means, axis=0))

    @pl.when(j == n_spans - 1)
    def _():
        km_hi, km_lo = _split_hi_lo(kmean_s[...])
        q = q_ref[0]
        gate = _nt(q, km_hi) + _nt(q, km_lo)
        gate = jnp.where(_own_kv_lane(gate.shape), gate, -jnp.inf)
        lane = lax.broadcasted_iota(I32, gate.shape, 1).astype(F32)
        out_lane = lax.broadcasted_iota(I32, (N_HEADS, LANE), 1)
        picked = jnp.zeros((N_HEADS, LANE), F32)
        for r in range(n_sel):
            best = jnp.max(gate, axis=1, keepdims=True)
            idx = jnp.min(jnp.where(gate == best, lane, float(n_blocks * N_KV)), axis=1, keepdims=True)
            picked = jnp.where(out_lane == r, idx, picked)
            gate = jnp.where(lane == idx, -jnp.inf, gate)
        sel_o[0] = picked.astype(I32) // N_KV


def _sample_moba_gate(page_table, qb, k_pool):
    n_b, n_pages = page_table.shape
    page = k_pool.shape[1] // N_KV
    n_spans = n_pages * page // SPAN
    n_blocks = (n_pages * page + 1) // MOBA_BLOCK
    grid_spec = pltpu.PrefetchScalarGridSpec(
        num_scalar_prefetch=1,
        grid=(n_b, n_spans),
        in_specs=[pl.BlockSpec((1, N_HEADS, HEAD_DIM), lambda b, j, pt: (b, 0, 0)),
                  pl.BlockSpec(memory_space=pl.ANY)],
        out_specs=pl.BlockSpec((1, N_HEADS, LANE), lambda b, j, pt: (b, 0, 0)),
        scratch_shapes=[pltpu.VMEM((2, SPAN_ROWS, HEAD_DIM), F32), pltpu.SemaphoreType.DMA((2,)),
                        pltpu.VMEM((n_blocks * N_KV, HEAD_DIM), F32)],
    )
    return pl.pallas_call(
        functools.partial(_smoba_gate_kernel, n_spans=n_spans, page=page,
                          n_sel=min(MOBA_TOPK, n_blocks), n_blocks=n_blocks),
        grid_spec=grid_spec,
        out_shape=jax.ShapeDtypeStruct((n_b, N_HEADS, LANE), I32),
        compiler_params=_cparams(("arbitrary", "arbitrary")),
        name="sample_moba_gate",
    )(page_table, qb, k_pool)


def _smoba_attn_kernel(pt_ref, sel_ref, selv_ref, q_ref, kn_ref, vn_ref, tail_ref, far_ref, b0_ref,
                       kpool_ref, vpool_ref, o_ref, kbuf, vbuf, sem, *, n_sel, page, n_blocks):
    b = pl.program_id(0)
    n_b = pl.num_programs(0)
    ppb = MOBA_BLOCK // page
    slot = lax.rem(b, 2)
    n_keys = n_sel * MOBA_BLOCK

    def copies(bb, sl):
        out = []
        for h in range(N_HEADS):
            kh = h // GQA
            for r in range(n_sel):
                blk = sel_ref[(bb * N_HEADS + h) * n_sel + r]
                for p in range(ppb):
                    pg = pt_ref[bb, blk * ppb + p]
                    rows = pl.ds((r * ppb + p) * page, page)
                    out.append(pltpu.make_async_copy(kpool_ref.at[pg, :, kh, :],
                                                     kbuf.at[sl, h, rows], sem.at[0, sl]))
                    out.append(pltpu.make_async_copy(vpool_ref.at[pg, :, kh, :],
                                                     vbuf.at[sl, h, rows], sem.at[1, sl]))
        return out

    @pl.when(b == 0)
    def _():
        for cp in copies(b, slot):
            cp.start()

    @pl.when(b + 1 < n_b)
    def _():
        for cp in copies(b + 1, 1 - slot):
            cp.start()

    for cp in copies(b, slot):
        cp.wait()

    q = q_ref[0]
    row = lax.broadcasted_iota(I32, (N_HEADS, n_keys), 0)
    lg = jnp.zeros((N_HEADS, n_keys), F32)
    for h in range(N_HEADS):
        lg = jnp.where(row == h, _nt(q, kbuf[slot, h].astype(BF16)), lg)
    selv = selv_ref[0]
    tail = tail_ref[...]
    far = far_ref[...]
    bias = jnp.concatenate(
        [jnp.where(selv[:, r:r + 1] == n_blocks - 1, tail, far) for r in range(n_sel)], axis=1)
    lg = lg * ATTN_SCALE + bias
    qf = q.astype(F32)
    kn = _rows_per_kv(kn_ref[0].astype(BF16).astype(F32))
    lg_new = jnp.sum(qf * kn, axis=1, keepdims=True) * ATTN_SCALE + b0_ref[...]
    m = jnp.maximum(jnp.max(lg, axis=1, keepdims=True), lg_new)
    p = jnp.exp(lg - m)
    p_new = jnp.exp(lg_new - m)
    denom = jnp.sum(p, axis=1, keepdims=True) + p_new
    pb = p.astype(BF16)
    acc = p_new.astype(BF16).astype(F32) * _rows_per_kv(vn_ref[0].astype(BF16).astype(F32))
    row_o = lax.broadcasted_iota(I32, (N_HEADS, HEAD_DIM), 0)
    for h in range(N_HEADS):
        acc = acc + jnp.where(row_o == h, _dot(pb, vbuf[slot, h].astype(BF16)), 0.0)
    o_ref[0] = (acc * (1.0 / denom)).astype(BF16)


def _sample_moba_attn(page_table, sel, qb, kb_new, vb_new, tail, far, bias0, k_pool, v_pool):
    n_b, n_pages = page_table.shape
    page = k_pool.shape[1]
    n_blocks = (n_pages * page + 1) // MOBA_BLOCK
    n_sel = min(MOBA_TOPK, n_blocks)
    sel_flat = sel[:, :, :n_sel].reshape(-1)
    per_b = lambda shape: pl.BlockSpec((1,) + shape, lambda b, pt, sf: (b, 0, 0))
    const = lambda shape: pl.BlockSpec(shape, lambda b, pt, sf: (0, 0))
    grid_spec = pltpu.PrefetchScalarGridSpec(
        num_scalar_prefetch=2,
        grid=(n_b,),
        in_specs=[per_b((N_HEADS, LANE)), per_b((N_HEADS, HEAD_DIM)), per_b((1, KV_W)),
                  per_b((1, KV_W)), const((N_HEADS, MOBA_BLOCK)), const((N_HEADS, 1)),
                  const((N_HEADS, 1)),
                  pl.BlockSpec(memory_space=pl.ANY), pl.BlockSpec(memory_space=pl.ANY)],
        out_specs=per_b((N_HEADS, HEAD_DIM)),
        scratch_shapes=[pltpu.VMEM((2, N_HEADS, n_sel * MOBA_BLOCK, HEAD_DIM), F32),
                        pltpu.VMEM((2, N_HEADS, n_sel * MOBA_BLOCK, HEAD_DIM), F32),
                        pltpu.SemaphoreType.DMA((2, 2))],
    )
    return pl.pallas_call(
        functools.partial(_smoba_attn_kernel, n_sel=n_sel, page=page, n_blocks=n_blocks),
        grid_spec=grid_spec,
        out_shape=jax.ShapeDtypeStruct((n_b, N_HEADS, HEAD_DIM), BF16),
        compiler_params=_cparams(("arbitrary",)),
        name="sample_moba_attn",
    )(page_table, sel_flat, sel, qb, kb_new, vb_new, tail, far, bias0, k_pool, v_pool)


def _split_w_in(w_in):
    widths = (Q_W, KV_W, KV_W, N_IDX_HEADS * IDX_DIM, IDX_DIM, N_IDX_HEADS, Q_W, KV_W, KV_W,
              w_in.shape[0], w_in.shape[0])
    pts = [0]
    for w in widths:
        pts.append(pts[-1] + w)
    qa, ka, va, qi, ki, wi, qb, kb, vb, ga, gb = [w_in[:, pts[n]:pts[n + 1]] for n in range(11)]
    pad = jnp.zeros((w_in.shape[0], LANE - IDX_DIM - N_IDX_HEADS), w_in.dtype)
    w_a = jnp.concatenate([qa, ka, va, qi, ki, wi, pad], axis=1).astype(BF16)
    w_b = jnp.concatenate([qb, kb, vb], axis=1).astype(BF16)
    return w_a, w_b, ga.astype(BF16), gb.astype(BF16)


def kernel(x_prompt, x_sample, cache_ka, cache_va, cache_kidx, cache_kb, cache_vb, page_table,
           c_prompt, c_sample, rel_bias, w_ada, b_ada, g_norm1, w_in, g_qa, g_ka, g_kidx, g_qb,
           g_kb, w_pa, w_pb, w_out, g_norm2, w_ffn_in, w_ffn_out):
    depth = w_in.shape[0]
    n_b, n_s, d = x_prompt.shape
    n_db, n_ds, _ = x_sample.shape
    n_pool, page = cache_ka.shape[1], cache_ka.shape[2]
    past_len = page_table.shape[1] * page
    assert n_ds == 1, "the decode kernels handle one new token per sequence"
    assert n_s % MOBA_BLOCK == 0 and past_len % SPAN == 0 and SPAN % page == 0
    assert MOBA_BLOCK % page == 0 and past_len >= 4 * DSA_TOPK

    bias_mat, bias_tail = _bias_tables(rel_bias, SPAN)
    mat_a, mat_b = bias_mat[:, :N_HEADS], bias_mat[:, N_HEADS:]
    tail = bias_tail[:, 0, :]
    tail_a4 = jnp.repeat(tail[:N_HEADS], N_KV, axis=1)
    tail_b = tail[N_HEADS:, SPAN - MOBA_BLOCK:]
    far = rel_bias[N_BUCKETS - 1].reshape(-1, 1)
    bias0 = rel_bias[0].reshape(-1, 1)

    xp = x_prompt
    xs = x_sample.reshape(1, n_db, d)
    c_all = jnp.concatenate([c_prompt, c_sample], axis=0)
    rows_p, rows_s = [], []
    for l in range(depth):
        mod = _adaln(c_all, w_ada[l], b_ada[l])
        mod_p = [m.reshape(n_b, 1, d) for m in jnp.split(mod[:n_b], 6, axis=-1)]
        mod_s = [m.reshape(1, n_db, d) for m in jnp.split(mod[n_b:], 6, axis=-1)]
        w_a, w_b, w_ga, w_gb = _split_w_in(w_in[l])
        w_pa_l, w_pb_l, w_out_l = w_pa[l].astype(BF16), w_pb[l].astype(BF16), w_out[l].astype(BF16)
        w_fi, w_fo = w_ffn_in[l].astype(BF16), w_ffn_out[l].astype(BF16)
        g_kidx_pad = jnp.pad(g_kidx[l], (0, LANE - IDX_DIM)).reshape(1, LANE)

        def project(x, mods):
            h, *dsa_side = _proj_a(x, g_norm1[l], mods[0], mods[1], w_a, g_qa[l], g_ka[l], g_kidx_pad)
            return h, dsa_side, _proj_b(h, w_b, g_qb[l], g_kb[l])

        def finish(x, h, oa, ob, mods):
            m = _gate_merge(h, oa, ob, w_ga, w_gb, w_pa_l, w_pb_l)
            x2, h2 = _out_proj(m, x, w_out_l, mods[2], g_norm2[l], mods[3], mods[4])
            return _ffn(h2, x2, w_fi, w_fo, mods[5])

        h, (qa, ka, va, qi, ki, misc, kilo, kihi), (qb, kb, vb) = project(xp, mod_p)
        oa = _dsa_prompt(qi, misc, kilo, kihi, qa, ka, va, mat_a)
        ob = _moba_prompt(qb, kb, vb, mat_b)
        xp = finish(xp, h, oa, ob, mod_p)
        rows_p.append((ka.reshape(n_b, n_s, N_KV, HEAD_DIM), va.reshape(n_b, n_s, N_KV, HEAD_DIM),
                       ki, kb.reshape(n_b, n_s, N_KV, HEAD_DIM), vb.reshape(n_b, n_s, N_KV, HEAD_DIM)))

        h, (qa, ka, va, qi, ki, misc, kilo, kihi), (qb, kb, vb) = project(xs, mod_s)
        tok = lambda t: t.reshape(n_db, 1, -1)
        heads = lambda t: t.reshape(n_db, N_HEADS, HEAD_DIM)
        rows_view = lambda c: c[l].reshape(n_pool, page * N_KV, HEAD_DIM)
        wi = misc[0, :, _WI_LANE:_WI_LANE + N_IDX_HEADS].reshape(n_db, N_IDX_HEADS, 1)
        mask, new_sel = _sample_index(page_table, qi.reshape(n_db, N_IDX_HEADS, IDX_DIM), wi,
                                      tok(ki[0]), jnp.swapaxes(cache_kidx[l], 1, 2))
        oa = _sample_dsa(page_table, jnp.repeat(mask, N_KV, axis=2), new_sel, heads(qa[0]),
                         tok(ka[0]), tok(va[0]), tail_a4, far[:N_HEADS], bias0[:N_HEADS],
                         rows_view(cache_ka), rows_view(cache_va))
        sel = _sample_moba_gate(page_table, heads(qb[0]), rows_view(cache_kb))
        ob = _sample_moba_attn(page_table, sel, heads(qb[0]), tok(kb[0]), tok(vb[0]),
                               tail_b, far[N_HEADS:], bias0[N_HEADS:], cache_kb[l], cache_vb[l])
        xs = finish(xs, h, oa.reshape(1, n_db, Q_W), ob.reshape(1, n_db, Q_W), mod_s)
        kv4 = lambda t: t.reshape(n_db, 1, N_KV, HEAD_DIM)
        rows_s.append((kv4(ka[0]), kv4(va[0]), ki.reshape(n_db, 1, IDX_DIM), kv4(kb[0]), kv4(vb[0])))

    outs_p = [jnp.stack(t) for t in zip(*rows_p)]
    outs_s = [jnp.stack(t) for t in zip(*rows_s)]
    return (xp, xs.reshape(n_db, n_ds, d), *outs_p, *outs_s)
```

```python
import functools
import math

import jax
import jax.numpy as jnp
from jax import lax
from jax.experimental import pallas as pl
from jax.experimental.pallas import tpu as pltpu

F32 = jnp.float32
BF16 = jnp.bfloat16
I32 = jnp.int32

HEAD_DIM = 128
N_HEADS = 8
N_KV = 4
GQA = N_HEADS // N_KV
N_IDX_HEADS = 16
IDX_DIM = 64
IDX_WEIGHT_SCALE = (N_IDX_HEADS * IDX_DIM) ** -0.5
DSA_TOPK = 256
MOBA_BLOCK = 256
MOBA_TOPK = 3
N_BUCKETS = 32
MAX_DISTANCE = 128
EPS = 1e-6
NEG = -1e30
ATTN_SCALE = HEAD_DIM ** -0.5

SUBLANE = 8
LANE = 128
INT_MIN = -(2 ** 31)
VMEM_LIMIT_V7X = 56 * 1024 * 1024

TQ = 128
TK = 128
TK2 = MOBA_BLOCK
CPP = TK2 // TK
KV_W = N_KV * HEAD_DIM
Q_W = N_HEADS * HEAD_DIM


def _cparams(sem):
    return pltpu.CompilerParams(dimension_semantics=sem, vmem_limit_bytes=VMEM_LIMIT_V7X)


def _nt(a, b):
    return lax.dot_general(a, b, (((1,), (1,)), ((), ())), preferred_element_type=F32)


def _dot(a, b):
    return jnp.dot(a, b, preferred_element_type=F32)


def _bucket_lower_bounds():
    max_exact = N_BUCKETS // 2

    def bucket(n):
        if n < max_exact:
            return n
        large = max_exact + int(math.log(n / max_exact) / math.log(MAX_DISTANCE / max_exact)
                                * (N_BUCKETS - max_exact))
        return min(large, N_BUCKETS - 1)

    lows = [None] * N_BUCKETS
    for n in range(0, 4 * MAX_DISTANCE):
        b = bucket(n)
        if lows[b] is None:
            lows[b] = n
    return lows


_BUCKET_LOW = _bucket_lower_bounds()


def _bias_of_distance(dist, table_ref, h):
    out = jnp.full(dist.shape, table_ref[0, h], F32)
    for b in range(1, N_BUCKETS):
        if _BUCKET_LOW[b] is None:
            continue
        out = jnp.where(dist >= _BUCKET_LOW[b], table_ref[b, h], out)
    return out


def _bias_kernel(table_ref, mat_ref, tail_ref, *, tail_w):
    h = pl.program_id(0)
    kl = lax.broadcasted_iota(I32, (TK, TQ), 0)
    ql = lax.broadcasted_iota(I32, (TK, TQ), 1)
    for dd in range(3):
        mat_ref[dd, 0] = _bias_of_distance(ql - kl + dd * TQ, table_ref, h)
    lane = lax.broadcasted_iota(I32, (1, tail_w), 1)
    tail_ref[0] = _bias_of_distance(tail_w - lane, table_ref, h)


def _bias_tables(rel_bias, tail_w):
    n_h = rel_bias.shape[1]
    return pl.pallas_call(
        functools.partial(_bias_kernel, tail_w=tail_w),
        grid=(n_h,),
        in_specs=[pl.BlockSpec(memory_space=pltpu.SMEM)],
        out_specs=[pl.BlockSpec((3, 1, TK, TQ), lambda h: (0, h, 0, 0)),
                   pl.BlockSpec((1, 1, tail_w), lambda h: (h, 0, 0))],
        out_shape=[jax.ShapeDtypeStruct((3, n_h, TK, TQ), F32),
                   jax.ShapeDtypeStruct((n_h, 1, tail_w), F32)],
        compiler_params=_cparams(("arbitrary",)),
        name="bias_tables",
    )(rel_bias)


def _ada_kernel(c_ref, w_ref, b_ref, o_ref):
    c = c_ref[...]
    s = (c / (1.0 + jnp.exp(-c))).astype(BF16)
    o_ref[...] = _dot(s, w_ref[...].astype(BF16)) + b_ref[...]


def _adaln(c, w, b):
    n, d = c.shape
    tn = min(1024, d)
    return pl.pallas_call(
        _ada_kernel,
        grid=(w.shape[1] // tn,),
        in_specs=[pl.BlockSpec((n, d), lambda j: (0, 0)),
                  pl.BlockSpec((d, tn), lambda j: (0, j)),
                  pl.BlockSpec((1, tn), lambda j: (0, j))],
        out_specs=pl.BlockSpec((n, tn), lambda j: (0, j)),
        out_shape=jax.ShapeDtypeStruct((n, w.shape[1]), F32),
        compiler_params=_cparams(("arbitrary",)),
        name="adaln",
    )(c, w, b.reshape(1, -1))


def _row_tile(r):
    return min(r, 512)


def _tok_spec(tr, w):
    return pl.BlockSpec((1, tr, w), lambda g, r: (g, r, 0))


def _mod_spec(mod, tr):
    if mod.shape[1] == 1:
        return pl.BlockSpec((1, 1, mod.shape[2]), lambda g, r: (g, 0, 0))
    return pl.BlockSpec((1, tr, mod.shape[2]), lambda g, r: (g, r, 0))


def _const_spec(shape):
    nd = len(shape)
    return pl.BlockSpec(shape, lambda g, r: (0,) * nd, pipeline_mode=pl.Buffered(1))


def _rms(x, g):
    return x * lax.rsqrt(jnp.mean(x * x, axis=-1, keepdims=True) + EPS) * g


def _query_heads_store(z, g, o_ref):
    for hh in range(N_HEADS):
        cols = slice(hh * HEAD_DIM, (hh + 1) * HEAD_DIM)
        o_ref[0, :, cols] = (_rms(z[:, cols], g) * ATTN_SCALE).astype(BF16)


def _kv_rows_store(z, o_ref, g=None):
    for kh in range(N_KV):
        v = z[:, kh * HEAD_DIM:(kh + 1) * HEAD_DIM]
        o_ref[0, pl.ds(kh, z.shape[0], stride=N_KV), :] = v if g is None else _rms(v, g)


def _kv_rows_spec(tr):
    return pl.BlockSpec((1, tr * N_KV, HEAD_DIM), lambda g, r: (g, r, 0))


_A_QA = (0, Q_W)
_A_KA = (_A_QA[1], _A_QA[1] + KV_W)
_A_VA = (_A_KA[1], _A_KA[1] + KV_W)
_A_QI = (_A_VA[1], _A_VA[1] + N_IDX_HEADS * IDX_DIM)
_A_MISC = (_A_QI[1], _A_QI[1] + LANE)
_WI_LANE = IDX_DIM


def _proj_a_kernel(x_ref, g1_ref, sh_ref, sc_ref, w_ref, gq_ref, gk_ref, gki_ref,
                   h_o, qa_o, ka_o, va_o, qi_o, ki_o, misc_o, kilo_o, kihi_o):
    h = (_rms(x_ref[0], g1_ref[...]) * (1.0 + sc_ref[0]) + sh_ref[0]).astype(BF16)
    h_o[0] = h
    _query_heads_store(_dot(h, w_ref[:, _A_QA[0]:_A_QA[1]]), gq_ref[...], qa_o)
    _kv_rows_store(_dot(h, w_ref[:, _A_KA[0]:_A_KA[1]]), ka_o, gk_ref[...])
    _kv_rows_store(_dot(h, w_ref[:, _A_VA[0]:_A_VA[1]]), va_o)
    qi_o[0] = _dot(h, w_ref[:, _A_QI[0]:_A_QI[1]]).astype(BF16)
    z = _dot(h, w_ref[:, _A_MISC[0]:_A_MISC[1]])
    is_ki = lax.broadcasted_iota(I32, z.shape, 1) < IDX_DIM
    ssq = jnp.sum(jnp.where(is_ki, z * z, 0.0), axis=-1, keepdims=True) * (1.0 / IDX_DIM)
    kin = jnp.where(is_ki, z * lax.rsqrt(ssq + EPS) * gki_ref[...], 0.0)
    ki_o[0] = kin[:, :IDX_DIM]
    misc_o[0] = jnp.where(is_ki, kin, z * IDX_WEIGHT_SCALE)
    kilo_o[0] = kin.astype(BF16)
    kihi_o[0] = pltpu.roll(kin, IDX_DIM, axis=1).astype(BF16)


def _proj_a(x, g1, sh, sc, w_a, g_qa, g_ka, g_kidx_pad):
    n_g, n_r, d = x.shape
    tr = _row_tile(n_r)
    widths = [(d, BF16), (Q_W, BF16), None, None, (N_IDX_HEADS * IDX_DIM, BF16),
              (IDX_DIM, F32), (LANE, F32), (LANE, BF16), (LANE, BF16)]
    kv_rows = jax.ShapeDtypeStruct((n_g, n_r * N_KV, HEAD_DIM), F32)
    return pl.pallas_call(
        _proj_a_kernel,
        grid=(n_g, n_r // tr),
        in_specs=[_tok_spec(tr, d), _const_spec((1, d)), _mod_spec(sh, tr), _mod_spec(sc, tr),
                  _const_spec(w_a.shape), _const_spec((1, HEAD_DIM)),
                  _const_spec((1, HEAD_DIM)), _const_spec((1, LANE))],
        out_specs=[_kv_rows_spec(tr) if w is None else _tok_spec(tr, w[0]) for w in widths],
        out_shape=[kv_rows if w is None else jax.ShapeDtypeStruct((n_g, n_r, w[0]), w[1])
                   for w in widths],
        compiler_params=_cparams(("arbitrary", "arbitrary")),
        name="proj_dsa",
    )(x, g1.reshape(1, d), sh, sc, w_a, g_qa.reshape(1, -1), g_ka.reshape(1, -1), g_kidx_pad)


def _proj_b_kernel(h_ref, w_ref, gq_ref, gk_ref, qb_o, kb_o, vb_o):
    h = h_ref[0]
    _query_heads_store(_dot(h, w_ref[:, 0:Q_W]), gq_ref[...], qb_o)
    _kv_rows_store(_dot(h, w_ref[:, Q_W:Q_W + KV_W]), kb_o, gk_ref[...])
    _kv_rows_store(_dot(h, w_ref[:, Q_W + KV_W:Q_W + 2 * KV_W]), vb_o)


def _proj_b(h, w_b, g_qb, g_kb):
    n_g, n_r, d = h.shape
    tr = _row_tile(n_r)
    kv_rows = jax.ShapeDtypeStruct((n_g, n_r * N_KV, HEAD_DIM), F32)
    return pl.pallas_call(
        _proj_b_kernel,
        grid=(n_g, n_r // tr),
        in_specs=[_tok_spec(tr, d), _const_spec(w_b.shape), _const_spec((1, HEAD_DIM)),
                  _const_spec((1, HEAD_DIM))],
        out_specs=[_tok_spec(tr, Q_W), _kv_rows_spec(tr), _kv_rows_spec(tr)],
        out_shape=[jax.ShapeDtypeStruct((n_g, n_r, Q_W), BF16), kv_rows, kv_rows],
        compiler_params=_cparams(("arbitrary", "arbitrary")),
        name="proj_moba",
    )(h, w_b, g_qb.reshape(1, -1), g_kb.reshape(1, -1))


def _sigmoid(x):
    return 1.0 / (1.0 + jnp.exp(-x))


def _gate_merge_kernel(h_ref, oa_ref, ob_ref, wga_ref, wgb_ref, wpa_ref, wpb_ref, m_o):
    h = h_ref[0]
    m = (_sigmoid(_dot(h, wga_ref[...])) * _dot(oa_ref[0], wpa_ref[...])
         + _sigmoid(_dot(h, wgb_ref[...])) * _dot(ob_ref[0], wpb_ref[...]))
    m_o[0] = m.astype(BF16)


def _gate_merge(h, oa, ob, w_ga, w_gb, w_pa, w_pb):
    n_g, n_r, d = h.shape
    tr = _row_tile(n_r)
    tn = d // 2
    tok = lambda w: pl.BlockSpec((1, tr, w), lambda c, g, r: (g, r, 0))
    col = lambda k: pl.BlockSpec((k, tn), lambda c, g, r: (0, c))
    return pl.pallas_call(
        _gate_merge_kernel,
        grid=(d // tn, n_g, n_r // tr),
        in_specs=[tok(d), tok(Q_W), tok(Q_W), col(d), col(d), col(Q_W), col(Q_W)],
        out_specs=pl.BlockSpec((1, tr, tn), lambda c, g, r: (g, r, c)),
        out_shape=jax.ShapeDtypeStruct((n_g, n_r, d), BF16),
        compiler_params=_cparams(("arbitrary", "arbitrary", "arbitrary")),
        name="gate_merge",
    )(h, oa, ob, w_ga, w_gb, w_pa, w_pb)


def _out_proj_kernel(m_ref, x_ref, w_ref, gt_ref, g2_ref, sh_ref, sc_ref, x2_o, h2_o):
    x2 = x_ref[0] + gt_ref[0] * _dot(m_ref[0], w_ref[...])
    x2_o[0] = x2
    h2_o[0] = (_rms(x2, g2_ref[...]) * (1.0 + sc_ref[0]) + sh_ref[0]).astype(BF16)


def _out_proj(m, x, w_out, gt, g2, sh, sc):
    n_g, n_r, d = x.shape
    tr = _row_tile(n_r)
    return pl.pallas_call(
        _out_proj_kernel,
        grid=(n_g, n_r // tr),
        in_specs=[_tok_spec(tr, d), _tok_spec(tr, d), _const_spec(w_out.shape), _mod_spec(gt, tr),
                  _const_spec((1, d)), _mod_spec(sh, tr), _mod_spec(sc, tr)],
        out_specs=[_tok_spec(tr, d), _tok_spec(tr, d)],
        out_shape=[jax.ShapeDtypeStruct((n_g, n_r, d), F32),
                   jax.ShapeDtypeStruct((n_g, n_r, d), BF16)],
        compiler_params=_cparams(("arbitrary", "arbitrary")),
        name="out_proj",
    )(m, x, w_out, gt, g2.reshape(1, d), sh, sc)


def _ffn_kernel(h_ref, x_ref, wa_ref, wu_ref, wo_ref, gt_ref, y_o, acc_s):
    j = pl.program_id(2)

    @pl.when(j == 0)
    def _():
        acc_s[...] = jnp.zeros_like(acc_s)

    h = h_ref[0]
    a = _dot(h, wa_ref[...])
    u = _dot(h, wu_ref[...])
    act = (a * _sigmoid(a) * u).astype(BF16)
    acc_s[...] += _dot(act, wo_ref[...])

    @pl.when(j == pl.num_programs(2) - 1)
    def _():
        y_o[0] = x_ref[0] + gt_ref[0] * acc_s[...]


def _ffn(h2, x2, w_in, w_out, gt):
    n_g, n_r, d = x2.shape
    d_ff = w_out.shape[0]
    tr = _row_tile(n_r)
    tf = 512 if d_ff % 512 == 0 else 256
    n_f = d_ff // tf
    tok = lambda w: pl.BlockSpec((1, tr, w), lambda g, r, j: (g, r, 0))
    if gt.shape[1] == 1:
        gt_spec = pl.BlockSpec((1, 1, d), lambda g, r, j: (g, 0, 0))
    else:
        gt_spec = pl.BlockSpec((1, tr, d), lambda g, r, j: (g, r, 0))
    return pl.pallas_call(
        _ffn_kernel,
        grid=(n_g, n_r // tr, n_f),
        in_specs=[tok(d), tok(d),
                  pl.BlockSpec((d, tf), lambda g, r, j: (0, j)),
                  pl.BlockSpec((d, tf), lambda g, r, j: (0, j + n_f)),
                  pl.BlockSpec((tf, d), lambda g, r, j: (j, 0)),
                  gt_spec],
        out_specs=tok(d),
        out_shape=jax.ShapeDtypeStruct((n_g, n_r, d), F32),
        scratch_shapes=[pltpu.VMEM((tr, d), F32)],
        compiler_params=_cparams(("arbitrary", "arbitrary", "arbitrary")),
        name="swiglu_ffn",
    )(h2, x2, w_in, w_in, w_out, gt)


def _sortable(x):
    bits = pltpu.bitcast(x, I32)
    return bits ^ (lax.shift_right_arithmetic(bits, 31) & 0x7FFFFFFF)


ONES_ROWS = 16
VT_ROWS = HEAD_DIM + ONES_ROWS


def _stage_kv(k_ref, v_ref, k_s, vt_s, n_blocks):
    def body(c, carry):
        rows = pl.ds(pl.multiple_of(c * TK2, TK2), TK2)
        for kh in range(N_KV):
            cols = slice(kh * HEAD_DIM, (kh + 1) * HEAD_DIM)
            src = pl.ds(c * (TK2 * N_KV) + kh, TK2, stride=N_KV)
            k_s[rows, cols] = k_ref[0, src, :].astype(BF16)
            vt_s[c, kh, :HEAD_DIM, :] = v_ref[0, src, :].T.astype(BF16)
            vt_s[c, kh, HEAD_DIM:, :] = jnp.ones((ONES_ROWS, TK2), BF16)
        return carry
    lax.fori_loop(0, n_blocks, body, 0)


def _fold8(x, op):
    r, w = x.shape
    return op(x.reshape(r // SUBLANE, SUBLANE, w), axis=0)


def _loop_pairs(n, body, init):
    carry = lax.fori_loop(0, n // 2, lambda j, c: body(2 * j + 1, body(2 * j, c)), init)
    return lax.cond(n % 2 == 1, lambda c: body(n - 1, c), lambda c: c, carry)


def _loop_quads(n, body, init):
    def quad(j, c):
        for r in range(4):
            c = body(4 * j + r, c)
        return c
    carry = lax.fori_loop(0, n // 4, quad, init)
    base = (n // 4) * 4
    carry = lax.cond(n % 4 >= 2, lambda c: body(base + 1, body(base, c)), lambda c: c, carry)
    return lax.cond(n % 2 == 1, lambda c: body(n - 1, c), lambda c: c, carry)


def _attend(i, n_past, mask_of, own_mask_of, q_ref, k_s, vt_s, bias_ref, lg_s, acc_s, o_ref):
    n_pc = n_past if own_mask_of is None else n_past + 1

    def logits_body(c2, mx, mask_of=mask_of):
        rows = pl.ds(pl.multiple_of(c2 * TK2, TK2), TK2)
        out = []
        for kh in range(N_KV):
            q2 = jnp.concatenate(
                [q_ref[0, :, (kh * GQA + g) * HEAD_DIM:(kh * GQA + g + 1) * HEAD_DIM]
                 for g in range(GQA)], axis=0)
            lg = _nt(k_s[rows, kh * HEAD_DIM:(kh + 1) * HEAD_DIM], q2)
            best = mx[kh]
            for a in range(CPP):
                dd = jnp.clip(i - (c2 * CPP + a), 0, 2)
                folded = []
                for g in range(GQA):
                    h = kh * GQA + g
                    t = lg[a * TK:(a + 1) * TK, g * TQ:(g + 1) * TQ] + bias_ref[dd, h]
                    t = jnp.where(mask_of(c2, a, h), t, NEG)
                    lg_s[kh, c2, a * TK:(a + 1) * TK, g * TQ:(g + 1) * TQ] = t
                    folded.append(_fold8(t, jnp.max))
                best = jnp.maximum(best, jnp.concatenate(folded, axis=1))
            out.append(best)
        return tuple(out)

    mx0 = tuple(jnp.full((SUBLANE, GQA * TQ), NEG, F32) for _ in range(N_KV))
    mx = _loop_quads(n_past, logits_body, mx0)
    if own_mask_of is not None:
        mx = logits_body(n_past, mx, own_mask_of)
    m = [jnp.max(x, axis=0, keepdims=True) for x in mx]

    acc_s[...] = jnp.zeros(acc_s.shape, F32)

    def pv_body(c2, carry):
        for kh in range(N_KV):
            p = jnp.exp(lg_s[kh, c2] - m[kh])
            acc_s[kh] += _dot(vt_s[c2, kh], p.astype(BF16))
        return carry

    _loop_quads(n_pc, pv_body, 0)
    for kh in range(N_KV):
        o2 = acc_s[kh, :HEAD_DIM, :] * (1.0 / acc_s[kh, HEAD_DIM:HEAD_DIM + 1, :])
        for g in range(GQA):
            h = kh * GQA + g
            o_ref[0, :, h * HEAD_DIM:(h + 1) * HEAD_DIM] = o2[:, g * TQ:(g + 1) * TQ].T.astype(BF16)


def _dsa_kernel(qi_ref, misc_ref, kilo_ref, kihi_ref, qa_ref, ka_ref, va_ref, bias_ref, o_ref,
                k_s, vt_s, sk_s, w_s, lg_s, acc_s, *, n_blocks, top_k, pos_bits):
    i = pl.program_id(1)
    n_pc = i // CPP + 1
    kl = lax.broadcasted_iota(I32, (TK, TQ), 0)
    ql = lax.broadcasted_iota(I32, (TK, TQ), 1)
    qpos = i * TQ + lax.broadcasted_iota(I32, (1, TQ), 1)

    @pl.when(i == 0)
    def _():
        _stage_kv(ka_ref, va_ref, k_s, vt_s, n_blocks)

    w_s[...] = misc_ref[0].T

    def score_body(c2, carry):
        rows = pl.ds(pl.multiple_of(c2 * TK2, TK2), TK2)
        k2 = jnp.concatenate([kilo_ref[0, rows, :], kihi_ref[0, rows, :]], axis=0)
        acc = [jnp.zeros((TK, TQ), F32) for _ in range(CPP)]
        for p2 in range(N_IDX_HEADS // 4):
            qp = jnp.concatenate([qi_ref[0, :, (2 * p2 + pair) * LANE:(2 * p2 + pair + 1) * LANE]
                                  for pair in range(2)], axis=0)
            s = jnp.maximum(_nt(k2, qp), 0.0)
            for half in range(2):
                for pair in range(2):
                    r = _WI_LANE + 4 * p2 + 2 * pair + half
                    w = w_s[r:r + 1, :]
                    for a in range(CPP):
                        r0 = half * TK2 + a * TK
                        acc[a] = acc[a] + w * s[r0:r0 + TK, pair * TQ:(pair + 1) * TQ]
        for a in range(CPP):
            c = c2 * CPP + a
            admissible = (c * TK + kl) <= (i * TQ + ql)
            sk_s[c] = jnp.where(admissible, _sortable(acc[a]), INT_MIN)
        return carry
    _loop_pairs(n_pc, score_body, 0)

    def count(pred):
        def body(c2, acc):
            for a in range(CPP):
                c = c2 * CPP + a
                acc = acc + _fold8(jnp.where(pred(c, sk_s[c]), 1.0, 0.0), jnp.sum)
            return acc
        acc = _loop_pairs(n_pc, body, jnp.zeros((SUBLANE, TQ), F32))
        return jnp.sum(acc, axis=0, keepdims=True)

    kf = float(top_k)
    few = qpos + 1 <= top_k

    def threshold():
        zero = jnp.zeros((1, TQ), I32)
        t0 = jnp.where(count(lambda c, s: s >= zero) >= kf, zero, INT_MIN)

        def bit_body(s, t):
            cand = t + lax.shift_left(jnp.int32(1), 30 - s)
            return jnp.where(count(lambda c, sk: sk >= cand) >= kf, cand, t)
        return lax.fori_loop(0, 31, bit_body, t0)

    t = lax.cond((i + 1) * TQ > top_k, threshold, lambda: jnp.full((1, TQ), INT_MIN + 1, I32))
    t = jnp.where(few, INT_MIN + 1, t)

    @pl.when((i + 1) * TQ > top_k)
    def _():
        need = kf - count(lambda c, sk: sk > t)
        n_eq = count(lambda c, sk: sk == t)
        tied = jnp.max(jnp.where(jnp.logical_and(n_eq > need, jnp.logical_not(few)), 1.0, 0.0))

        @pl.when(tied > 0.0)
        def _():
            def pos_body(s, lo):
                cand = lo + lax.shift_left(jnp.int32(1), pos_bits - 1 - s)
                below = count(lambda c, sk: jnp.logical_and(sk == t, c * TK + kl < cand))
                return jnp.where(below < need, cand, lo)
            last = lax.fori_loop(0, pos_bits, pos_body, jnp.zeros((1, TQ), I32))

            def drop_body(c, carry):
                sk = sk_s[c]
                drop = jnp.logical_and(sk == t, c * TK + kl > last)
                sk_s[c] = jnp.where(drop, INT_MIN, sk)
                return carry
            lax.fori_loop(0, i + 1, drop_body, 0)

    _attend(i, n_pc, lambda c2, a, h: sk_s[c2 * CPP + a] >= t, None, qa_ref, k_s, vt_s, bias_ref,
            lg_s, acc_s, o_ref)


def _attn_scratch(n_blocks):
    return [pltpu.VMEM((n_blocks * TK2, KV_W), BF16),
            pltpu.VMEM((n_blocks, N_KV, VT_ROWS, TK2), BF16)]


def _softmax_scratch(n_blocks):
    return [pltpu.VMEM((N_KV, n_blocks, TK2, GQA * TQ), F32),
            pltpu.VMEM((N_KV, VT_ROWS, GQA * TQ), F32)]


def _dsa_prompt(qi, misc, kilo, kihi, qa, ka, va, bias):
    n_b, n_s, _ = qa.shape
    n_blocks = n_s // TK2
    top_k = min(DSA_TOPK, n_s // 4)
    qblk = lambda w: pl.BlockSpec((1, TQ, w), lambda b, i: (b, i, 0))
    seq = lambda w: pl.BlockSpec((1, n_s, w), lambda b, i: (b, 0, 0))
    kv_rows = pl.BlockSpec((1, n_s * N_KV, HEAD_DIM), lambda b, i: (b, 0, 0))
    return pl.pallas_call(
        functools.partial(_dsa_kernel, n_blocks=n_blocks, top_k=top_k,
                          pos_bits=max(1, (n_s - 1).bit_length())),
        grid=(n_b, n_s // TQ),
        in_specs=[qblk(N_IDX_HEADS * IDX_DIM), qblk(LANE), seq(LANE), seq(LANE), qblk(Q_W),
                  kv_rows, kv_rows,
                  pl.BlockSpec((3, N_HEADS, TK, TQ), lambda b, i: (0, 0, 0, 0))],
        out_specs=qblk(Q_W),
        out_shape=jax.ShapeDtypeStruct((n_b, n_s, Q_W), BF16),
        scratch_shapes=_attn_scratch(n_blocks) + [
            pltpu.VMEM((n_s // TK, TK, TQ), I32),
            pltpu.VMEM((LANE, TQ), F32),
        ] + _softmax_scratch(n_blocks),
        compiler_params=_cparams(("arbitrary", "arbitrary")),
        name="dsa_prompt",
    )(qi, misc, kilo, kihi, qa, ka, va, bias)


def _roll_rows(x, r):
    if x.shape[0] % SUBLANE == 0:
        return pltpu.roll(x, r, axis=0)
    return jnp.roll(x, r, axis=0)


def _split_hi_lo(x):
    hi = x.astype(BF16)
    return hi, (x - hi.astype(F32)).astype(BF16)


def _moba_kernel(qb_ref, kb_ref, vb_ref, bias_ref, o_ref,
                 k_s, vt_s, kmean_s, sel_s, lg_s, acc_s, *, n_blocks, n_sel):
    i = pl.program_id(1)
    cur = (i * TQ) // MOBA_BLOCK
    kl = lax.broadcasted_iota(I32, (TK, TQ), 0)
    ql = lax.broadcasted_iota(I32, (TK, TQ), 1)

    @pl.when(i == 0)
    def _():
        _stage_kv(kb_ref, vb_ref, k_s, vt_s, n_blocks)
        for n in range(n_blocks):
            blk = kb_ref[0, n * MOBA_BLOCK * N_KV:(n + 1) * MOBA_BLOCK * N_KV, :]
            s8 = _fold8(blk, jnp.sum)
            kmean_s[n] = (s8[:N_KV] + s8[N_KV:]) * (1.0 / MOBA_BLOCK)

    blk_id = lax.broadcasted_iota(I32, (n_blocks, TQ), 0)
    for kh in range(N_KV):
        km_hi, km_lo = _split_hi_lo(kmean_s[:, kh, :])
        for g in range(GQA):
            h = kh * GQA + g
            qh = qb_ref[0, :, h * HEAD_DIM:(h + 1) * HEAD_DIM]
            gate = _nt(km_hi, qh) + _nt(km_lo, qh)
            gate = jnp.where(blk_id < cur, gate, NEG)
            rank = jnp.zeros((n_blocks, TQ), F32)
            for r in range(1, n_blocks):
                other = _roll_rows(gate, r)
                beats = jnp.logical_or(other > gate,
                                       jnp.logical_and(other == gate, blk_id >= r))
                rank = rank + jnp.where(beats, 1.0, 0.0)
            keep = jnp.logical_and(rank < n_sel, blk_id < cur)
            sel_s[h] = jnp.where(keep, 1.0, 0.0)

    def past_mask_of(c2, a, h):
        return jnp.broadcast_to(sel_s[h, pl.ds(c2, 1), :], (TK, TQ)) > 0.0

    def own_mask_of(c2, a, h):
        return (c2 * TK2 + a * TK + kl) <= (i * TQ + ql)

    _attend(i, cur, past_mask_of, own_mask_of, qb_ref, k_s, vt_s, bias_ref, lg_s, acc_s, o_ref)


def _moba_prompt(qb, kb, vb, bias):
    n_b, n_s, _ = qb.shape
    n_blocks = n_s // MOBA_BLOCK
    qblk = lambda w: pl.BlockSpec((1, TQ, w), lambda b, i: (b, i, 0))
    kv_rows = pl.BlockSpec((1, n_s * N_KV, HEAD_DIM), lambda b, i: (b, 0, 0))
    return pl.pallas_call(
        functools.partial(_moba_kernel, n_blocks=n_blocks, n_sel=min(MOBA_TOPK, n_blocks)),
        grid=(n_b, n_s // TQ),
        in_specs=[qblk(Q_W), kv_rows, kv_rows,
                  pl.BlockSpec((3, N_HEADS, TK, TQ), lambda b, i: (0, 0, 0, 0))],
        out_specs=qblk(Q_W),
        out_shape=jax.ShapeDtypeStruct((n_b, n_s, Q_W), BF16),
        scratch_shapes=_attn_scratch(n_blocks) + [
            pltpu.VMEM((n_blocks, N_KV, HEAD_DIM), F32),
            pltpu.VMEM((N_HEADS, n_blocks, TQ), F32),
        ] + _softmax_scratch(n_blocks),
        compiler_params=_cparams(("arbitrary", "arbitrary")),
        name="moba_prompt",
    )(qb, kb, vb, bias)


SPAN = 2048
SPAN_ROWS = SPAN * N_KV


def _page_copies(pt_ref, b, first_page, n_pages, pool_ref, buf, sem, rows):
    return [pltpu.make_async_copy(pool_ref.at[pt_ref[b, first_page + p]],
                                  buf.at[pl.ds(p * rows, rows)], sem)
            for p in range(n_pages)]


def _own_kv_lane(shape):
    row = lax.broadcasted_iota(I32, shape, 0)
    lane = lax.broadcasted_iota(I32, shape, 1)
    return (lane & (N_KV - 1)) == row // GQA


def _sidx_kernel(pt_ref, qi_ref, wi_ref, kin_ref, pool_ref, mask_o, new_o,
                 xbuf, sem, s_s, *, n_spans, n_pages, page, top_k):
    b = pl.program_id(0)
    n_b = pl.num_programs(0)
    slot = lax.rem(b, 2)

    def copies(bb, sl):
        return [pltpu.make_async_copy(pool_ref.at[pt_ref[bb, p]],
                                      xbuf.at[sl, :, pl.ds(p * page, page)], sem.at[sl])
                for p in range(n_pages)]

    @pl.when(b == 0)
    def _():
        for cp in copies(b, slot):
            cp.start()

    @pl.when(b + 1 < n_b)
    def _():
        for cp in copies(b + 1, 1 - slot):
            cp.start()

    for cp in copies(b, slot):
        cp.wait()

    q = qi_ref[0]
    w = wi_ref[0]
    for j in range(n_spans):
        x = xbuf[slot, :, j * SPAN:(j + 1) * SPAN].astype(BF16)
        s = jnp.sum(w * jnp.maximum(_dot(q, x), 0.0), axis=0, keepdims=True)
        s_s[j:j + 1, :] = _sortable(s + 0.0)
    kin = kin_ref[0].astype(BF16).astype(F32)
    s_new = jnp.sum(q.astype(F32) * kin, axis=1, keepdims=True)
    s_new = jnp.sum(w * jnp.maximum(s_new, 0.0), axis=0, keepdims=True)
    key_new = _sortable(s_new + 0.0)
    sk = s_s[...]

    def count(pred_past, pred_new):
        c = jnp.sum(jnp.where(pred_past, 1.0, 0.0), axis=1, keepdims=True)
        return jnp.sum(c, axis=0, keepdims=True) + jnp.where(pred_new, 1.0, 0.0)

    kf = float(top_k)
    zero = jnp.zeros((1, 1), I32)
    t0 = jnp.where(count(sk >= zero, key_new >= zero) >= kf, zero, INT_MIN)

    def ge(cand):
        return count(sk >= cand, key_new >= cand) >= kf

    def bit_pair_body(s, t):
        step = lax.shift_left(jnp.int32(1), 29 - 2 * s)
        c1, c2, c3 = t + step, t + 2 * step, t + 3 * step
        return jnp.where(ge(c3), c3, jnp.where(ge(c2), c2, jnp.where(ge(c1), c1, t)))
    t = lax.fori_loop(0, 15, bit_pair_body, t0)
    t = jnp.where(ge(t + 1), t + 1, t)

    need = kf - count(sk > t, key_new > t)
    eq = sk == t
    n_eq_past = count(eq, False)
    mask_o[0] = jnp.where(sk >= t, 1.0, 0.0)
    new_o[0] = jnp.broadcast_to(jnp.where(key_new >= t, 1.0, 0.0), (1, LANE))

    n_eq = n_eq_past + jnp.where(key_new == t, 1.0, 0.0)

    @pl.when(jnp.max(n_eq - need) > 0.0)
    def _():
        pos = (lax.broadcasted_iota(I32, sk.shape, 0) * SPAN
               + lax.broadcasted_iota(I32, sk.shape, 1))
        pos_bits = max(1, (n_spans * SPAN - 1).bit_length())

        def pos_body(s, lo):
            cand = lo + lax.shift_left(jnp.int32(1), pos_bits - 1 - s)
            below = count(jnp.logical_and(eq, pos < cand), False)
            return jnp.where(below < need, cand, lo)
        last = lax.fori_loop(0, pos_bits, pos_body, jnp.zeros((1, 1), I32))
        sel = jnp.logical_or(sk > t, jnp.logical_and(eq, pos <= last))
        mask_o[0] = jnp.where(sel, 1.0, 0.0)
        new_sel = jnp.logical_or(key_new > t, jnp.logical_and(key_new == t, n_eq_past < need))
        new_o[0] = jnp.broadcast_to(jnp.where(new_sel, 1.0, 0.0), (1, LANE))


def _sample_index(page_table, qi, wi, ki_new, pool_t):
    n_b, n_pages = page_table.shape
    page = pool_t.shape[2]
    n_spans = n_pages * page // SPAN
    top_k = min(DSA_TOPK, (n_pages * page + 1) // 4)
    grid_spec = pltpu.PrefetchScalarGridSpec(
        num_scalar_prefetch=1,
        grid=(n_b,),
        in_specs=[pl.BlockSpec((1, N_IDX_HEADS, IDX_DIM), lambda b, pt: (b, 0, 0)),
                  pl.BlockSpec((1, N_IDX_HEADS, 1), lambda b, pt: (b, 0, 0)),
                  pl.BlockSpec((1, 1, IDX_DIM), lambda b, pt: (b, 0, 0)),
                  pl.BlockSpec(memory_space=pl.ANY)],
        out_specs=[pl.BlockSpec((1, n_spans, SPAN), lambda b, pt: (b, 0, 0)),
                   pl.BlockSpec((1, 1, LANE), lambda b, pt: (b, 0, 0))],
        scratch_shapes=[pltpu.VMEM((2, IDX_DIM, n_pages * page), F32),
                        pltpu.SemaphoreType.DMA((2,)),
                        pltpu.VMEM((n_spans, SPAN), I32)],
    )
    return pl.pallas_call(
        functools.partial(_sidx_kernel, n_spans=n_spans, n_pages=n_pages, page=page, top_k=top_k),
        grid_spec=grid_spec,
        out_shape=[jax.ShapeDtypeStruct((n_b, n_spans, SPAN), F32),
                   jax.ShapeDtypeStruct((n_b, 1, LANE), F32)],
        compiler_params=_cparams(("arbitrary",)),
        name="sample_indexer",
    )(page_table, qi, wi, ki_new, pool_t)


def _rows_per_kv(x):
    return jnp.concatenate(
        [jnp.broadcast_to(x[:, kh * HEAD_DIM:(kh + 1) * HEAD_DIM], (GQA, HEAD_DIM))
         for kh in range(N_KV)], axis=0)


def _sdsa_kernel(pt_ref, mask_ref, new_ref, q_ref, kn_ref, vn_ref, tail_ref, far_ref, b0_ref,
                 kpool_ref, vpool_ref, o_ref, kbuf, vbuf, sem, m_s, l_s, acc_s, *, n_spans, page):
    b = pl.program_id(0)
    j = pl.program_id(1)
    n_b = pl.num_programs(0)
    pps = SPAN // page
    step = b * n_spans + j
    slot = lax.rem(step, 2)

    def copies(bb, jj, sl):
        rows = page * N_KV
        return (_page_copies(pt_ref, bb, jj * pps, pps, kpool_ref, kbuf.at[sl], sem.at[0, sl], rows)
                + _page_copies(pt_ref, bb, jj * pps, pps, vpool_ref, vbuf.at[sl], sem.at[1, sl], rows))

    @pl.when(step == 0)
    def _():
        for cp in copies(b, j, slot):
            cp.start()

    @pl.when(step + 1 < n_b * n_spans)
    def _():
        nxt = step + 1
        for cp in copies(nxt // n_spans, lax.rem(nxt, n_spans), 1 - slot):
            cp.start()

    q = q_ref[0]

    @pl.when(j == 0)
    def _():
        qf = q.astype(F32)
        kn = _rows_per_kv(kn_ref[0].astype(BF16).astype(F32))
        lg = jnp.sum(qf * kn, axis=1, keepdims=True) + b0_ref[...]
        keep = new_ref[0][:, 0:1] > 0.0
        m_s[...] = jnp.where(keep, lg, NEG)
        l_s[...] = jnp.where(keep, jnp.ones_like(lg), 0.0)
        vn = _rows_per_kv(vn_ref[0].astype(BF16).astype(F32))
        acc_s[...] = jnp.where(keep, vn, 0.0)

    for cp in copies(b, j, slot):
        cp.wait()

    kc = kbuf[slot].astype(BF16)
    vc = vbuf[slot].astype(BF16)
    bias = jnp.where(j == n_spans - 1, tail_ref[...], far_ref[...])
    mk = jnp.logical_and(mask_ref[0, pl.ds(j, 1), :] > 0.0, _own_kv_lane((N_HEADS, SPAN_ROWS)))
    lg = jnp.where(mk, _nt(q, kc) + bias, NEG)
    m_old = m_s[...]
    m_new = jnp.maximum(m_old, jnp.max(lg, axis=1, keepdims=True))
    alpha = jnp.exp(m_old - m_new)
    p = jnp.where(mk, jnp.exp(lg - m_new), 0.0)
    l_s[...] = alpha * l_s[...] + jnp.sum(p, axis=1, keepdims=True)
    acc_s[...] = alpha * acc_s[...] + _dot(p.astype(BF16), vc)
    m_s[...] = m_new

    @pl.when(j == n_spans - 1)
    def _():
        o_ref[0] = (acc_s[...] * (1.0 / l_s[...])).astype(BF16)


def _sample_dsa(page_table, mask4, new_sel, qa, ka_new, va_new, tail4, far, bias0, k_pool, v_pool):
    n_b, n_pages = page_table.shape
    page = k_pool.shape[1] // N_KV
    n_spans = n_pages * page // SPAN
    per_b = lambda shape: pl.BlockSpec((1,) + shape, lambda b, j, pt: (b, 0, 0))
    const = lambda shape: pl.BlockSpec(shape, lambda b, j, pt: (0, 0))
    grid_spec = pltpu.PrefetchScalarGridSpec(
        num_scalar_prefetch=1,
        grid=(n_b, n_spans),
        in_specs=[per_b((n_spans, SPAN_ROWS)), per_b((1, LANE)), per_b((N_HEADS, HEAD_DIM)),
                  per_b((1, KV_W)), per_b((1, KV_W)),
                  const((N_HEADS, SPAN_ROWS)), const((N_HEADS, 1)), const((N_HEADS, 1)),
                  pl.BlockSpec(memory_space=pl.ANY), pl.BlockSpec(memory_space=pl.ANY)],
        out_specs=per_b((N_HEADS, HEAD_DIM)),
        scratch_shapes=[pltpu.VMEM((2, SPAN_ROWS, HEAD_DIM), F32),
                        pltpu.VMEM((2, SPAN_ROWS, HEAD_DIM), F32),
                        pltpu.SemaphoreType.DMA((2, 2)),
                        pltpu.VMEM((N_HEADS, 1), F32), pltpu.VMEM((N_HEADS, 1), F32),
                        pltpu.VMEM((N_HEADS, HEAD_DIM), F32)],
    )
    return pl.pallas_call(
        functools.partial(_sdsa_kernel, n_spans=n_spans, page=page),
        grid_spec=grid_spec,
        out_shape=jax.ShapeDtypeStruct((n_b, N_HEADS, HEAD_DIM), BF16),
        compiler_params=_cparams(("arbitrary", "arbitrary")),
        name="sample_dsa",
    )(page_table, mask4, new_sel, qa, ka_new, va_new, tail4, far, bias0, k_pool, v_pool)


def _smoba_gate_kernel(pt_ref, q_ref, kpool_ref, sel_o, kbuf, sem, kmean_s,
                       *, n_spans, page, n_sel, n_blocks):
    b = pl.program_id(0)
    j = pl.program_id(1)
    n_b = pl.num_programs(0)
    pps = SPAN // page
    bps = SPAN // MOBA_BLOCK
    blk_rows = MOBA_BLOCK * N_KV
    step = b * n_spans + j
    slot = lax.rem(step, 2)

    def copies(bb, jj, sl):
        return _page_copies(pt_ref, bb, jj * pps, pps, kpool_ref, kbuf.at[sl], sem.at[sl],
                            page * N_KV)

    @pl.when(step == 0)
    def _():
        for cp in copies(b, j, slot):
            cp.start()

    @pl.when(step + 1 < n_b * n_spans)
    def _():
        nxt = step + 1
        for cp in copies(nxt // n_spans, lax.rem(nxt, n_spans), 1 - slot):
            cp.start()

    for cp in copies(b, j, slot):
        cp.wait()

    means = []
    for n in range(bps):
        s8 = _fold8(kbuf[slot, n * blk_rows:(n + 1) * blk_rows, :], jnp.sum)
        means.append((s8[:N_KV] + s8[N_KV:]) * (1.0 / MOBA_BLOCK))
    kmean_s[pl.ds(pl.multiple_of(j * bps * N_KV, bps * N_KV), bps * N_KV), :] = (
        jnp.concatenate(means, axis=0))

    @pl.when(j == n_spans - 1)
    def _():
        km_hi, km_lo = _split_hi_lo(kmean_s[...])
        q = q_ref[0]
        gate = _nt(q, km_hi) + _nt(q, km_lo)
        gate = jnp.where(_own_kv_lane(gate.shape), gate, -jnp.inf)
        lane = lax.broadcasted_iota(I32, gate.shape, 1).astype(F32)
        out_lane = lax.broadcasted_iota(I32, (N_HEADS, LANE), 1)
        picked = jnp.zeros((N_HEADS, LANE), F32)
        for r in range(n_sel):
            best = jnp.max(gate, axis=1, keepdims=True)
            idx = jnp.min(jnp.where(gate == best, lane, float(n_blocks * N_KV)), axis=1, keepdims=True)
            picked = jnp.where(out_lane == r, idx, picked)
            gate = jnp.where(lane == idx, -jnp.inf, gate)
        sel_o[0] = picked.astype(I32) // N_KV


def _sample_moba_gate(page_table, qb, k_pool):
    n_b, n_pages = page_table.shape
    page = k_pool.shape[1] // N_KV
    n_spans = n_pages * page // SPAN
    n_blocks = (n_pages * page + 1) // MOBA_BLOCK
    grid_spec = pltpu.PrefetchScalarGridSpec(
        num_scalar_prefetch=1,
        grid=(n_b, n_spans),
        in_specs=[pl.BlockSpec((1, N_HEADS, HEAD_DIM), lambda b, j, pt: (b, 0, 0)),
                  pl.BlockSpec(memory_space=pl.ANY)],
        out_specs=pl.BlockSpec((1, N_HEADS, LANE), lambda b, j, pt: (b, 0, 0)),
        scratch_shapes=[pltpu.VMEM((2, SPAN_ROWS, HEAD_DIM), F32), pltpu.SemaphoreType.DMA((2,)),
                        pltpu.VMEM((n_blocks * N_KV, HEAD_DIM), F32)],
    )
    return pl.pallas_call(
        functools.partial(_smoba_gate_kernel, n_spans=n_spans, page=page,
                          n_sel=min(MOBA_TOPK, n_blocks), n_blocks=n_blocks),
        grid_spec=grid_spec,
        out_shape=jax.ShapeDtypeStruct((n_b, N_HEADS, LANE), I32),
        compiler_params=_cparams(("arbitrary", "arbitrary")),
        name="sample_moba_gate",
    )(page_table, qb, k_pool)


def _smoba_attn_kernel(pt_ref, sel_ref, selv_ref, q_ref, kn_ref, vn_ref, tail_ref, far_ref, b0_ref,
                       kpool_ref, vpool_ref, o_ref, kbuf, vbuf, sem, *, n_sel, page, n_blocks):
    b = pl.program_id(0)
    n_b = pl.num_programs(0)
    ppb = MOBA_BLOCK // page
    slot = lax.rem(b, 2)
    n_keys = n_sel * MOBA_BLOCK

    def copies(bb, sl):
        out = []
        for h in range(N_HEADS):
            kh = h // GQA
            for r in range(n_sel):
                blk = sel_ref[(bb * N_HEADS + h) * n_sel + r]
                for p in range(ppb):
                    pg = pt_ref[bb, blk * ppb + p]
                    rows = pl.ds((r * ppb + p) * page, page)
                    out.append(pltpu.make_async_copy(kpool_ref.at[pg, :, kh, :],
                                                     kbuf.at[sl, h, rows], sem.at[0, sl]))
                    out.append(pltpu.make_async_copy(vpool_ref.at[pg, :, kh, :],
                                                     vbuf.at[sl, h, rows], sem.at[1, sl]))
        return out

    @pl.when(b == 0)
    def _():
        for cp in copies(b, slot):
            cp.start()

    @pl.when(b + 1 < n_b)
    def _():
        for cp in copies(b + 1, 1 - slot):
            cp.start()

    for cp in copies(b, slot):
        cp.wait()

    q = q_ref[0]
    row = lax.broadcasted_iota(I32, (N_HEADS, n_keys), 0)
    lg = jnp.zeros((N_HEADS, n_keys), F32)
    for h in range(N_HEADS):
        lg = jnp.where(row == h, _nt(q, kbuf[slot, h].astype(BF16)), lg)
    selv = selv_ref[0]
    tail = tail_ref[...]
    far = far_ref[...]
    bias = jnp.concatenate(
        [jnp.where(selv[:, r:r + 1] == n_blocks - 1, tail, far) for r in range(n_sel)], axis=1)
    lg = lg + bias
    qf = q.astype(F32)
    kn = _rows_per_kv(kn_ref[0].astype(BF16).astype(F32))
    lg_new = jnp.sum(qf * kn, axis=1, keepdims=True) + b0_ref[...]
    m = jnp.maximum(jnp.max(lg, axis=1, keepdims=True), lg_new)
    p = jnp.exp(lg - m)
    p_new = jnp.exp(lg_new - m)
    denom = jnp.sum(p, axis=1, keepdims=True) + p_new
    pb = p.astype(BF16)
    acc = p_new.astype(BF16).astype(F32) * _rows_per_kv(vn_ref[0].astype(BF16).astype(F32))
    row_o = lax.broadcasted_iota(I32, (N_HEADS, HEAD_DIM), 0)
    for h in range(N_HEADS):
        acc = acc + jnp.where(row_o == h, _dot(pb, vbuf[slot, h].astype(BF16)), 0.0)
    o_ref[0] = (acc * (1.0 / denom)).astype(BF16)


def _sample_moba_attn(page_table, sel, qb, kb_new, vb_new, tail, far, bias0, k_pool, v_pool):
    n_b, n_pages = page_table.shape
    page = k_pool.shape[1]
    n_blocks = (n_pages * page + 1) // MOBA_BLOCK
    n_sel = min(MOBA_TOPK, n_blocks)
    sel_flat = sel[:, :, :n_sel].reshape(-1)
    per_b = lambda shape: pl.BlockSpec((1,) + shape, lambda b, pt, sf: (b, 0, 0))
    const = lambda shape: pl.BlockSpec(shape, lambda b, pt, sf: (0, 0))
    grid_spec = pltpu.PrefetchScalarGridSpec(
        num_scalar_prefetch=2,
        grid=(n_b,),
        in_specs=[per_b((N_HEADS, LANE)), per_b((N_HEADS, HEAD_DIM)), per_b((1, KV_W)),
                  per_b((1, KV_W)), const((N_HEADS, MOBA_BLOCK)), const((N_HEADS, 1)),
                  const((N_HEADS, 1)),
                  pl.BlockSpec(memory_space=pl.ANY), pl.BlockSpec(memory_space=pl.ANY)],
        out_specs=per_b((N_HEADS, HEAD_DIM)),
        scratch_shapes=[pltpu.VMEM((2, N_HEADS, n_sel * MOBA_BLOCK, HEAD_DIM), F32),
                        pltpu.VMEM((2, N_HEADS, n_sel * MOBA_BLOCK, HEAD_DIM), F32),
                        pltpu.SemaphoreType.DMA((2, 2))],
    )
    return pl.pallas_call(
        functools.partial(_smoba_attn_kernel, n_sel=n_sel, page=page, n_blocks=n_blocks),
        grid_spec=grid_spec,
        out_shape=jax.ShapeDtypeStruct((n_b, N_HEADS, HEAD_DIM), BF16),
        compiler_params=_cparams(("arbitrary",)),
        name="sample_moba_attn",
    )(page_table, sel_flat, sel, qb, kb_new, vb_new, tail, far, bias0, k_pool, v_pool)


def _split_w_in(w_in):
    widths = (Q_W, KV_W, KV_W, N_IDX_HEADS * IDX_DIM, IDX_DIM, N_IDX_HEADS, Q_W, KV_W, KV_W,
              w_in.shape[0], w_in.shape[0])
    pts = [0]
    for w in widths:
        pts.append(pts[-1] + w)
    qa, ka, va, qi, ki, wi, qb, kb, vb, ga, gb = [w_in[:, pts[n]:pts[n + 1]] for n in range(11)]
    pad = jnp.zeros((w_in.shape[0], LANE - IDX_DIM - N_IDX_HEADS), w_in.dtype)
    w_a = jnp.concatenate([qa, ka, va, qi, ki, wi, pad], axis=1).astype(BF16)
    w_b = jnp.concatenate([qb, kb, vb], axis=1).astype(BF16)
    return w_a, w_b, ga.astype(BF16), gb.astype(BF16)


def kernel(x_prompt, x_sample, cache_ka, cache_va, cache_kidx, cache_kb, cache_vb, page_table,
           c_prompt, c_sample, rel_bias, w_ada, b_ada, g_norm1, w_in, g_qa, g_ka, g_kidx, g_qb,
           g_kb, w_pa, w_pb, w_out, g_norm2, w_ffn_in, w_ffn_out):
    depth = w_in.shape[0]
    n_b, n_s, d = x_prompt.shape
    n_db, n_ds, _ = x_sample.shape
    n_pool, page = cache_ka.shape[1], cache_ka.shape[2]
    past_len = page_table.shape[1] * page
    assert n_ds == 1, "the decode kernels handle one new token per sequence"
    assert n_s % MOBA_BLOCK == 0 and past_len % SPAN == 0 and SPAN % page == 0
    assert MOBA_BLOCK % page == 0 and past_len >= 4 * DSA_TOPK

    bias_mat, bias_tail = _bias_tables(rel_bias, SPAN)
    mat_a, mat_b = bias_mat[:, :N_HEADS], bias_mat[:, N_HEADS:]
    tail = bias_tail[:, 0, :]
    tail_a4 = jnp.repeat(tail[:N_HEADS], N_KV, axis=1)
    tail_b = tail[N_HEADS:, SPAN - MOBA_BLOCK:]
    far = rel_bias[N_BUCKETS - 1].reshape(-1, 1)
    bias0 = rel_bias[0].reshape(-1, 1)

    xp = x_prompt
    xs = x_sample.reshape(1, n_db, d)
    c_all = jnp.concatenate([c_prompt, c_sample], axis=0)
    rows_p, rows_s = [], []
    for l in range(depth):
        mod = _adaln(c_all, w_ada[l], b_ada[l])
        mod_p = [m.reshape(n_b, 1, d) for m in jnp.split(mod[:n_b], 6, axis=-1)]
        mod_s = [m.reshape(1, n_db, d) for m in jnp.split(mod[n_b:], 6, axis=-1)]
        w_a, w_b, w_ga, w_gb = _split_w_in(w_in[l])
        w_pa_l, w_pb_l, w_out_l = w_pa[l].astype(BF16), w_pb[l].astype(BF16), w_out[l].astype(BF16)
        w_fi, w_fo = w_ffn_in[l].astype(BF16), w_ffn_out[l].astype(BF16)
        g_kidx_pad = jnp.pad(g_kidx[l], (0, LANE - IDX_DIM)).reshape(1, LANE)

        def project(x, mods):
            h, *dsa_side = _proj_a(x, g_norm1[l], mods[0], mods[1], w_a, g_qa[l], g_ka[l], g_kidx_pad)
            return h, dsa_side, _proj_b(h, w_b, g_qb[l], g_kb[l])

        def finish(x, h, oa, ob, mods):
            m = _gate_merge(h, oa, ob, w_ga, w_gb, w_pa_l, w_pb_l)
            x2, h2 = _out_proj(m, x, w_out_l, mods[2], g_norm2[l], mods[3], mods[4])
            return _ffn(h2, x2, w_fi, w_fo, mods[5])

        h, (qa, ka, va, qi, ki, misc, kilo, kihi), (qb, kb, vb) = project(xp, mod_p)
        oa = _dsa_prompt(qi, misc, kilo, kihi, qa, ka, va, mat_a)
        ob = _moba_prompt(qb, kb, vb, mat_b)
        xp = finish(xp, h, oa, ob, mod_p)
        rows_p.append((ka.reshape(n_b, n_s, N_KV, HEAD_DIM), va.reshape(n_b, n_s, N_KV, HEAD_DIM),
                       ki, kb.reshape(n_b, n_s, N_KV, HEAD_DIM), vb.reshape(n_b, n_s, N_KV, HEAD_DIM)))

        h, (qa, ka, va, qi, ki, misc, kilo, kihi), (qb, kb, vb) = project(xs, mod_s)
        tok = lambda t: t.reshape(n_db, 1, -1)
        heads = lambda t: t.reshape(n_db, N_HEADS, HEAD_DIM)
        rows_view = lambda c: c[l].reshape(n_pool, page * N_KV, HEAD_DIM)
        wi = misc[0, :, _WI_LANE:_WI_LANE + N_IDX_HEADS].reshape(n_db, N_IDX_HEADS, 1)
        mask, new_sel = _sample_index(page_table, qi.reshape(n_db, N_IDX_HEADS, IDX_DIM), wi,
                                      tok(ki[0]), jnp.swapaxes(cache_kidx[l], 1, 2))
        oa = _sample_dsa(page_table, jnp.repeat(mask, N_KV, axis=2), new_sel, heads(qa[0]),
                         tok(ka[0]), tok(va[0]), tail_a4, far[:N_HEADS], bias0[:N_HEADS],
                         rows_view(cache_ka), rows_view(cache_va))
        sel = _sample_moba_gate(page_table, heads(qb[0]), rows_view(cache_kb))
        ob = _sample_moba_attn(page_table, sel, heads(qb[0]), tok(kb[0]), tok(vb[0]),
                               tail_b, far[N_HEADS:], bias0[N_HEADS:], cache_kb[l], cache_vb[l])
        xs = finish(xs, h, oa.reshape(1, n_db, Q_W), ob.reshape(1, n_db, Q_W), mod_s)
        kv4 = lambda t: t.reshape(n_db, 1, N_KV, HEAD_DIM)
        rows_s.append((kv4(ka[0]), kv4(va[0]), ki.reshape(n_db, 1, IDX_DIM), kv4(kb[0]), kv4(vb[0])))

    outs_p = [jnp.stack(t) for t in zip(*rows_p)]
    outs_s = [jnp.stack(t) for t in zip(*rows_s)]
    return (xp, xs.reshape(n_db, n_ds, d), *outs_p, *outs_s)
```

```python
import functools
import math

import jax
import jax.numpy as jnp
from jax import lax
from jax.experimental import pallas as pl
from jax.experimental.pallas import tpu as pltpu

F32 = jnp.float32
BF16 = jnp.bfloat16
I32 = jnp.int32

HEAD_DIM = 128
N_HEADS = 8
N_KV = 4
GQA = N_HEADS // N_KV
N_IDX_HEADS = 16
IDX_DIM = 64
IDX_WEIGHT_SCALE = (N_IDX_HEADS * IDX_DIM) ** -0.5
DSA_TOPK = 256
MOBA_BLOCK = 256
MOBA_TOPK = 3
N_BUCKETS = 32
MAX_DISTANCE = 128
EPS = 1e-6
NEG = -1e30
ATTN_SCALE = HEAD_DIM ** -0.5

SUBLANE = 8
LANE = 128
INT_MIN = -(2 ** 31)
VMEM_LIMIT_V7X = 56 * 1024 * 1024

TQ = 128
TK = 128
TK2 = MOBA_BLOCK
CPP = TK2 // TK
KV_W = N_KV * HEAD_DIM
Q_W = N_HEADS * HEAD_DIM


def _cparams(sem):
    return pltpu.CompilerParams(dimension_semantics=sem, vmem_limit_bytes=VMEM_LIMIT_V7X)


def _nt(a, b):
    return lax.dot_general(a, b, (((1,), (1,)), ((), ())), preferred_element_type=F32)


def _dot(a, b):
    return jnp.dot(a, b, preferred_element_type=F32)


def _bucket_lower_bounds():
    max_exact = N_BUCKETS // 2

    def bucket(n):
        if n < max_exact:
            return n
        large = max_exact + int(math.log(n / max_exact) / math.log(MAX_DISTANCE / max_exact)
                                * (N_BUCKETS - max_exact))
        return min(large, N_BUCKETS - 1)

    lows = [None] * N_BUCKETS
    for n in range(0, 4 * MAX_DISTANCE):
        b = bucket(n)
        if lows[b] is None:
            lows[b] = n
    return lows


_BUCKET_LOW = _bucket_lower_bounds()


def _bias_of_distance(dist, table_ref, h):
    out = jnp.full(dist.shape, table_ref[0, h], F32)
    for b in range(1, N_BUCKETS):
        if _BUCKET_LOW[b] is None:
            continue
        out = jnp.where(dist >= _BUCKET_LOW[b], table_ref[b, h], out)
    return out


def _bias_kernel(table_ref, mat_ref, tail_ref, *, tail_w):
    h = pl.program_id(0)
    kl = lax.broadcasted_iota(I32, (TK, TQ), 0)
    ql = lax.broadcasted_iota(I32, (TK, TQ), 1)
    for dd in range(3):
        mat_ref[dd, 0] = _bias_of_distance(ql - kl + dd * TQ, table_ref, h)
    lane = lax.broadcasted_iota(I32, (1, tail_w), 1)
    tail_ref[0] = _bias_of_distance(tail_w - lane, table_ref, h)


def _bias_tables(rel_bias, tail_w):
    n_h = rel_bias.shape[1]
    return pl.pallas_call(
        functools.partial(_bias_kernel, tail_w=tail_w),
        grid=(n_h,),
        in_specs=[pl.BlockSpec(memory_space=pltpu.SMEM)],
        out_specs=[pl.BlockSpec((3, 1, TK, TQ), lambda h: (0, h, 0, 0)),
                   pl.BlockSpec((1, 1, tail_w), lambda h: (h, 0, 0))],
        out_shape=[jax.ShapeDtypeStruct((3, n_h, TK, TQ), F32),
                   jax.ShapeDtypeStruct((n_h, 1, tail_w), F32)],
        compiler_params=_cparams(("arbitrary",)),
        name="bias_tables",
    )(rel_bias)


def _ada_kernel(c_ref, w_ref, b_ref, o_ref):
    c = c_ref[...]
    s = (c / (1.0 + jnp.exp(-c))).astype(BF16)
    o_ref[...] = _dot(s, w_ref[...].astype(BF16)) + b_ref[...]


def _adaln(c, w, b):
    n, d = c.shape
    tn = min(1024, d)
    return pl.pallas_call(
        _ada_kernel,
        grid=(w.shape[1] // tn,),
        in_specs=[pl.BlockSpec((n, d), lambda j: (0, 0)),
                  pl.BlockSpec((d, tn), lambda j: (0, j)),
                  pl.BlockSpec((1, tn), lambda j: (0, j))],
        out_specs=pl.BlockSpec((n, tn), lambda j: (0, j)),
        out_shape=jax.ShapeDtypeStruct((n, w.shape[1]), F32),
        compiler_params=_cparams(("arbitrary",)),
        name="adaln",
    )(c, w, b.reshape(1, -1))


def _row_tile(r):
    return min(r, 512)


def _tok_spec(tr, w):
    return pl.BlockSpec((1, tr, w), lambda g, r: (g, r, 0))


def _mod_spec(mod, tr):
    if mod.shape[1] == 1:
        return pl.BlockSpec((1, 1, mod.shape[2]), lambda g, r: (g, 0, 0))
    return pl.BlockSpec((1, tr, mod.shape[2]), lambda g, r: (g, r, 0))


def _const_spec(shape):
    nd = len(shape)
    return pl.BlockSpec(shape, lambda g, r: (0,) * nd, pipeline_mode=pl.Buffered(1))


def _rms(x, g):
    return x * lax.rsqrt(jnp.mean(x * x, axis=-1, keepdims=True) + EPS) * g


def _query_heads_store(z, g, o_ref):
    for hh in range(N_HEADS):
        cols = slice(hh * HEAD_DIM, (hh + 1) * HEAD_DIM)
        o_ref[0, :, cols] = (_rms(z[:, cols], g) * ATTN_SCALE).astype(BF16)


def _kv_rows_store(z, o_ref, g=None):
    for kh in range(N_KV):
        v = z[:, kh * HEAD_DIM:(kh + 1) * HEAD_DIM]
        o_ref[0, pl.ds(kh, z.shape[0], stride=N_KV), :] = v if g is None else _rms(v, g)


def _kv_rows_spec(tr):
    return pl.BlockSpec((1, tr * N_KV, HEAD_DIM), lambda g, r: (g, r, 0))


_A_QA = (0, Q_W)
_A_KA = (_A_QA[1], _A_QA[1] + KV_W)
_A_VA = (_A_KA[1], _A_KA[1] + KV_W)
_A_QI = (_A_VA[1], _A_VA[1] + N_IDX_HEADS * IDX_DIM)
_A_MISC = (_A_QI[1], _A_QI[1] + LANE)
_WI_LANE = IDX_DIM


def _proj_a_kernel(x_ref, g1_ref, sh_ref, sc_ref, w_ref, gq_ref, gk_ref, gki_ref,
                   h_o, qa_o, ka_o, va_o, qi_o, ki_o, misc_o, kilo_o, kihi_o):
    h = (_rms(x_ref[0], g1_ref[...]) * (1.0 + sc_ref[0]) + sh_ref[0]).astype(BF16)
    h_o[0] = h
    _query_heads_store(_dot(h, w_ref[:, _A_QA[0]:_A_QA[1]]), gq_ref[...], qa_o)
    _kv_rows_store(_dot(h, w_ref[:, _A_KA[0]:_A_KA[1]]), ka_o, gk_ref[...])
    _kv_rows_store(_dot(h, w_ref[:, _A_VA[0]:_A_VA[1]]), va_o)
    qi_o[0] = _dot(h, w_ref[:, _A_QI[0]:_A_QI[1]]).astype(BF16)
    z = _dot(h, w_ref[:, _A_MISC[0]:_A_MISC[1]])
    is_ki = lax.broadcasted_iota(I32, z.shape, 1) < IDX_DIM
    ssq = jnp.sum(jnp.where(is_ki, z * z, 0.0), axis=-1, keepdims=True) * (1.0 / IDX_DIM)
    kin = jnp.where(is_ki, z * lax.rsqrt(ssq + EPS) * gki_ref[...], 0.0)
    ki_o[0] = kin[:, :IDX_DIM]
    misc_o[0] = jnp.where(is_ki, kin, z * IDX_WEIGHT_SCALE)
    kilo_o[0] = kin.astype(BF16)
    kihi_o[0] = pltpu.roll(kin, IDX_DIM, axis=1).astype(BF16)


def _proj_a(x, g1, sh, sc, w_a, g_qa, g_ka, g_kidx_pad):
    n_g, n_r, d = x.shape
    tr = _row_tile(n_r)
    widths = [(d, BF16), (Q_W, BF16), None, None, (N_IDX_HEADS * IDX_DIM, BF16),
              (IDX_DIM, F32), (LANE, F32), (LANE, BF16), (LANE, BF16)]
    kv_rows = jax.ShapeDtypeStruct((n_g, n_r * N_KV, HEAD_DIM), F32)
    return pl.pallas_call(
        _proj_a_kernel,
        grid=(n_g, n_r // tr),
        in_specs=[_tok_spec(tr, d), _const_spec((1, d)), _mod_spec(sh, tr), _mod_spec(sc, tr),
                  _const_spec(w_a.shape), _const_spec((1, HEAD_DIM)),
                  _const_spec((1, HEAD_DIM)), _const_spec((1, LANE))],
        out_specs=[_kv_rows_spec(tr) if w is None else _tok_spec(tr, w[0]) for w in widths],
        out_shape=[kv_rows if w is None else jax.ShapeDtypeStruct((n_g, n_r, w[0]), w[1])
                   for w in widths],
        compiler_params=_cparams(("arbitrary", "arbitrary")),
        name="proj_dsa",
    )(x, g1.reshape(1, d), sh, sc, w_a, g_qa.reshape(1, -1), g_ka.reshape(1, -1), g_kidx_pad)


def _proj_b_kernel(h_ref, w_ref, gq_ref, gk_ref, qb_o, kb_o, vb_o):
    h = h_ref[0]
    _query_heads_store(_dot(h, w_ref[:, 0:Q_W]), gq_ref[...], qb_o)
    _kv_rows_store(_dot(h, w_ref[:, Q_W:Q_W + KV_W]), kb_o, gk_ref[...])
    _kv_rows_store(_dot(h, w_ref[:, Q_W + KV_W:Q_W + 2 * KV_W]), vb_o)


def _proj_b(h, w_b, g_qb, g_kb):
    n_g, n_r, d = h.shape
    tr = _row_tile(n_r)
    kv_rows = jax.ShapeDtypeStruct((n_g, n_r * N_KV, HEAD_DIM), F32)
    return pl.pallas_call(
        _proj_b_kernel,
        grid=(n_g, n_r // tr),
        in_specs=[_tok_spec(tr, d), _const_spec(w_b.shape), _const_spec((1, HEAD_DIM)),
                  _const_spec((1, HEAD_DIM))],
        out_specs=[_tok_spec(tr, Q_W), _kv_rows_spec(tr), _kv_rows_spec(tr)],
        out_shape=[jax.ShapeDtypeStruct((n_g, n_r, Q_W), BF16), kv_rows, kv_rows],
        compiler_params=_cparams(("arbitrary", "arbitrary")),
        name="proj_moba",
    )(h, w_b, g_qb.reshape(1, -1), g_kb.reshape(1, -1))


def _sigmoid(x):
    return 1.0 / (1.0 + jnp.exp(-x))


def _gate_merge_kernel(h_ref, oa_ref, ob_ref, wga_ref, wgb_ref, wpa_ref, wpb_ref, m_o):
    h = h_ref[0]
    m = (_sigmoid(_dot(h, wga_ref[...])) * _dot(oa_ref[0], wpa_ref[...])
         + _sigmoid(_dot(h, wgb_ref[...])) * _dot(ob_ref[0], wpb_ref[...]))
    m_o[0] = m.astype(BF16)


def _gate_merge(h, oa, ob, w_ga, w_gb, w_pa, w_pb):
    n_g, n_r, d = h.shape
    tr = _row_tile(n_r)
    tn = d // 2
    tok = lambda w: pl.BlockSpec((1, tr, w), lambda c, g, r: (g, r, 0))
    col = lambda k: pl.BlockSpec((k, tn), lambda c, g, r: (0, c))
    return pl.pallas_call(
        _gate_merge_kernel,
        grid=(d // tn, n_g, n_r // tr),
        in_specs=[tok(d), tok(Q_W), tok(Q_W), col(d), col(d), col(Q_W), col(Q_W)],
        out_specs=pl.BlockSpec((1, tr, tn), lambda c, g, r: (g, r, c)),
        out_shape=jax.ShapeDtypeStruct((n_g, n_r, d), BF16),
        compiler_params=_cparams(("arbitrary", "arbitrary", "arbitrary")),
        name="gate_merge",
    )(h, oa, ob, w_ga, w_gb, w_pa, w_pb)


def _out_proj_kernel(m_ref, x_ref, w_ref, gt_ref, g2_ref, sh_ref, sc_ref, x2_o, h2_o):
    x2 = x_ref[0] + gt_ref[0] * _dot(m_ref[0], w_ref[...])
    x2_o[0] = x2
    h2_o[0] = (_rms(x2, g2_ref[...]) * (1.0 + sc_ref[0]) + sh_ref[0]).astype(BF16)


def _out_proj(m, x, w_out, gt, g2, sh, sc):
    n_g, n_r, d = x.shape
    tr = _row_tile(n_r)
    return pl.pallas_call(
        _out_proj_kernel,
        grid=(n_g, n_r // tr),
        in_specs=[_tok_spec(tr, d), _tok_spec(tr, d), _const_spec(w_out.shape), _mod_spec(gt, tr),
                  _const_spec((1, d)), _mod_spec(sh, tr), _mod_spec(sc, tr)],
        out_specs=[_tok_spec(tr, d), _tok_spec(tr, d)],
        out_shape=[jax.ShapeDtypeStruct((n_g, n_r, d), F32),
                   jax.ShapeDtypeStruct((n_g, n_r, d), BF16)],
        compiler_params=_cparams(("arbitrary", "arbitrary")),
        name="out_proj",
    )(m, x, w_out, gt, g2.reshape(1, d), sh, sc)


def _ffn_kernel(h_ref, x_ref, wa_ref, wu_ref, wo_ref, gt_ref, y_o, acc_s):
    j = pl.program_id(2)

    @pl.when(j == 0)
    def _():
        acc_s[...] = jnp.zeros_like(acc_s)

    h = h_ref[0]
    a = _dot(h, wa_ref[...])
    u = _dot(h, wu_ref[...])
    act = (a * _sigmoid(a) * u).astype(BF16)
    acc_s[...] += _dot(act, wo_ref[...])

    @pl.when(j == pl.num_programs(2) - 1)
    def _():
        y_o[0] = x_ref[0] + gt_ref[0] * acc_s[...]


def _ffn(h2, x2, w_in, w_out, gt):
    n_g, n_r, d = x2.shape
    d_ff = w_out.shape[0]
    tr = _row_tile(n_r)
    tf = 512 if d_ff % 512 == 0 else 256
    n_f = d_ff // tf
    tok = lambda w: pl.BlockSpec((1, tr, w), lambda g, r, j: (g, r, 0))
    if gt.shape[1] == 1:
        gt_spec = pl.BlockSpec((1, 1, d), lambda g, r, j: (g, 0, 0))
    else:
        gt_spec = pl.BlockSpec((1, tr, d), lambda g, r, j: (g, r, 0))
    return pl.pallas_call(
        _ffn_kernel,
        grid=(n_g, n_r // tr, n_f),
        in_specs=[tok(d), tok(d),
                  pl.BlockSpec((d, tf), lambda g, r, j: (0, j)),
                  pl.BlockSpec((d, tf), lambda g, r, j: (0, j + n_f)),
                  pl.BlockSpec((tf, d), lambda g, r, j: (j, 0)),
                  gt_spec],
        out_specs=tok(d),
        out_shape=jax.ShapeDtypeStruct((n_g, n_r, d), F32),
        scratch_shapes=[pltpu.VMEM((tr, d), F32)],
        compiler_params=_cparams(("arbitrary", "arbitrary", "arbitrary")),
        name="swiglu_ffn",
    )(h2, x2, w_in, w_in, w_out, gt)


def _sortable(x):
    bits = pltpu.bitcast(x, I32)
    return bits ^ (lax.shift_right_arithmetic(bits, 31) & 0x7FFFFFFF)


ONES_ROWS = 16
VT_ROWS = HEAD_DIM + ONES_ROWS


def _stage_kv(k_ref, v_ref, k_s, vt_s, n_blocks):
    def body(c, carry):
        rows = pl.ds(pl.multiple_of(c * TK2, TK2), TK2)
        for kh in range(N_KV):
            cols = slice(kh * HEAD_DIM, (kh + 1) * HEAD_DIM)
            src = pl.ds(c * (TK2 * N_KV) + kh, TK2, stride=N_KV)
            k_s[rows, cols] = k_ref[0, src, :].astype(BF16)
            vt_s[c, kh, :HEAD_DIM, :] = v_ref[0, src, :].T.astype(BF16)
            vt_s[c, kh, HEAD_DIM:, :] = jnp.ones((ONES_ROWS, TK2), BF16)
        return carry
    lax.fori_loop(0, n_blocks, body, 0)


def _fold8(x, op):
    r, w = x.shape
    return op(x.reshape(r // SUBLANE, SUBLANE, w), axis=0)


def _loop_pairs(n, body, init):
    carry = lax.fori_loop(0, n // 2, lambda j, c: body(2 * j + 1, body(2 * j, c)), init)
    return lax.cond(n % 2 == 1, lambda c: body(n - 1, c), lambda c: c, carry)


def _loop_quads(n, body, init):
    def quad(j, c):
        for r in range(4):
            c = body(4 * j + r, c)
        return c
    carry = lax.fori_loop(0, n // 4, quad, init)
    base = (n // 4) * 4
    carry = lax.cond(n % 4 >= 2, lambda c: body(base + 1, body(base, c)), lambda c: c, carry)
    return lax.cond(n % 2 == 1, lambda c: body(n - 1, c), lambda c: c, carry)


def _attend(i, n_past, mask_of, own_mask_of, q_ref, k_s, vt_s, bias_ref, lg_s, acc_s, o_ref):
    n_pc = n_past if own_mask_of is None else n_past + 1

    def logits_body(c2, mx, mask_of=mask_of):
        rows = pl.ds(pl.multiple_of(c2 * TK2, TK2), TK2)
        out = []
        for kh in range(N_KV):
            q2 = jnp.concatenate(
                [q_ref[0, :, (kh * GQA + g) * HEAD_DIM:(kh * GQA + g + 1) * HEAD_DIM]
                 for g in range(GQA)], axis=0)
            lg = _nt(k_s[rows, kh * HEAD_DIM:(kh + 1) * HEAD_DIM], q2)
            best = mx[kh]
            for a in range(CPP):
                dd = jnp.clip(i - (c2 * CPP + a), 0, 2)
                folded = []
                for g in range(GQA):
                    h = kh * GQA + g
                    t = lg[a * TK:(a + 1) * TK, g * TQ:(g + 1) * TQ] + bias_ref[dd, h]
                    t = jnp.where(mask_of(c2, a, h), t, NEG)
                    lg_s[kh, c2, a * TK:(a + 1) * TK, g * TQ:(g + 1) * TQ] = t
                    folded.append(_fold8(t, jnp.max))
                best = jnp.maximum(best, jnp.concatenate(folded, axis=1))
            out.append(best)
        return tuple(out)

    mx0 = tuple(jnp.full((SUBLANE, GQA * TQ), NEG, F32) for _ in range(N_KV))
    mx = _loop_quads(n_past, logits_body, mx0)
    if own_mask_of is not None:
        mx = logits_body(n_past, mx, own_mask_of)
    m = [jnp.max(x, axis=0, keepdims=True) for x in mx]

    acc_s[...] = jnp.zeros(acc_s.shape, F32)

    def pv_body(c2, carry):
        for kh in range(N_KV):
            p = jnp.exp(lg_s[kh, c2] - m[kh])
            acc_s[kh] += _dot(vt_s[c2, kh], p.astype(BF16))
        return carry

    _loop_quads(n_pc, pv_body, 0)
    for kh in range(N_KV):
        o2 = acc_s[kh, :HEAD_DIM, :] * (1.0 / acc_s[kh, HEAD_DIM:HEAD_DIM + 1, :])
        for g in range(GQA):
            h = kh * GQA + g
            o_ref[0, :, h * HEAD_DIM:(h + 1) * HEAD_DIM] = o2[:, g * TQ:(g + 1) * TQ].T.astype(BF16)


def _dsa_kernel(qi_ref, misc_ref, kilo_ref, kihi_ref, qa_ref, ka_ref, va_ref, bias_ref, o_ref,
                k_s, vt_s, sk_s, w_s, lg_s, acc_s, *, n_blocks, top_k, pos_bits):
    i = pl.program_id(1)
    n_pc = i // CPP + 1
    kl = lax.broadcasted_iota(I32, (TK, TQ), 0)
    ql = lax.broadcasted_iota(I32, (TK, TQ), 1)
    qpos = i * TQ + lax.broadcasted_iota(I32, (1, TQ), 1)

    @pl.when(i == 0)
    def _():
        _stage_kv(ka_ref, va_ref, k_s, vt_s, n_blocks)

    w_s[...] = misc_ref[0].T

    def score_body(c2, carry):
        rows = pl.ds(pl.multiple_of(c2 * TK2, TK2), TK2)
        k2 = jnp.concatenate([kilo_ref[0, rows, :], kihi_ref[0, rows, :]], axis=0)
        acc = [jnp.zeros((TK, TQ), F32) for _ in range(CPP)]
        for p2 in range(N_IDX_HEADS // 4):
            qp = jnp.concatenate([qi_ref[0, :, (2 * p2 + pair) * LANE:(2 * p2 + pair + 1) * LANE]
                                  for pair in range(2)], axis=0)
            s = jnp.maximum(_nt(k2, qp), 0.0)
            for half in range(2):
                for pair in range(2):
                    r = _WI_LANE + 4 * p2 + 2 * pair + half
                    w = w_s[r:r + 1, :]
                    for a in range(CPP):
                        r0 = half * TK2 + a * TK
                        acc[a] = acc[a] + w * s[r0:r0 + TK, pair * TQ:(pair + 1) * TQ]
        for a in range(CPP):
            c = c2 * CPP + a
            admissible = (c * TK + kl) <= (i * TQ + ql)
            sk_s[c] = jnp.where(admissible, _sortable(acc[a]), INT_MIN)
        return carry
    _loop_pairs(n_pc, score_body, 0)

    def count(pred):
        def body(c2, acc):
            for a in range(CPP):
                c = c2 * CPP + a
                acc = acc + _fold8(jnp.where(pred(c, sk_s[c]), 1.0, 0.0), jnp.sum)
            return acc
        acc = _loop_quads(n_pc, body, jnp.zeros((SUBLANE, TQ), F32))
        return jnp.sum(acc, axis=0, keepdims=True)

    kf = float(top_k)
    few = qpos + 1 <= top_k

    def threshold():
        zero = jnp.zeros((1, TQ), I32)
        t0 = jnp.where(count(lambda c, s: s >= zero) >= kf, zero, INT_MIN)

        def bit_body(s, t):
            cand = t + lax.shift_left(jnp.int32(1), 30 - s)
            return jnp.where(count(lambda c, sk: sk >= cand) >= kf, cand, t)
        return lax.fori_loop(0, 31, bit_body, t0)

    t = lax.cond((i + 1) * TQ > top_k, threshold, lambda: jnp.full((1, TQ), INT_MIN + 1, I32))
    t = jnp.where(few, INT_MIN + 1, t)

    @pl.when((i + 1) * TQ > top_k)
    def _():
        need = kf - count(lambda c, sk: sk > t)
        n_eq = count(lambda c, sk: sk == t)
        tied = jnp.max(jnp.where(jnp.logical_and(n_eq > need, jnp.logical_not(few)), 1.0, 0.0))

        @pl.when(tied > 0.0)
        def _():
            def pos_body(s, lo):
                cand = lo + lax.shift_left(jnp.int32(1), pos_bits - 1 - s)
                below = count(lambda c, sk: jnp.logical_and(sk == t, c * TK + kl < cand))
                return jnp.where(below < need, cand, lo)
            last = lax.fori_loop(0, pos_bits, pos_body, jnp.zeros((1, TQ), I32))

            def drop_body(c, carry):
                sk = sk_s[c]
                drop = jnp.logical_and(sk == t, c * TK + kl > last)
                sk_s[c] = jnp.where(drop, INT_MIN, sk)
                return carry
            lax.fori_loop(0, i + 1, drop_body, 0)

    _attend(i, n_pc, lambda c2, a, h: sk_s[c2 * CPP + a] >= t, None, qa_ref, k_s, vt_s, bias_ref,
            lg_s, acc_s, o_ref)


def _attn_scratch(n_blocks):
    return [pltpu.VMEM((n_blocks * TK2, KV_W), BF16),
            pltpu.VMEM((n_blocks, N_KV, VT_ROWS, TK2), BF16)]


def _softmax_scratch(n_blocks):
    return [pltpu.VMEM((N_KV, n_blocks, TK2, GQA * TQ), F32),
            pltpu.VMEM((N_KV, VT_ROWS, GQA * TQ), F32)]


def _dsa_prompt(qi, misc, kilo, kihi, qa, ka, va, bias):
    n_b, n_s, _ = qa.shape
    n_blocks = n_s // TK2
    top_k = min(DSA_TOPK, n_s // 4)
    qblk = lambda w: pl.BlockSpec((1, TQ, w), lambda b, i: (b, i, 0))
    seq = lambda w: pl.BlockSpec((1, n_s, w), lambda b, i: (b, 0, 0))
    kv_rows = pl.BlockSpec((1, n_s * N_KV, HEAD_DIM), lambda b, i: (b, 0, 0))
    return pl.pallas_call(
        functools.partial(_dsa_kernel, n_blocks=n_blocks, top_k=top_k,
                          pos_bits=max(1, (n_s - 1).bit_length())),
        grid=(n_b, n_s // TQ),
        in_specs=[qblk(N_IDX_HEADS * IDX_DIM), qblk(LANE), seq(LANE), seq(LANE), qblk(Q_W),
                  kv_rows, kv_rows,
                  pl.BlockSpec((3, N_HEADS, TK, TQ), lambda b, i: (0, 0, 0, 0))],
        out_specs=qblk(Q_W),
        out_shape=jax.ShapeDtypeStruct((n_b, n_s, Q_W), BF16),
        scratch_shapes=_attn_scratch(n_blocks) + [
            pltpu.VMEM((n_s // TK, TK, TQ), I32),
            pltpu.VMEM((LANE, TQ), F32),
        ] + _softmax_scratch(n_blocks),
        compiler_params=_cparams(("arbitrary", "arbitrary")),
        name="dsa_prompt",
    )(qi, misc, kilo, kihi, qa, ka, va, bias)


def _roll_rows(x, r):
    if x.shape[0] % SUBLANE == 0:
        return pltpu.roll(x, r, axis=0)
    return jnp.roll(x, r, axis=0)


def _split_hi_lo(x):
    hi = x.astype(BF16)
    return hi, (x - hi.astype(F32)).astype(BF16)


def _moba_kernel(qb_ref, kb_ref, vb_ref, bias_ref, o_ref,
                 k_s, vt_s, kmean_s, sel_s, lg_s, acc_s, *, n_blocks, n_sel):
    i = pl.program_id(1)
    cur = (i * TQ) // MOBA_BLOCK
    kl = lax.broadcasted_iota(I32, (TK, TQ), 0)
    ql = lax.broadcasted_iota(I32, (TK, TQ), 1)

    @pl.when(i == 0)
    def _():
        _stage_kv(kb_ref, vb_ref, k_s, vt_s, n_blocks)
        for n in range(n_blocks):
            blk = kb_ref[0, n * MOBA_BLOCK * N_KV:(n + 1) * MOBA_BLOCK * N_KV, :]
            s8 = _fold8(blk, jnp.sum)
            kmean_s[n] = (s8[:N_KV] + s8[N_KV:]) * (1.0 / MOBA_BLOCK)

    blk_id = lax.broadcasted_iota(I32, (n_blocks, TQ), 0)
    for kh in range(N_KV):
        km_hi, km_lo = _split_hi_lo(kmean_s[:, kh, :])
        for g in range(GQA):
            h = kh * GQA + g
            qh = qb_ref[0, :, h * HEAD_DIM:(h + 1) * HEAD_DIM]
            gate = _nt(km_hi, qh) + _nt(km_lo, qh)
            gate = jnp.where(blk_id < cur, gate, NEG)
            rank = jnp.zeros((n_blocks, TQ), F32)
            for r in range(1, n_blocks):
                other = _roll_rows(gate, r)
                beats = jnp.logical_or(other > gate,
                                       jnp.logical_and(other == gate, blk_id >= r))
                rank = rank + jnp.where(beats, 1.0, 0.0)
            keep = jnp.logical_and(rank < n_sel, blk_id < cur)
            sel_s[h] = jnp.where(keep, 1.0, 0.0)

    def past_mask_of(c2, a, h):
        return jnp.broadcast_to(sel_s[h, pl.ds(c2, 1), :], (TK, TQ)) > 0.0

    def own_mask_of(c2, a, h):
        return (c2 * TK2 + a * TK + kl) <= (i * TQ + ql)

    _attend(i, cur, past_mask_of, own_mask_of, qb_ref, k_s, vt_s, bias_ref, lg_s, acc_s, o_ref)


def _moba_prompt(qb, kb, vb, bias):
    n_b, n_s, _ = qb.shape
    n_blocks = n_s // MOBA_BLOCK
    qblk = lambda w: pl.BlockSpec((1, TQ, w), lambda b, i: (b, i, 0))
    kv_rows = pl.BlockSpec((1, n_s * N_KV, HEAD_DIM), lambda b, i: (b, 0, 0))
    return pl.pallas_call(
        functools.partial(_moba_kernel, n_blocks=n_blocks, n_sel=min(MOBA_TOPK, n_blocks)),
        grid=(n_b, n_s // TQ),
        in_specs=[qblk(Q_W), kv_rows, kv_rows,
                  pl.BlockSpec((3, N_HEADS, TK, TQ), lambda b, i: (0, 0, 0, 0))],
        out_specs=qblk(Q_W),
        out_shape=jax.ShapeDtypeStruct((n_b, n_s, Q_W), BF16),
        scratch_shapes=_attn_scratch(n_blocks) + [
            pltpu.VMEM((n_blocks, N_KV, HEAD_DIM), F32),
            pltpu.VMEM((N_HEADS, n_blocks, TQ), F32),
        ] + _softmax_scratch(n_blocks),
        compiler_params=_cparams(("arbitrary", "arbitrary")),
        name="moba_prompt",
    )(qb, kb, vb, bias)


SPAN = 2048
SPAN_ROWS = SPAN * N_KV


def _page_copies(pt_ref, b, first_page, n_pages, pool_ref, buf, sem, rows):
    return [pltpu.make_async_copy(pool_ref.at[pt_ref[b, first_page + p]],
                                  buf.at[pl.ds(p * rows, rows)], sem)
            for p in range(n_pages)]


def _own_kv_lane(shape):
    row = lax.broadcasted_iota(I32, shape, 0)
    lane = lax.broadcasted_iota(I32, shape, 1)
    return (lane & (N_KV - 1)) == row // GQA


def _sidx_kernel(pt_ref, qi_ref, wi_ref, kin_ref, pool_ref, mask_o, new_o,
                 xbuf, sem, s_s, *, n_spans, n_pages, page, top_k):
    b = pl.program_id(0)
    n_b = pl.num_programs(0)
    slot = lax.rem(b, 2)

    def copies(bb, sl):
        return [pltpu.make_async_copy(pool_ref.at[pt_ref[bb, p]],
                                      xbuf.at[sl, :, pl.ds(p * page, page)], sem.at[sl])
                for p in range(n_pages)]

    @pl.when(b == 0)
    def _():
        for cp in copies(b, slot):
            cp.start()

    @pl.when(b + 1 < n_b)
    def _():
        for cp in copies(b + 1, 1 - slot):
            cp.start()

    for cp in copies(b, slot):
        cp.wait()

    q = qi_ref[0]
    w = wi_ref[0]
    for j in range(n_spans):
        x = xbuf[slot, :, j * SPAN:(j + 1) * SPAN].astype(BF16)
        s = jnp.sum(w * jnp.maximum(_dot(q, x), 0.0), axis=0, keepdims=True)
        s_s[j:j + 1, :] = _sortable(s + 0.0)
    kin = kin_ref[0].astype(BF16).astype(F32)
    s_new = jnp.sum(q.astype(F32) * kin, axis=1, keepdims=True)
    s_new = jnp.sum(w * jnp.maximum(s_new, 0.0), axis=0, keepdims=True)
    key_new = _sortable(s_new + 0.0)
    sk = s_s[...]

    def count(pred_past, pred_new):
        c = jnp.sum(jnp.where(pred_past, 1.0, 0.0), axis=1, keepdims=True)
        return jnp.sum(c, axis=0, keepdims=True) + jnp.where(pred_new, 1.0, 0.0)

    kf = float(top_k)
    zero = jnp.zeros((1, 1), I32)
    t0 = jnp.where(count(sk >= zero, key_new >= zero) >= kf, zero, INT_MIN)

    def ge(cand):
        return count(sk >= cand, key_new >= cand) >= kf

    def bit_pair_body(s, t):
        step = lax.shift_left(jnp.int32(1), 29 - 2 * s)
        c1, c2, c3 = t + step, t + 2 * step, t + 3 * step
        return jnp.where(ge(c3), c3, jnp.where(ge(c2), c2, jnp.where(ge(c1), c1, t)))
    t = lax.fori_loop(0, 15, bit_pair_body, t0)
    t = jnp.where(ge(t + 1), t + 1, t)

    need = kf - count(sk > t, key_new > t)
    eq = sk == t
    n_eq_past = count(eq, False)
    mask_o[0] = jnp.where(sk >= t, 1.0, 0.0)
    new_o[0] = jnp.broadcast_to(jnp.where(key_new >= t, 1.0, 0.0), (1, LANE))

    n_eq = n_eq_past + jnp.where(key_new == t, 1.0, 0.0)

    @pl.when(jnp.max(n_eq - need) > 0.0)
    def _():
        pos = (lax.broadcasted_iota(I32, sk.shape, 0) * SPAN
               + lax.broadcasted_iota(I32, sk.shape, 1))
        pos_bits = max(1, (n_spans * SPAN - 1).bit_length())

        def pos_body(s, lo):
            cand = lo + lax.shift_left(jnp.int32(1), pos_bits - 1 - s)
            below = count(jnp.logical_and(eq, pos < cand), False)
            return jnp.where(below < need, cand, lo)
        last = lax.fori_loop(0, pos_bits, pos_body, jnp.zeros((1, 1), I32))
        sel = jnp.logical_or(sk > t, jnp.logical_and(eq, pos <= last))
        mask_o[0] = jnp.where(sel, 1.0, 0.0)
        new_sel = jnp.logical_or(key_new > t, jnp.logical_and(key_new == t, n_eq_past < need))
        new_o[0] = jnp.broadcast_to(jnp.where(new_sel, 1.0, 0.0), (1, LANE))


def _sample_index(page_table, qi, wi, ki_new, pool_t):
    n_b, n_pages = page_table.shape
    page = pool_t.shape[2]
    n_spans = n_pages * page // SPAN
    top_k = min(DSA_TOPK, (n_pages * page + 1) // 4)
    grid_spec = pltpu.PrefetchScalarGridSpec(
        num_scalar_prefetch=1,
        grid=(n_b,),
        in_specs=[pl.BlockSpec((1, N_IDX_HEADS, IDX_DIM), lambda b, pt: (b, 0, 0)),
                  pl.BlockSpec((1, N_IDX_HEADS, 1), lambda b, pt: (b, 0, 0)),
                  pl.BlockSpec((1, 1, IDX_DIM), lambda b, pt: (b, 0, 0)),
                  pl.BlockSpec(memory_space=pl.ANY)],
        out_specs=[pl.BlockSpec((1, n_spans, SPAN), lambda b, pt: (b, 0, 0)),
                   pl.BlockSpec((1, 1, LANE), lambda b, pt: (b, 0, 0))],
        scratch_shapes=[pltpu.VMEM((2, IDX_DIM, n_pages * page), F32),
                        pltpu.SemaphoreType.DMA((2,)),
                        pltpu.VMEM((n_spans, SPAN), I32)],
    )
    return pl.pallas_call(
        functools.partial(_sidx_kernel, n_spans=n_spans, n_pages=n_pages, page=page, top_k=top_k),
        grid_spec=grid_spec,
        out_shape=[jax.ShapeDtypeStruct((n_b, n_spans, SPAN), F32),
                   jax.ShapeDtypeStruct((n_b, 1, LANE), F32)],
        compiler_params=_cparams(("arbitrary",)),
        name="sample_indexer",
    )(page_table, qi, wi, ki_new, pool_t)


def _rows_per_kv(x):
    return jnp.concatenate(
        [jnp.broadcast_to(x[:, kh * HEAD_DIM:(kh + 1) * HEAD_DIM], (GQA, HEAD_DIM))
         for kh in range(N_KV)], axis=0)


def _sdsa_kernel(pt_ref, mask_ref, new_ref, q_ref, kn_ref, vn_ref, tail_ref, far_ref, b0_ref,
                 kpool_ref, vpool_ref, o_ref, kbuf, vbuf, sem, m_s, l_s, acc_s, *, n_spans, page):
    b = pl.program_id(0)
    j = pl.program_id(1)
    n_b = pl.num_programs(0)
    pps = SPAN // page
    step = b * n_spans + j
    slot = lax.rem(step, 2)

    def copies(bb, jj, sl):
        rows = page * N_KV
        return (_page_copies(pt_ref, bb, jj * pps, pps, kpool_ref, kbuf.at[sl], sem.at[0, sl], rows)
                + _page_copies(pt_ref, bb, jj * pps, pps, vpool_ref, vbuf.at[sl], sem.at[1, sl], rows))

    @pl.when(step == 0)
    def _():
        for cp in copies(b, j, slot):
            cp.start()

    @pl.when(step + 1 < n_b * n_spans)
    def _():
        nxt = step + 1
        for cp in copies(nxt // n_spans, lax.rem(nxt, n_spans), 1 - slot):
            cp.start()

    q = q_ref[0]

    @pl.when(j == 0)
    def _():
        qf = q.astype(F32)
        kn = _rows_per_kv(kn_ref[0].astype(BF16).astype(F32))
        lg = jnp.sum(qf * kn, axis=1, keepdims=True) + b0_ref[...]
        keep = new_ref[0][:, 0:1] > 0.0
        m_s[...] = jnp.where(keep, lg, NEG)
        l_s[...] = jnp.where(keep, jnp.ones_like(lg), 0.0)
        vn = _rows_per_kv(vn_ref[0].astype(BF16).astype(F32))
        acc_s[...] = jnp.where(keep, vn, 0.0)

    for cp in copies(b, j, slot):
        cp.wait()

    kc = kbuf[slot].astype(BF16)
    vc = vbuf[slot].astype(BF16)
    bias = jnp.where(j == n_spans - 1, tail_ref[...], far_ref[...])
    mk = jnp.logical_and(mask_ref[0, pl.ds(j, 1), :] > 0.0, _own_kv_lane((N_HEADS, SPAN_ROWS)))
    lg = jnp.where(mk, _nt(q, kc) + bias, NEG)
    m_old = m_s[...]
    m_new = jnp.maximum(m_old, jnp.max(lg, axis=1, keepdims=True))
    alpha = jnp.exp(m_old - m_new)
    p = jnp.where(mk, jnp.exp(lg - m_new), 0.0)
    l_s[...] = alpha * l_s[...] + jnp.sum(p, axis=1, keepdims=True)
    acc_s[...] = alpha * acc_s[...] + _dot(p.astype(BF16), vc)
    m_s[...] = m_new

    @pl.when(j == n_spans - 1)
    def _():
        o_ref[0] = (acc_s[...] * (1.0 / l_s[...])).astype(BF16)


def _sample_dsa(page_table, mask4, new_sel, qa, ka_new, va_new, tail4, far, bias0, k_pool, v_pool):
    n_b, n_pages = page_table.shape
    page = k_pool.shape[1] // N_KV
    n_spans = n_pages * page // SPAN
    per_b = lambda shape: pl.BlockSpec((1,) + shape, lambda b, j, pt: (b, 0, 0))
    const = lambda shape: pl.BlockSpec(shape, lambda b, j, pt: (0, 0))
    grid_spec = pltpu.PrefetchScalarGridSpec(
        num_scalar_prefetch=1,
        grid=(n_b, n_spans),
        in_specs=[per_b((n_spans, SPAN_ROWS)), per_b((1, LANE)), per_b((N_HEADS, HEAD_DIM)),
                  per_b((1, KV_W)), per_b((1, KV_W)),
                  const((N_HEADS, SPAN_ROWS)), const((N_HEADS, 1)), const((N_HEADS, 1)),
                  pl.BlockSpec(memory_space=pl.ANY), pl.BlockSpec(memory_space=pl.ANY)],
        out_specs=per_b((N_HEADS, HEAD_DIM)),
        scratch_shapes=[pltpu.VMEM((2, SPAN_ROWS, HEAD_DIM), F32),
                        pltpu.VMEM((2, SPAN_ROWS, HEAD_DIM), F32),
                        pltpu.SemaphoreType.DMA((2, 2)),
                        pltpu.VMEM((N_HEADS, 1), F32), pltpu.VMEM((N_HEADS, 1), F32),
                        pltpu.VMEM((N_HEADS, HEAD_DIM), F32)],
    )
    return pl.pallas_call(
        functools.partial(_sdsa_kernel, n_spans=n_spans, page=page),
        grid_spec=grid_spec,
        out_shape=jax.ShapeDtypeStruct((n_b, N_HEADS, HEAD_DIM), BF16),
        compiler_params=_cparams(("arbitrary", "arbitrary")),
        name="sample_dsa",
    )(page_table, mask4, new_sel, qa, ka_new, va_new, tail4, far, bias0, k_pool, v_pool)


def _smoba_gate_kernel(pt_ref, q_ref, kpool_ref, sel_o, kbuf, sem, kmean_s,
                       *, n_spans, page, n_sel, n_blocks):
    b = pl.program_id(0)
    j = pl.program_id(1)
    n_b = pl.num_programs(0)
    pps = SPAN // page
    bps = SPAN // MOBA_BLOCK
    blk_rows = MOBA_BLOCK * N_KV
    step = b * n_spans + j
    slot = lax.rem(step, 2)

    def copies(bb, jj, sl):
        return _page_copies(pt_ref, bb, jj * pps, pps, kpool_ref, kbuf.at[sl], sem.at[sl],
                            page * N_KV)

    @pl.when(step == 0)
    def _():
        for cp in copies(b, j, slot):
            cp.start()

    @pl.when(step + 1 < n_b * n_spans)
    def _():
        nxt = step + 1
        for cp in copies(nxt // n_spans, lax.rem(nxt, n_spans), 1 - slot):
            cp.start()

    for cp in copies(b, j, slot):
        cp.wait()

    means = []
    for n in range(bps):
        s8 = _fold8(kbuf[slot, n * blk_rows:(n + 1) * blk_rows, :], jnp.sum)
        means.append((s8[:N_KV] + s8[N_KV:]) * (1.0 / MOBA_BLOCK))
    kmean_s[pl.ds(pl.multiple_of(j * bps * N_KV, bps * N_KV), bps * N_KV), :] = (
        jnp.concatenate(means, axis=0))

    @pl.when(j == n_spans - 1)
    def _():
        km_hi, km_lo = _split_hi_lo(kmean_s[...])
        q = q_ref[0]
        gate = _nt(q, km_hi) + _nt(q, km_lo)
        gate = jnp.where(_own_kv_lane(gate.shape), gate, -jnp.inf)
        lane = lax.broadcasted_iota(I32, gate.shape, 1).astype(F32)
        out_lane = lax.broadcasted_iota(I32, (N_HEADS, LANE), 1)
        picked = jnp.zeros((N_HEADS, LANE), F32)
        for r in range(n_sel):
            best = jnp.max(gate, axis=1, keepdims=True)
            idx = jnp.min(jnp.where(gate == best, lane, float(n_blocks * N_KV)), axis=1, keepdims=True)
            picked = jnp.where(out_lane == r, idx, picked)
            gate = jnp.where(lane == idx, -jnp.inf, gate)
        sel_o[0] = picked.astype(I32) // N_KV


def _sample_moba_gate(page_table, qb, k_pool):
    n_b, n_pages = page_table.shape
    page = k_pool.shape[1] // N_KV
    n_spans = n_pages * page // SPAN
    n_blocks = (n_pages * page + 1) // MOBA_BLOCK
    grid_spec = pltpu.PrefetchScalarGridSpec(
        num_scalar_prefetch=1,
        grid=(n_b, n_spans),
        in_specs=[pl.BlockSpec((1, N_HEADS, HEAD_DIM), lambda b, j, pt: (b, 0, 0)),
                  pl.BlockSpec(memory_space=pl.ANY)],
        out_specs=pl.BlockSpec((1, N_HEADS, LANE), lambda b, j, pt: (b, 0, 0)),
        scratch_shapes=[pltpu.VMEM((2, SPAN_ROWS, HEAD_DIM), F32), pltpu.SemaphoreType.DMA((2,)),
                        pltpu.VMEM((n_blocks * N_KV, HEAD_DIM), F32)],
    )
    return pl.pallas_call(
        functools.partial(_smoba_gate_kernel, n_spans=n_spans, page=page,
                          n_sel=min(MOBA_TOPK, n_blocks), n_blocks=n_blocks),
        grid_spec=grid_spec,
        out_shape=jax.ShapeDtypeStruct((n_b, N_HEADS, LANE), I32),
        compiler_params=_cparams(("arbitrary", "arbitrary")),
        name="sample_moba_gate",
    )(page_table, qb, k_pool)


def _smoba_attn_kernel(pt_ref, sel_ref, selv_ref, q_ref, kn_ref, vn_ref, tail_ref, far_ref, b0_ref,
                       kpool_ref, vpool_ref, o_ref, kbuf, vbuf, sem, *, n_sel, page, n_blocks):
    b = pl.program_id(0)
    n_b = pl.num_programs(0)
    ppb = MOBA_BLOCK // page
    slot = lax.rem(b, 2)
    n_keys = n_sel * MOBA_BLOCK

    def copies(bb, sl):
        out = []
        for h in range(N_HEADS):
            kh = h // GQA
            for r in range(n_sel):
                blk = sel_ref[(bb * N_HEADS + h) * n_sel + r]
                for p in range(ppb):
                    pg = pt_ref[bb, blk * ppb + p]
                    rows = pl.ds((r * ppb + p) * page, page)
                    out.append(pltpu.make_async_copy(kpool_ref.at[pg, :, kh, :],
                                                     kbuf.at[sl, h, rows], sem.at[0, sl]))
                    out.append(pltpu.make_async_copy(vpool_ref.at[pg, :, kh, :],
                                                     vbuf.at[sl, h, rows], sem.at[1, sl]))
        return out

    @pl.when(b == 0)
    def _():
        for cp in copies(b, slot):
            cp.start()

    @pl.when(b + 1 < n_b)
    def _():
        for cp in copies(b + 1, 1 - slot):
            cp.start()

    for cp in copies(b, slot):
        cp.wait()

    q = q_ref[0]
    row = lax.broadcasted_iota(I32, (N_HEADS, n_keys), 0)
    lg = jnp.zeros((N_HEADS, n_keys), F32)
    for h in range(N_HEADS):
        lg = jnp.where(row == h, _nt(q, kbuf[slot, h].astype(BF16)), lg)
    selv = selv_ref[0]
    tail = tail_ref[...]
    far = far_ref[...]
    bias = jnp.concatenate(
        [jnp.where(selv[:, r:r + 1] == n_blocks - 1, tail, far) for r in range(n_sel)], axis=1)
    lg = lg + bias
    qf = q.astype(F32)
    kn = _rows_per_kv(kn_ref[0].astype(BF16).astype(F32))
    lg_new = jnp.sum(qf * kn, axis=1, keepdims=True) + b0_ref[...]
    m = jnp.maximum(jnp.max(lg, axis=1, keepdims=True), lg_new)
    p = jnp.exp(lg - m)
    p_new = jnp.exp(lg_new - m)
    denom = jnp.sum(p, axis=1, keepdims=True) + p_new
    pb = p.astype(BF16)
    acc = p_new.astype(BF16).astype(F32) * _rows_per_kv(vn_ref[0].astype(BF16).astype(F32))
    row_o = lax.broadcasted_iota(I32, (N_HEADS, HEAD_DIM), 0)
    for h in range(N_HEADS):
        acc = acc + jnp.where(row_o == h, _dot(pb, vbuf[slot, h].astype(BF16)), 0.0)
    o_ref[0] = (acc * (1.0 / denom)).astype(BF16)


def _sample_moba_attn(page_table, sel, qb, kb_new, vb_new, tail, far, bias0, k_pool, v_pool):
    n_b, n_pages = page_table.shape
    page = k_pool.shape[1]
    n_blocks = (n_pages * page + 1) // MOBA_BLOCK
    n_sel = min(MOBA_TOPK, n_blocks)
    sel_flat = sel[:, :, :n_sel].reshape(-1)
    per_b = lambda shape: pl.BlockSpec((1,) + shape, lambda b, pt, sf: (b, 0, 0))
    const = lambda shape: pl.BlockSpec(shape, lambda b, pt, sf: (0, 0))
    grid_spec = pltpu.PrefetchScalarGridSpec(
        num_scalar_prefetch=2,
        grid=(n_b,),
        in_specs=[per_b((N_HEADS, LANE)), per_b((N_HEADS, HEAD_DIM)), per_b((1, KV_W)),
                  per_b((1, KV_W)), const((N_HEADS, MOBA_BLOCK)), const((N_HEADS, 1)),
                  const((N_HEADS, 1)),
                  pl.BlockSpec(memory_space=pl.ANY), pl.BlockSpec(memory_space=pl.ANY)],
        out_specs=per_b((N_HEADS, HEAD_DIM)),
        scratch_shapes=[pltpu.VMEM((2, N_HEADS, n_sel * MOBA_BLOCK, HEAD_DIM), F32),
                        pltpu.VMEM((2, N_HEADS, n_sel * MOBA_BLOCK, HEAD_DIM), F32),
                        pltpu.SemaphoreType.DMA((2, 2))],
    )
    return pl.pallas_call(
        functools.partial(_smoba_attn_kernel, n_sel=n_sel, page=page, n_blocks=n_blocks),
        grid_spec=grid_spec,
        out_shape=jax.ShapeDtypeStruct((n_b, N_HEADS, HEAD_DIM), BF16),
        compiler_params=_cparams(("arbitrary",)),
        name="sample_moba_attn",
    )(page_table, sel_flat, sel, qb, kb_new, vb_new, tail, far, bias0, k_pool, v_pool)


def _split_w_in(w_in):
    widths = (Q_W, KV_W, KV_W, N_IDX_HEADS * IDX_DIM, IDX_DIM, N_IDX_HEADS, Q_W, KV_W, KV_W,
              w_in.shape[0], w_in.shape[0])
    pts = [0]
    for w in widths:
        pts.append(pts[-1] + w)
    qa, ka, va, qi, ki, wi, qb, kb, vb, ga, gb = [w_in[:, pts[n]:pts[n + 1]] for n in range(11)]
    pad = jnp.zeros((w_in.shape[0], LANE - IDX_DIM - N_IDX_HEADS), w_in.dtype)
    w_a = jnp.concatenate([qa, ka, va, qi, ki, wi, pad], axis=1).astype(BF16)
    w_b = jnp.concatenate([qb, kb, vb], axis=1).astype(BF16)
    return w_a, w_b, ga.astype(BF16), gb.astype(BF16)


def kernel(x_prompt, x_sample, cache_ka, cache_va, cache_kidx, cache_kb, cache_vb, page_table,
           c_prompt, c_sample, rel_bias, w_ada, b_ada, g_norm1, w_in, g_qa, g_ka, g_kidx, g_qb,
           g_kb, w_pa, w_pb, w_out, g_norm2, w_ffn_in, w_ffn_out):
    depth = w_in.shape[0]
    n_b, n_s, d = x_prompt.shape
    n_db, n_ds, _ = x_sample.shape
    n_pool, page = cache_ka.shape[1], cache_ka.shape[2]
    past_len = page_table.shape[1] * page
    assert n_ds == 1, "the decode kernels handle one new token per sequence"
    assert n_s % MOBA_BLOCK == 0 and past_len % SPAN == 0 and SPAN % page == 0
    assert MOBA_BLOCK % page == 0 and past_len >= 4 * DSA_TOPK

    bias_mat, bias_tail = _bias_tables(rel_bias, SPAN)
    mat_a, mat_b = bias_mat[:, :N_HEADS], bias_mat[:, N_HEADS:]
    tail = bias_tail[:, 0, :]
    tail_a4 = jnp.repeat(tail[:N_HEADS], N_KV, axis=1)
    tail_b = tail[N_HEADS:, SPAN - MOBA_BLOCK:]
    far = rel_bias[N_BUCKETS - 1].reshape(-1, 1)
    bias0 = rel_bias[0].reshape(-1, 1)

    xp = x_prompt
    xs = x_sample.reshape(1, n_db, d)
    c_all = jnp.concatenate([c_prompt, c_sample], axis=0)
    rows_p, rows_s = [], []
    for l in range(depth):
        mod = _adaln(c_all, w_ada[l], b_ada[l])
        mod_p = [m.reshape(n_b, 1, d) for m in jnp.split(mod[:n_b], 6, axis=-1)]
        mod_s = [m.reshape(1, n_db, d) for m in jnp.split(mod[n_b:], 6, axis=-1)]
        w_a, w_b, w_ga, w_gb = _split_w_in(w_in[l])
        w_pa_l, w_pb_l, w_out_l = w_pa[l].astype(BF16), w_pb[l].astype(BF16), w_out[l].astype(BF16)
        w_fi, w_fo = w_ffn_in[l].astype(BF16), w_ffn_out[l].astype(BF16)
        g_kidx_pad = jnp.pad(g_kidx[l], (0, LANE - IDX_DIM)).reshape(1, LANE)

        def project(x, mods):
            h, *dsa_side = _proj_a(x, g_norm1[l], mods[0], mods[1], w_a, g_qa[l], g_ka[l], g_kidx_pad)
            return h, dsa_side, _proj_b(h, w_b, g_qb[l], g_kb[l])

        def finish(x, h, oa, ob, mods):
            m = _gate_merge(h, oa, ob, w_ga, w_gb, w_pa_l, w_pb_l)
            x2, h2 = _out_proj(m, x, w_out_l, mods[2], g_norm2[l], mods[3], mods[4])
            return _ffn(h2, x2, w_fi, w_fo, mods[5])

        h, (qa, ka, va, qi, ki, misc, kilo, kihi), (qb, kb, vb) = project(xp, mod_p)
        oa = _dsa_prompt(qi, misc, kilo, kihi, qa, ka, va, mat_a)
        ob = _moba_prompt(qb, kb, vb, mat_b)
        xp = finish(xp, h, oa, ob, mod_p)
        rows_p.append((ka.reshape(n_b, n_s, N_KV, HEAD_DIM), va.reshape(n_b, n_s, N_KV, HEAD_DIM),
                       ki, kb.reshape(n_b, n_s, N_KV, HEAD_DIM), vb.reshape(n_b, n_s, N_KV, HEAD_DIM)))

        h, (qa, ka, va, qi, ki, misc, kilo, kihi), (qb, kb, vb) = project(xs, mod_s)
        tok = lambda t: t.reshape(n_db, 1, -1)
        heads = lambda t: t.reshape(n_db, N_HEADS, HEAD_DIM)
        rows_view = lambda c: c[l].reshape(n_pool, page * N_KV, HEAD_DIM)
        wi = misc[0, :, _WI_LANE:_WI_LANE + N_IDX_HEADS].reshape(n_db, N_IDX_HEADS, 1)
        mask, new_sel = _sample_index(page_table, qi.reshape(n_db, N_IDX_HEADS, IDX_DIM), wi,
                                      tok(ki[0]), jnp.swapaxes(cache_kidx[l], 1, 2))
        oa = _sample_dsa(page_table, jnp.repeat(mask, N_KV, axis=2), new_sel, heads(qa[0]),
                         tok(ka[0]), tok(va[0]), tail_a4, far[:N_HEADS], bias0[:N_HEADS],
                         rows_view(cache_ka), rows_view(cache_va))
        sel = _sample_moba_gate(page_table, heads(qb[0]), rows_view(cache_kb))
        ob = _sample_moba_attn(page_table, sel, heads(qb[0]), tok(kb[0]), tok(vb[0]),
                               tail_b, far[N_HEADS:], bias0[N_HEADS:], cache_kb[l], cache_vb[l])
        xs = finish(xs, h, oa.reshape(1, n_db, Q_W), ob.reshape(1, n_db, Q_W), mod_s)
        kv4 = lambda t: t.reshape(n_db, 1, N_KV, HEAD_DIM)
        rows_s.append((kv4(ka[0]), kv4(va[0]), ki.reshape(n_db, 1, IDX_DIM), kv4(kb[0]), kv4(vb[0])))

    outs_p = [jnp.stack(t) for t in zip(*rows_p)]
    outs_s = [jnp.stack(t) for t in zip(*rows_s)]
    return (xp, xs.reshape(n_db, n_ds, d), *outs_p, *outs_s)
```
